```python
import math
import jax, jax.numpy as jnp
from jax import lax
import numpy as np

D_MODEL = 1024
BATCH = 4
SEQ = 4096
DEPTH = 1
DEC_BATCH = 32
DEC_SEQ = 4
PAST_LEN = 16384
PAGE_SIZE = 128

M_HEADS = 4
M_HEAD_DIM = D_MODEL // M_HEADS
M_WIDTH = M_HEADS * M_HEAD_DIM
M_CHUNK = 64
A_HEADS = 16
A_HEAD_DIM = 64
A_KV_HEADS = 4
A_GROUP = A_HEADS // A_KV_HEADS
A_WIDTH = A_HEADS * A_HEAD_DIM
KV_WIDTH = A_KV_HEADS * A_HEAD_DIM
CMP_STRIDE = 16
CMP_BLOCK = 2 * CMP_STRIDE
SLC_BLOCK = 64
SLC_TOPK = 16
WINDOW = 512
Q_BLOCK = 64
ROPE_THETA = 500000.0
ROPE_DIM = A_HEAD_DIM // 4
BIG = 1e6
D_FF = 2816
CONV_W = 3
NORM_EPS = 1e-6
IN_WIDTH = 4 * M_WIDTH + 2 * M_HEADS + A_WIDTH + 6 * KV_WIDTH + 3 * A_HEADS + 2 * D_MODEL

kernel_name = 'hybrid_mlstm_nsa_decoder_step'


def rmsnorm(x, g):
    x = x.astype(jnp.float32)
    return x * lax.rsqrt(jnp.mean(x * x, axis=-1, keepdims=True) + NORM_EPS) * g


def rope(x, pos):
    half = ROPE_DIM // 2
    inv_freq = ROPE_THETA ** (-jnp.arange(half, dtype=jnp.float32) / half)
    ang = pos.astype(jnp.float32)[:, None] * inv_freq
    cos = jnp.cos(ang)[:, None, :]
    sin = jnp.sin(ang)[:, None, :]
    x1, x2, rest = x[..., :half], x[..., half:ROPE_DIM], x[..., ROPE_DIM:]
    return jnp.concatenate([x1 * cos - x2 * sin, x2 * cos + x1 * sin, rest], axis=-1)


def masked_softmax(s, mask):
    s = jnp.where(mask, s, -jnp.inf)
    m = jnp.max(s, axis=-1, keepdims=True)
    m = jnp.where(jnp.isfinite(m), m, 0.0)
    e = jnp.exp(s - m)
    return e / jnp.maximum(jnp.sum(e, axis=-1, keepdims=True), jnp.finfo(jnp.float32).tiny)


def split_in_proj(z):
    sizes = [M_WIDTH] * 4 + [M_HEADS, M_HEADS, A_WIDTH, 6 * KV_WIDTH, 3 * A_HEADS, D_MODEL, D_MODEL]
    cuts = [int(c) for c in np.cumsum(sizes)[:-1]]
    return jnp.split(z, cuts, axis=-1)


def mlstm(q, k, v, i_pre, f_pre, o_pre, norm_w, C0, n0, m0):
    B, T, _ = q.shape
    L = math.gcd(T, M_CHUNK)
    NC = T // L

    def heads(a):
        return a.reshape(B, NC, L, M_HEADS, M_HEAD_DIM).transpose(1, 0, 3, 2, 4)

    def gate(a):
        return a.reshape(B, NC, L, M_HEADS).transpose(1, 0, 3, 2)

    xs = (heads(q), heads(k) * M_HEAD_DIM ** -0.5, heads(v), gate(i_pre), gate(jax.nn.log_sigmoid(f_pre)))
    causal = jnp.tril(jnp.ones((L, L), dtype=bool))

    def step(carry, inp):
        C, n, m = carry
        qc, kc, vc, li, lf = inp
        b = jnp.cumsum(lf, axis=-1)
        dlog = jnp.where(causal, b[..., :, None] - b[..., None, :] + li[..., None, :], -jnp.inf)
        inter = m[..., None] + b
        mt = jnp.maximum(inter, jnp.max(dlog, axis=-1))
        a = jnp.exp(inter - mt)
        s = jnp.einsum('bhtd,bhsd->bhts', qc, kc) * jnp.exp(dlog - mt[..., None])
        num = a[..., None] * jnp.einsum('bhvk,bhtk->bhtv', C, qc) + jnp.einsum('bhts,bhsv->bhtv', s, vc)
        den = a * jnp.einsum('bhk,bhtk->bht', n, qc) + jnp.sum(s, axis=-1)
        h = num / jnp.maximum(jnp.abs(den), jnp.exp(-mt))[..., None]
        bl = b[..., -1]
        wlog = bl[..., None] - b + li
        m_new = jnp.maximum(m + bl, jnp.max(wlog, axis=-1))
        w = jnp.exp(wlog - m_new[..., None])
        decay = jnp.exp(m + bl - m_new)
        C = decay[..., None, None] * C + jnp.einsum('bhsv,bhsk->bhvk', vc * w[..., None], kc)
        n = decay[..., None] * n + jnp.einsum('bhs,bhsk->bhk', w, kc)
        return (C, n, m_new), h

    init = (C0.astype(jnp.float32), n0.astype(jnp.float32), m0.astype(jnp.float32))
    (C, n, m), hs = lax.scan(step, init, xs)
    h = hs.transpose(1, 0, 3, 2, 4)
    mu = jnp.mean(h, axis=-1, keepdims=True)
    var = jnp.mean(jnp.square(h - mu), axis=-1, keepdims=True)
    h = ((h - mu) * lax.rsqrt(var + NORM_EPS)).reshape(B, T, M_WIDTH)
    return h * norm_w * jax.nn.sigmoid(o_pre), (C, n, m)


def compress(kv, w1, pe, w2):
    B, S = kv.shape[:2]
    nseg = S // CMP_STRIDE
    seg = kv[:, :nseg * CMP_STRIDE].reshape(B, nseg, CMP_STRIDE, 2, A_KV_HEADS, A_HEAD_DIM)
    first = jnp.einsum('bnskgd,ksde->bnkge', seg[:, :-1], w1[:, :CMP_STRIDE])
    second = jnp.einsum('bnskgd,ksde->bnkge', seg[:, 1:], w1[:, CMP_STRIDE:])
    pe_bias = jnp.einsum('ksd,ksde->ke', pe, w1)
    hid = jax.nn.gelu(first + second + pe_bias[:, None, :])
    return jnp.einsum('bnkge,kef->bnkgf', hid, w2)


def to_blocks(kv):
    B, S = kv.shape[:2]
    ns = -(-S // SLC_BLOCK)
    pad = ns * SLC_BLOCK - S
    if pad:
        kv = jnp.pad(kv, ((0, 0), (0, pad), (0, 0), (0, 0), (0, 0)))
    return kv.reshape(B, ns, SLC_BLOCK, 2, A_KV_HEADS, A_HEAD_DIM)


def coverage(n_cmp, n_slc):
    cs = jnp.arange(n_cmp) * CMP_STRIDE
    ss = jnp.arange(n_slc) * SLC_BLOCK
    lo = jnp.maximum(cs[:, None], ss[None, :])
    hi = jnp.minimum(cs[:, None] + CMP_BLOCK, ss[None, :] + SLC_BLOCK)
    return (jnp.clip(hi - lo, 0, None) / CMP_BLOCK).astype(jnp.float32)


def nsa_block(q, gates, qpos, cmp_kv, slc_blocks, win_kv, win_pos):
    B, Q = q.shape[:2]
    qg = q.reshape(B, Q, A_KV_HEADS, A_GROUP, A_HEAD_DIM) * A_HEAD_DIM ** -0.5
    n_cmp = cmp_kv.shape[1]
    cmp_end = jnp.arange(n_cmp) * CMP_STRIDE + CMP_BLOCK - 1
    p_c = masked_softmax(jnp.einsum('bqgrd,bjgd->bgrqj', qg, cmp_kv[:, :, 0]), cmp_end[None, :] <= qpos[:, None])
    o_c = jnp.einsum('bgrqj,bjgd->bqgrd', p_c, cmp_kv[:, :, 1])
    n_slc = slc_blocks.shape[1]
    imp = jnp.einsum('bgrqj,js->bgqs', p_c, coverage(n_cmp, n_slc))
    blk = jnp.arange(n_slc)[None, :]
    cur = (qpos // SLC_BLOCK)[:, None]
    forced = (blk == 0) | (blk == cur) | (blk == cur - 1)
    future = blk * SLC_BLOCK > qpos[:, None]
    score = jnp.where(future, -BIG, imp + jnp.where(forced, BIG, 0.0))
    _, idx = lax.top_k(score, min(SLC_TOPK, n_slc))
    bi = jnp.arange(B)[:, None, None, None]
    gi = jnp.arange(A_KV_HEADS)[None, :, None, None]
    sel = slc_blocks[bi, idx, :, :, gi]
    kk = idx.shape[-1]
    sel_pos = idx[..., None] * SLC_BLOCK + jnp.arange(SLC_BLOCK)
    mask_s = (sel_pos <= qpos[:, None, None]).reshape(B, A_KV_HEADS, 1, Q, kk * SLC_BLOCK)
    s_s = jnp.einsum('bqgrd,bgqkld->bgrqkl', qg, sel[..., 0, :]).reshape(B, A_KV_HEADS, A_GROUP, Q, kk * SLC_BLOCK)
    p_s = masked_softmax(s_s, mask_s).reshape(B, A_KV_HEADS, A_GROUP, Q, kk, SLC_BLOCK)
    o_s = jnp.einsum('bgrqkl,bgqkld->bqgrd', p_s, sel[..., 1, :])
    mask_w = (win_pos[None, :] <= qpos[:, None]) & (win_pos[None, :] > qpos[:, None] - WINDOW) & (win_pos[None, :] >= 0)
    p_w = masked_softmax(jnp.einsum('bqgrd,bwgd->bgrqw', qg, win_kv[:, :, 0]), mask_w)
    o_w = jnp.einsum('bgrqw,bwgd->bqgrd', p_w, win_kv[:, :, 1])
    g = gates.reshape(B, Q, A_KV_HEADS, A_GROUP, 3)
    o = g[..., 0:1] * o_c + g[..., 1:2] * o_s + g[..., 2:3] * o_w
    return o.reshape(B, Q, A_WIDTH)


def nsa_prompt(q, gates, kv_cmp, kv_slc, kv_win, cmp_w1, cmp_pe, cmp_w2):
    B, T = q.shape[:2]
    cmp = compress(kv_cmp, cmp_w1, cmp_pe, cmp_w2)
    blocks = to_blocks(kv_slc)
    win_pad = jnp.pad(kv_win, ((0, 0), (WINDOW, 0), (0, 0), (0, 0), (0, 0)))
    nb = T // Q_BLOCK
    qb = q.reshape(B, nb, Q_BLOCK, A_HEADS, A_HEAD_DIM).swapaxes(0, 1)
    gb = gates.reshape(B, nb, Q_BLOCK, A_HEADS, 3).swapaxes(0, 1)
    starts = jnp.arange(nb, dtype=jnp.int32) * Q_BLOCK

    def one(args):
        qi, gi, s0 = args
        qpos = s0 + jnp.arange(Q_BLOCK, dtype=jnp.int32)
        wkv = lax.dynamic_slice_in_dim(win_pad, s0, WINDOW + Q_BLOCK, axis=1)
        wpos = s0 - WINDOW + jnp.arange(WINDOW + Q_BLOCK, dtype=jnp.int32)
        return nsa_block(qi, gi, qpos, cmp, blocks, wkv, wpos)

    out = lax.map(one, (qb, gb, starts))
    return out.swapaxes(0, 1).reshape(B, T, A_WIDTH)


def nsa_sample(q, gates, qpos, kv_cmp, kv_slc, kv_win, past_kv, cmp_w1, cmp_pe, cmp_w2):
    past_cmp, past_slc, win_buf = past_kv
    T = q.shape[1]
    full_cmp = jnp.concatenate([past_cmp, kv_cmp], axis=1)
    full_slc = jnp.concatenate([past_slc, kv_slc], axis=1)
    win = jnp.concatenate([win_buf, kv_win], axis=1)
    wb = win_buf.shape[1]
    win_pos = qpos[0] - wb + jnp.arange(wb + T, dtype=jnp.int32)
    out = nsa_block(q, gates, qpos, compress(full_cmp, cmp_w1, cmp_pe, cmp_w2), to_blocks(full_slc), win, win_pos)
    return out, win[:, -wb:]


def conv_ffn(h, prev, w_up, conv_w, conv_b, w_down):
    T = h.shape[1]
    u = h @ w_up
    buf = jnp.concatenate([prev.astype(u.dtype), u], axis=1)
    conv = conv_b
    for j in range(CONV_W):
        conv = conv + conv_w[j] * buf[:, j:j + T]
    a, g = jnp.split(conv, 2, axis=-1)
    return (jax.nn.gelu(g) * a) @ w_down, buf[:, -(CONV_W - 1):]


def decoder_layer(x, c, pos0, mstate, conv_prev, past_kv, lw):
    (w_ada, b_ada, g_pre_mix, g_post_mix, g_pre_ffn, g_post_ffn, w_in, b_in, m_norm_w,
     cmp_w1, cmp_pe, cmp_w2, w_branch_m, w_branch_a, w_out, w_up, conv_w, conv_b, w_down) = lw
    B, T, _ = x.shape
    pos = pos0 + jnp.arange(T, dtype=jnp.int32)
    mod = (jax.nn.silu(c.astype(jnp.float32)) @ w_ada + b_ada)[:, None, :]
    sh_m, sc_m, gt_m, sh_f, sc_f, gt_f = jnp.split(mod, 6, axis=-1)
    h = rmsnorm(x, g_pre_mix) * (1.0 + sc_m) + sh_m
    mq, mk, mv, mo, mi, mf, aq, akv, ag, ga, gb = split_in_proj(h @ w_in + b_in)
    hm, new_m = mlstm(mq, mk, mv, mi, mf, mo, m_norm_w, *mstate)
    q = rope(aq.reshape(B, T, A_HEADS, A_HEAD_DIM), pos)
    kv = akv.reshape(B, T, 3, 2, A_KV_HEADS, A_HEAD_DIM)
    k_rot = rope(kv[:, :, :, 0].reshape(B, T, 3 * A_KV_HEADS, A_HEAD_DIM), pos).reshape(B, T, 3, A_KV_HEADS, A_HEAD_DIM)
    kv = jnp.stack([k_rot, kv[:, :, :, 1]], axis=3)
    kv_cmp, kv_slc, kv_win = kv[:, :, 0], kv[:, :, 1], kv[:, :, 2]
    gates = jax.nn.sigmoid(ag).reshape(B, T, A_HEADS, 3)
    if past_kv is None:
        ha = nsa_prompt(q, gates, kv_cmp, kv_slc, kv_win, cmp_w1, cmp_pe, cmp_w2)
        win_new = kv_win[:, -min(WINDOW, T):]
    else:
        ha, win_new = nsa_sample(q, gates, pos, kv_cmp, kv_slc, kv_win, past_kv, cmp_w1, cmp_pe, cmp_w2)
    mixed = jax.nn.sigmoid(ga) * (hm @ w_branch_m) + jax.nn.sigmoid(gb) * (ha @ w_branch_a)
    x = x + gt_m * rmsnorm(mixed @ w_out, g_post_mix)
    h = rmsnorm(x, g_pre_ffn) * (1.0 + sc_f) + sh_f
    f, conv_new = conv_ffn(h, conv_prev, w_up, conv_w, conv_b, w_down)
    x = x + gt_f * rmsnorm(f, g_post_ffn)
    return x, (kv_cmp, kv_slc, win_new, new_m[0], new_m[1], new_m[2], conv_new)


def setup_inputs(seed: int = 0) -> dict:
    key = jax.random.key(seed)
    keys = iter(jax.random.split(key, 40))

    def nrm(shape, scale):
        return scale * jax.random.normal(next(keys), shape, dtype=jnp.float32)

    def gain(shape):
        return 1.0 + nrm(shape, 0.02)

    n_pages = PAST_LEN // PAGE_SIZE
    n_pool = (DEC_BATCH * n_pages * 5) // 4
    win_buf = min(WINDOW, PAST_LEN)
    kv_row = (2, A_KV_HEADS, A_HEAD_DIM)
    page_table = jax.random.permutation(next(keys), n_pool)[:DEC_BATCH * n_pages].reshape(DEC_BATCH, n_pages).astype(jnp.int32)
    f_off = 4 * M_WIDTH + M_HEADS
    b_in = nrm((DEPTH, IN_WIDTH), 0.02).at[:, f_off:f_off + M_HEADS].add(jnp.linspace(3.0, 6.0, M_HEADS))
    return {
        'x_prompt': nrm((BATCH, SEQ, D_MODEL), 1.0),
        'x_sample': nrm((DEC_BATCH, DEC_SEQ, D_MODEL), 1.0),
        'cache_cmp_kv': nrm((DEPTH, n_pool, PAGE_SIZE) + kv_row, 1.0),
        'cache_slc_kv': nrm((DEPTH, n_pool, PAGE_SIZE) + kv_row, 1.0),
        'cache_win_kv': nrm((DEPTH, DEC_BATCH, win_buf) + kv_row, 1.0),
        'state_C': nrm((DEPTH, DEC_BATCH, M_HEADS, M_HEAD_DIM, M_HEAD_DIM), 0.05),
        'state_n': nrm((DEPTH, DEC_BATCH, M_HEADS, M_HEAD_DIM), 0.05),
        'state_m': nrm((DEPTH, DEC_BATCH, M_HEADS), 0.5),
        'state_conv': nrm((DEPTH, DEC_BATCH, CONV_W - 1, 2 * D_FF), 1.0),
        'page_table': page_table,
        'c_prompt': nrm((BATCH, D_MODEL), 1.0),
        'c_sample': nrm((DEC_BATCH, D_MODEL), 1.0),
        'w_ada': nrm((DEPTH, D_MODEL, 6 * D_MODEL), 0.5 * D_MODEL ** -0.5),
        'b_ada': nrm((DEPTH, 6 * D_MODEL), 0.02),
        'g_pre_mix': gain((DEPTH, D_MODEL)),
        'g_post_mix': gain((DEPTH, D_MODEL)),
        'g_pre_ffn': gain((DEPTH, D_MODEL)),
        'g_post_ffn': gain((DEPTH, D_MODEL)),
        'w_in': nrm((DEPTH, D_MODEL, IN_WIDTH), D_MODEL ** -0.5),
        'b_in': b_in,
        'm_norm_w': gain((DEPTH, M_WIDTH)),
        'cmp_w1': nrm((DEPTH, 2, CMP_BLOCK, A_HEAD_DIM, A_HEAD_DIM), (CMP_BLOCK * A_HEAD_DIM) ** -0.5),
        'cmp_pe': nrm((DEPTH, 2, CMP_BLOCK, A_HEAD_DIM), 0.1),
        'cmp_w2': nrm((DEPTH, 2, A_HEAD_DIM, A_HEAD_DIM), A_HEAD_DIM ** -0.5),
        'w_branch_m': nrm((DEPTH, M_WIDTH, D_MODEL), M_WIDTH ** -0.5),
        'w_branch_a': nrm((DEPTH, A_WIDTH, D_MODEL), A_WIDTH ** -0.5),
        'w_out': nrm((DEPTH, D_MODEL, D_MODEL), D_MODEL ** -0.5),
        'w_up': nrm((DEPTH, D_MODEL, 2 * D_FF), D_MODEL ** -0.5),
        'conv_w': nrm((DEPTH, CONV_W, 2 * D_FF), CONV_W ** -0.5),
        'conv_b': nrm((DEPTH, 2 * D_FF), 0.02),
        'w_down': nrm((DEPTH, D_FF, D_MODEL), D_FF ** -0.5),
    }


def reference(x_prompt, x_sample, cache_cmp_kv, cache_slc_kv, cache_win_kv, state_C, state_n, state_m,
              state_conv, page_table, c_prompt, c_sample, w_ada, b_ada, g_pre_mix, g_post_mix, g_pre_ffn,
              g_post_ffn, w_in, b_in, m_norm_w, cmp_w1, cmp_pe, cmp_w2, w_branch_m, w_branch_a, w_out,
              w_up, conv_w, conv_b, w_down):
    past = page_table.shape[1] * PAGE_SIZE
    xp = x_prompt.astype(jnp.float32)
    xs = x_sample.astype(jnp.float32)
    B = xp.shape[0]
    DB = xs.shape[0]
    zero_mstate = (jnp.zeros((B, M_HEADS, M_HEAD_DIM, M_HEAD_DIM), jnp.float32),
                   jnp.zeros((B, M_HEADS, M_HEAD_DIM), jnp.float32),
                   jnp.zeros((B, M_HEADS), jnp.float32))
    zero_conv = jnp.zeros((B, CONV_W - 1, 2 * D_FF), jnp.float32)
    p_states, s_states = [], []
    for l in range(DEPTH):
        lw = (w_ada[l], b_ada[l], g_pre_mix[l], g_post_mix[l], g_pre_ffn[l], g_post_ffn[l], w_in[l], b_in[l],
              m_norm_w[l], cmp_w1[l], cmp_pe[l], cmp_w2[l], w_branch_m[l], w_branch_a[l], w_out[l], w_up[l],
              conv_w[l], conv_b[l], w_down[l])
        xp, sp = decoder_layer(xp, c_prompt, 0, zero_mstate, zero_conv, None, lw)
        past_cmp = cache_cmp_kv[l, page_table].reshape(DB, past, 2, A_KV_HEADS, A_HEAD_DIM)
        past_slc = cache_slc_kv[l, page_table].reshape(DB, past, 2, A_KV_HEADS, A_HEAD_DIM)
        xs, ss = decoder_layer(xs, c_sample, past, (state_C[l], state_n[l], state_m[l]), state_conv[l],
                               (past_cmp, past_slc, cache_win_kv[l]), lw)
        p_states.append(sp)
        s_states.append(ss)
    p_cmp, p_slc, p_win, p_C, p_n, p_m, p_conv = [jnp.stack([s[i] for s in p_states]) for i in range(7)]
    s_cmp, s_slc, s_win, s_C, s_n, s_m, s_conv = [jnp.stack([s[i] for s in s_states]) for i in range(7)]
    return (xp, xs, p_cmp, p_slc, p_win, p_C, p_n, p_m, p_conv, s_cmp, s_slc, s_win, s_C, s_n, s_m, s_conv)
```

```python
import functools
import math

import numpy as np
import jax
import jax.numpy as jnp
from jax import lax
from jax.experimental import pallas as pl
from jax.experimental.pallas import tpu as pltpu

F32 = jnp.float32
BF16 = jnp.bfloat16

D_MODEL = 1024
M_HEADS = 4
M_HEAD_DIM = 256
A_HEADS = 16
A_HEAD_DIM = 64
A_KV_HEADS = 4
A_GROUP = 4
KV_WIDTH = A_KV_HEADS * A_HEAD_DIM
CMP_STRIDE = 16
CMP_BLOCK = 32
SLC_BLOCK = 64
SLC_TOPK = 16
WINDOW = 512
ROPE_THETA = 500000.0
ROPE_DIM = 16
BIG = 1e6
D_FF = 2816
CONV_W = 3
NORM_EPS = 1e-6
PAGE_SIZE = 128
NEG_INF = float("-inf")
TINY = float(np.finfo(np.float32).tiny)

Z_MQ, Z_MK, Z_MV, Z_MO, Z_AQ, Z_GA, Z_GB, Z_AKV, Z_SMALL = 0, 1024, 2048, 3072, 4096, 5120, 6144, 7168, 8704
Z_WIDTH = 9216
SMALL_MI, SMALL_MF, SMALL_AG = 0, 4, 8

VMEM_LIMIT = 48 * 1024 * 1024


def _cparams(sem):
    return pltpu.CompilerParams(dimension_semantics=sem, vmem_limit_bytes=VMEM_LIMIT)


def _dot(a, b):
    return jnp.dot(a.astype(BF16), b.astype(BF16), preferred_element_type=F32)


def _dot_nt(a, b):
    return lax.dot_general(a.astype(BF16), b.astype(BF16), (((1,), (1,)), ((), ())), preferred_element_type=F32)


def _dot_tn(a, b):
    return lax.dot_general(a.astype(BF16), b.astype(BF16), (((0,), (0,)), ((), ())), preferred_element_type=F32)


def _split3(x):
    x1 = x.astype(BF16)
    r1 = x - x1.astype(F32)
    x2 = r1.astype(BF16)
    x3 = (r1 - x2.astype(F32)).astype(BF16)
    return x1, x2, x3


def _rms(x, g):
    return x * lax.rsqrt(jnp.mean(x * x, axis=-1, keepdims=True) + NORM_EPS) * g


def _ada_kernel(c_ref, w_ref, b_ref, o_ref):
    c = c_ref[...]
    o_ref[...] = _dot(c * jax.nn.sigmoid(c), w_ref[...]) + b_ref[...]


def ada_modulation(c, w_ada, b_ada):
    rows, d = c.shape
    n = w_ada.shape[1]
    tn = 512
    return pl.pallas_call(
        _ada_kernel,
        out_shape=jax.ShapeDtypeStruct((rows, n), F32),
        grid=(n // tn,),
        in_specs=[pl.BlockSpec((rows, d), lambda j: (0, 0)),
                  pl.BlockSpec((d, tn), lambda j: (0, j)),
                  pl.BlockSpec((1, tn), lambda j: (0, j))],
        out_specs=pl.BlockSpec((rows, tn), lambda j: (0, j)),
        compiler_params=_cparams(("arbitrary",)),
        name="ada",
    )(c, w_ada, b_ada.reshape(1, n))


def _inproj_kernel(x_ref, g_ref, sc_ref, sh_ref, w_ref, b_ref, o_ref, h_ref):
    @pl.when(pl.program_id(2) == 0)
    def _():
        h = _rms(x_ref[0], g_ref[...]) * (1.0 + sc_ref[0]) + sh_ref[0]
        h_ref[...] = h.astype(BF16)

    o_ref[0] = jnp.dot(h_ref[...], w_ref[...], preferred_element_type=F32) + b_ref[...]


def in_projection(x, g, sc, sh, w_bf16, b, tm):
    bsz, t, d = x.shape
    r = sc.shape[1]
    rb = 1 if r == 1 else tm
    tn = 1024
    mod_spec = pl.BlockSpec((1, rb, d), (lambda b_, i, j: (b_, 0, 0)) if r == 1 else (lambda b_, i, j: (b_, i, 0)))
    return pl.pallas_call(
        _inproj_kernel,
        out_shape=jax.ShapeDtypeStruct((bsz, t, Z_WIDTH), F32),
        grid=(bsz, t // tm, Z_WIDTH // tn),
        in_specs=[pl.BlockSpec((1, tm, d), lambda b_, i, j: (b_, i, 0)),
                  pl.BlockSpec((1, d), lambda b_, i, j: (0, 0)),
                  mod_spec, mod_spec,
                  pl.BlockSpec((d, tn), lambda b_, i, j: (0, j)),
                  pl.BlockSpec((1, tn), lambda b_, i, j: (0, j))],
        out_specs=pl.BlockSpec((1, tm, tn), lambda b_, i, j: (b_, i, j)),
        scratch_shapes=[pltpu.VMEM((tm, d), BF16)],
        compiler_params=_cparams(("arbitrary", "arbitrary", "arbitrary")),
        name="inproj",
    )(x, g.reshape(1, d), sc, sh, w_bf16, b.reshape(1, Z_WIDTH))


def regroup_in_weights(w_in, b_in):
    mw = M_HEADS * M_HEAD_DIM
    o_mi = 4 * mw
    o_aq = o_mi + 2 * M_HEADS
    o_akv = o_aq + A_HEADS * A_HEAD_DIM
    o_ag = o_akv + 6 * KV_WIDTH
    o_ga = o_ag + 3 * A_HEADS
    o_gb = o_ga + D_MODEL

    def regroup(a):
        lead = a.shape[:-1]
        parts = [a[..., :o_mi], a[..., o_aq:o_akv], a[..., o_ga:o_gb], a[..., o_gb:o_gb + D_MODEL],
                 a[..., o_akv:o_ag], a[..., o_mi:o_aq], a[..., o_ag:o_ga],
                 jnp.zeros(lead + (128 - 2 * M_HEADS - 3 * A_HEADS,), a.dtype),
                 jnp.zeros(lead + (Z_WIDTH - Z_SMALL - 128,), a.dtype)]
        return jnp.concatenate(parts, axis=-1)

    return regroup(w_in).astype(BF16), regroup(b_in)


def rope_tables(pos):
    half = ROPE_DIM // 2
    inv_freq = ROPE_THETA ** (-jnp.arange(half, dtype=F32) / half)
    ang = pos.astype(F32)[:, None] * inv_freq
    cos, sin = jnp.cos(ang), jnp.sin(ang)
    rows = pos.shape[0]
    zeros = jnp.zeros((rows, half), F32)
    rest1 = jnp.ones((rows, A_HEAD_DIM - ROPE_DIM), F32)
    rest0 = jnp.zeros((rows, A_HEAD_DIM - ROPE_DIM), F32)
    c = jnp.concatenate([cos, cos, rest1], axis=1)
    sa = jnp.concatenate([zeros, sin, rest0], axis=1)
    sb = jnp.concatenate([-sin, zeros, rest0], axis=1)
    return tuple(jnp.concatenate([a, a], axis=1) for a in (c, sa, sb))


def _rope_apply(x, c, sa, sb):
    w = x.shape[1]
    n = w // 128
    ct, sat, sbt = (jnp.concatenate([a] * n, axis=1) for a in (c, sa, sb))
    return x * ct + pltpu.roll(x, ROPE_DIM // 2, 1) * sat + pltpu.roll(x, w - ROPE_DIM // 2, 1) * sbt


def _rope_kernel(q_ref, c_ref, s_ref, w_ref, cos_ref, sa_ref, sb_ref, qo_ref, co_ref, so_ref, wo_ref):
    c, sa, sb = cos_ref[...], sa_ref[...], sb_ref[...]
    qo_ref[0] = _rope_apply(q_ref[0], c, sa, sb)
    for src, dst in ((c_ref, co_ref), (s_ref, so_ref), (w_ref, wo_ref)):
        kv = src[0]
        dst[0] = jnp.concatenate([_rope_apply(kv[:, :KV_WIDTH], c, sa, sb), kv[:, KV_WIDTH:]], axis=1)


def rope_split(z, tables, tm):
    bsz, t, _ = z.shape
    nt = t // tm
    kvw = 2 * KV_WIDTH
    tab_spec = pl.BlockSpec((tm, 128), lambda b_, i: (i, 0))
    return pl.pallas_call(
        _rope_kernel,
        out_shape=(jax.ShapeDtypeStruct((bsz, t, D_MODEL), F32),) + (jax.ShapeDtypeStruct((bsz, t, kvw), F32),) * 3,
        grid=(bsz, nt),
        in_specs=[pl.BlockSpec((1, tm, D_MODEL), lambda b_, i: (b_, i, Z_AQ // D_MODEL)),
                  pl.BlockSpec((1, tm, kvw), lambda b_, i: (b_, i, Z_AKV // kvw)),
                  pl.BlockSpec((1, tm, kvw), lambda b_, i: (b_, i, Z_AKV // kvw + 1)),
                  pl.BlockSpec((1, tm, kvw), lambda b_, i: (b_, i, Z_AKV // kvw + 2)),
                  tab_spec, tab_spec, tab_spec],
        out_specs=(pl.BlockSpec((1, tm, D_MODEL), lambda b_, i: (b_, i, 0)),) +
                  (pl.BlockSpec((1, tm, kvw), lambda b_, i: (b_, i, 0)),) * 3,
        compiler_params=_cparams(("arbitrary", "arbitrary")),
        name="rope",
    )(z, z, z, z, *tables)


def _mlstm_kernel(q_ref, k_ref, v_ref, o_ref, s_ref, nw_ref, c0_ref, n0_ref, m0_ref,
                  h_ref, c_ref, n_ref, m_ref, *, lb, lp, t_real):
    @pl.when(pl.program_id(1) == 0)
    def _():
        c_ref[...] = c0_ref[...]
        n_ref[...] = n0_ref[...]
        m_ref[...] = m0_ref[...]

    def pad(a):
        if lb == lp:
            return a
        return jnp.concatenate([a, jnp.zeros((lp - lb, a.shape[1]), a.dtype)], axis=0)

    small = pad(s_ref[0])
    small_t = small.T
    row_c = lax.broadcasted_iota(jnp.int32, (lp, 1), 0)
    row_r = lax.broadcasted_iota(jnp.int32, (1, lp), 1)
    li_col_all = jnp.where(row_c < t_real, small, NEG_INF)
    lf_col_all = jnp.where(row_c < t_real, jax.nn.log_sigmoid(small), 0.0)
    li_row_all = jnp.where(row_r < t_real, small_t[0:8], NEG_INF)
    lf_row_all = jnp.where(row_r < t_real, jax.nn.log_sigmoid(small_t[0:8]), 0.0)
    rr = lax.broadcasted_iota(jnp.int32, (lp, lp), 0)
    cc = lax.broadcasted_iota(jnp.int32, (lp, lp), 1)
    causal = cc <= rr
    tril = jnp.where(causal, 1.0, 0.0).astype(BF16)
    triu = jnp.where(rr <= cc, 1.0, 0.0).astype(BF16)
    b_col_all = sum(jnp.dot(tril, p, preferred_element_type=F32) for p in _split3(lf_col_all))
    b_row_all = sum(jnp.dot(p, triu, preferred_element_type=F32) for p in _split3(lf_row_all))

    q_all, k_all, v_all, o_all = pad(q_ref[0]), pad(k_ref[0]), pad(v_ref[0]), pad(o_ref[0])
    nw = nw_ref[...]
    for h in range(M_HEADS):
        hs = slice(h * M_HEAD_DIM, (h + 1) * M_HEAD_DIM)
        qf = q_all[:, hs]
        kf = k_all[:, hs] * (M_HEAD_DIM ** -0.5)
        vf = v_all[:, hs]
        li_row = li_row_all[SMALL_MI + h:SMALL_MI + h + 1, :]
        b_row = b_row_all[SMALL_MF + h:SMALL_MF + h + 1, :]
        li_col = li_col_all[:, SMALL_MI + h:SMALL_MI + h + 1]
        b_col = b_col_all[:, SMALL_MF + h:SMALL_MF + h + 1]
        m_prev = m_ref[0, h]
        c_prev = c_ref[0, h]
        n_prev = n_ref[0, h]

        dlog = jnp.where(causal, b_col - b_row + li_row, NEG_INF)
        inter = m_prev + b_col
        mt = jnp.maximum(inter, jnp.max(dlog, axis=1, keepdims=True))
        a = jnp.exp(inter - mt)
        s = _dot_nt(qf, kf) * jnp.exp(dlog - mt)
        num = a * _dot_nt(qf, c_prev) + _dot(s, vf)
        den = a * jnp.sum(qf * n_prev, axis=1, keepdims=True) + jnp.sum(s, axis=1, keepdims=True)
        hh = num / jnp.maximum(jnp.abs(den), jnp.exp(-mt))
        mu = jnp.mean(hh, axis=1, keepdims=True)
        var = jnp.mean(jnp.square(hh - mu), axis=1, keepdims=True)
        out = (hh - mu) * lax.rsqrt(var + NORM_EPS) * nw[:, hs] * jax.nn.sigmoid(o_all[:, hs])
        h_ref[0, :, hs] = out[:lb]

        bl = b_row[:, lp - 1:lp]
        wlog = bl - b_col + li_col
        m_new = jnp.maximum(m_prev + bl, jnp.max(wlog, axis=0, keepdims=True))
        w = jnp.exp(wlog - m_new)
        decay = jnp.exp(m_prev + bl - m_new)
        c_ref[0, h] = decay * c_prev + _dot_tn(vf * w, kf)
        n_ref[0, h] = decay * n_prev + jnp.sum(w * kf, axis=0, keepdims=True)
        m_ref[0, h] = m_new


def mlstm(z, norm_w, c0, n0, m0, lb, lp, t_real):
    bsz, t, _ = z.shape
    nc = t // lb
    mw = M_HEADS * M_HEAD_DIM
    zspec = lambda col: pl.BlockSpec((1, lb, mw), lambda b_, c: (b_, c, col // mw))
    cst = lambda shape: pl.BlockSpec((1,) + shape, lambda b_, c: (b_,) + (0,) * len(shape))
    h, c, n, m = pl.pallas_call(
        functools.partial(_mlstm_kernel, lb=lb, lp=lp, t_real=t_real),
        out_shape=(jax.ShapeDtypeStruct((bsz, t, mw), F32),
                   jax.ShapeDtypeStruct((bsz, M_HEADS, M_HEAD_DIM, M_HEAD_DIM), F32),
                   jax.ShapeDtypeStruct((bsz, M_HEADS, 1, M_HEAD_DIM), F32),
                   jax.ShapeDtypeStruct((bsz, M_HEADS, 1, 1), F32)),
        grid=(bsz, nc),
        in_specs=[zspec(Z_MQ), zspec(Z_MK), zspec(Z_MV), zspec(Z_MO),
                  pl.BlockSpec((1, lb, 128), lambda b_, c: (b_, c, Z_SMALL // 128)),
                  pl.BlockSpec((1, mw), lambda b_, c: (0, 0)),
                  cst((M_HEADS, M_HEAD_DIM, M_HEAD_DIM)), cst((M_HEADS, 1, M_HEAD_DIM)), cst((M_HEADS, 1, 1))],
        out_specs=(pl.BlockSpec((1, lb, mw), lambda b_, c: (b_, c, 0)),
                   cst((M_HEADS, M_HEAD_DIM, M_HEAD_DIM)), cst((M_HEADS, 1, M_HEAD_DIM)), cst((M_HEADS, 1, 1))),
        compiler_params=_cparams(("arbitrary", "arbitrary")),
        name="mlstm",
    )(z, z, z, z, z, norm_w.reshape(1, mw), c0, n0.reshape(bsz, M_HEADS, 1, M_HEAD_DIM),
      m0.reshape(bsz, M_HEADS, 1, 1))
    return h, c, n.reshape(bsz, M_HEADS, M_HEAD_DIM), m.reshape(bsz, M_HEADS)


def _mix_kernel(x_ref, hm_ref, ha_ref, ga_ref, gb_ref, gt_ref, g_ref, wm_ref, wa_ref, wo_ref, o_ref):
    mixed = (jax.nn.sigmoid(ga_ref[0]) * _dot(hm_ref[0], wm_ref[...]) +
             jax.nn.sigmoid(gb_ref[0]) * _dot(ha_ref[0], wa_ref[...]))
    o_ref[0] = x_ref[0] + gt_ref[0] * _rms(_dot(mixed, wo_ref[...]), g_ref[...])


def mix_out(x, hm, ha, z, gt, g_post, wm, wa, wo, tm):
    bsz, t, d = x.shape
    r = gt.shape[1]
    rb = 1 if r == 1 else tm
    row = lambda col=0: pl.BlockSpec((1, tm, d), lambda b_, i: (b_, i, col // d))
    mod = pl.BlockSpec((1, rb, d), (lambda b_, i: (b_, 0, 0)) if r == 1 else (lambda b_, i: (b_, i, 0)))
    wsp = pl.BlockSpec((d, d), lambda b_, i: (0, 0))
    return pl.pallas_call(
        _mix_kernel,
        out_shape=jax.ShapeDtypeStruct((bsz, t, d), F32),
        grid=(bsz, t // tm),
        in_specs=[row(), row(), row(), row(Z_GA), row(Z_GB), mod,
                  pl.BlockSpec((1, d), lambda b_, i: (0, 0)), wsp, wsp, wsp],
        out_specs=row(),
        compiler_params=_cparams(("arbitrary", "arbitrary")),
        name="mix",
    )(x, hm, ha, z, z, gt, g_post.reshape(1, d), wm, wa, wo)


FF_CHUNK = 256


def _ffn_kernel(*refs, per_row_state):
    if per_row_state:
        (x_ref, g_ref, sc_ref, sh_ref, gt_ref, gp_ref, wa_ref, wg_ref, cwa_ref, cwg_ref, cba_ref, cbg_ref, wd_ref,
         s1a_ref, s1g_ref, s2a_ref, s2g_ref, y_ref, ua_ref, ug_ref, h_ref, acc_ref) = refs
    else:
        (x_ref, g_ref, sc_ref, sh_ref, gt_ref, gp_ref, wa_ref, wg_ref, cwa_ref, cwg_ref, cba_ref, cbg_ref, wd_ref,
         y_ref, ua_ref, ug_ref, h_ref, acc_ref, carry_ref) = refs
    i = pl.program_id(1)
    f = pl.program_id(2)
    tm = x_ref.shape[1]

    @pl.when(f == 0)
    def _():
        h = _rms(x_ref[0], g_ref[...]) * (1.0 + sc_ref[0]) + sh_ref[0]
        h_ref[...] = h.astype(BF16)
        acc_ref[...] = jnp.zeros_like(acc_ref)

    if not per_row_state:
        @pl.when(i == 0)
        def _():
            carry_ref[f] = jnp.zeros(carry_ref.shape[1:], F32)

    row = lax.broadcasted_iota(jnp.int32, (tm, 1), 0)

    def branch(w_ref, cw_ref, cb_ref, part, s1_ref=None, s2_ref=None):
        u = jnp.dot(h_ref[...], w_ref[...], preferred_element_type=F32)
        r1 = pltpu.roll(u, 1, 0)
        r2 = pltpu.roll(u, 2, 0)
        if per_row_state:
            t = row % 8
            u1 = jnp.where(t < 1, s1_ref[0], r1)
            u2 = jnp.where(t < 2, s2_ref[0], r2)
        else:
            prev = carry_ref[f, part]
            u1 = jnp.where(row < 1, prev[1:2], r1)
            u2 = jnp.where(row < 1, prev[0:1], jnp.where(row < 2, prev[1:2], r2))
            carry_ref[f, part, 0:2] = u[tm - 2:tm]
        cw = cw_ref[...]
        return u, cb_ref[...] + cw[0:1] * u2 + cw[1:2] * u1 + cw[2:3] * u

    if per_row_state:
        u_a, conv_a = branch(wa_ref, cwa_ref, cba_ref, 0, s1a_ref, s2a_ref)
        u_g, conv_g = branch(wg_ref, cwg_ref, cbg_ref, 1, s1g_ref, s2g_ref)
        ua_ref[0] = u_a
        ug_ref[0] = u_g
    else:
        u_a, conv_a = branch(wa_ref, cwa_ref, cba_ref, 0)
        u_g, conv_g = branch(wg_ref, cwg_ref, cbg_ref, 1)
        ua_ref[0, 0] = u_a[tm - 2:tm]
        ug_ref[0, 0] = u_g[tm - 2:tm]
    acc_ref[...] += _dot(jax.nn.gelu(conv_g) * conv_a, wd_ref[...])

    @pl.when(f == pl.num_programs(2) - 1)
    def _():
        y_ref[0] = x_ref[0] + gt_ref[0] * _rms(acc_ref[...], gp_ref[...])


def conv_ffn(x, g_pre, sc, sh, gt, g_post, w_up, conv_w, conv_b, w_down, tm, state_rows=None):
    bsz, t, d = x.shape
    r = sc.shape[1]
    rb = 1 if r == 1 else tm
    ck = FF_CHUNK
    nf = D_FF // ck
    per_row = state_rows is not None
    mod = pl.BlockSpec((1, rb, d), (lambda b_, i, f: (b_, 0, 0)) if r == 1 else (lambda b_, i, f: (b_, i, 0)))
    xrow = pl.BlockSpec((1, tm, d), lambda b_, i, f: (b_, i, 0))
    vec = pl.BlockSpec((1, d), lambda b_, i, f: (0, 0))
    col_a = lambda rows: pl.BlockSpec((rows, ck), lambda b_, i, f: (0, f))
    col_g = lambda rows: pl.BlockSpec((rows, ck), lambda b_, i, f: (0, nf + f))
    in_specs = [xrow, vec, mod, mod, mod, vec, col_a(d), col_g(d), col_a(CONV_W), col_g(CONV_W), col_a(1), col_g(1),
                pl.BlockSpec((ck, d), lambda b_, i, f: (f, 0))]
    args = [x, g_pre.reshape(1, d), sc, sh, gt, g_post.reshape(1, d), w_up, w_up, conv_w, conv_w,
            conv_b.reshape(1, 2 * D_FF), conv_b.reshape(1, 2 * D_FF), w_down]
    scratch = [pltpu.VMEM((tm, d), BF16), pltpu.VMEM((tm, d), F32)]
    if per_row:
        urows = tm
        st_a = pl.BlockSpec((1, tm, ck), lambda b_, i, f: (b_, i, f))
        st_g = pl.BlockSpec((1, tm, ck), lambda b_, i, f: (b_, i, nf + f))
        in_specs += [st_a, st_g, st_a, st_g]
        args += [state_rows[0], state_rows[0], state_rows[1], state_rows[1]]
        u_shape = jax.ShapeDtypeStruct((bsz, t, D_FF), F32)
        u_spec = pl.BlockSpec((1, tm, ck), lambda b_, i, f: (b_, i, f))
    else:
        u_shape = jax.ShapeDtypeStruct((bsz, t // tm, CONV_W - 1, D_FF), F32)
        u_spec = pl.BlockSpec((1, 1, CONV_W - 1, ck), lambda b_, i, f: (b_, i, 0, f))
        scratch.append(pltpu.VMEM((nf, 2, 8, ck), F32))
    y, ua, ug = pl.pallas_call(
        functools.partial(_ffn_kernel, per_row_state=per_row),
        out_shape=(jax.ShapeDtypeStruct((bsz, t, d), F32), u_shape, u_shape),
        grid=(bsz, t // tm, nf),
        in_specs=in_specs,
        out_specs=(xrow, u_spec, u_spec),
        scratch_shapes=scratch,
        compiler_params=_cparams(("arbitrary", "arbitrary", "arbitrary")),
        name="ffn",
    )(*args)
    if not per_row:
        ua, ug = ua[:, -1], ug[:, -1]
    return y, jnp.concatenate([ua, ug], axis=-1)


LANE = 128
QUARTERS = 2 * KV_WIDTH // LANE
HEADS_PER_LANE_ROW = LANE // A_HEAD_DIM


def compress_weights(cmp_w1, cmp_pe, cmp_w2):
    eye = jnp.eye(HEADS_PER_LANE_ROW, dtype=F32)
    bd = lambda w: jnp.einsum("gh,...de->...gdhe", eye, w).reshape(w.shape[:-2] + (LANE, LANE))
    w1ab = jnp.concatenate([bd(cmp_w1[:, :CMP_STRIDE]), bd(cmp_w1[:, CMP_STRIDE:])], axis=-1).astype(BF16)
    w2 = bd(cmp_w2).astype(BF16)
    w1r = cmp_w1.reshape(2, CMP_BLOCK * A_HEAD_DIM, A_HEAD_DIM)
    pe = cmp_pe.reshape(2, CMP_BLOCK * A_HEAD_DIM, 1)
    return w1ab, w2, w1r, pe


def _compress_chunk(get_x, nrows, w1_ref, w2_ref, w1r_ref, pe_ref, carry_ref, out_ref):
    row = lax.broadcasted_iota(jnp.int32, (nrows, 1), 0)
    for kind in range(2):
        peb = jnp.sum(pe_ref[kind] * w1r_ref[kind], axis=0, keepdims=True)
        peb = jnp.concatenate([peb] * HEADS_PER_LANE_ROW, axis=1)
        for half in range(2):
            acc = jnp.zeros((nrows, 2 * LANE), F32)
            for s in range(CMP_STRIDE):
                x = get_x(2 * kind + half, s).astype(BF16)
                acc = acc + jnp.dot(x, w1_ref[kind, s], preferred_element_type=F32)
            acc_a, acc_b = acc[:, :LANE], acc[:, LANE:]
            a_shift = jnp.where(row == 0, carry_ref[kind, half], pltpu.roll(acc_a, 1, 0))
            carry_ref[kind, half] = acc_a[nrows - 1:nrows]
            hid = jax.nn.gelu(a_shift + acc_b + peb)
            out_ref[0, kind, :, half * LANE:(half + 1) * LANE] = _dot(hid, w2_ref[kind])


def _cmp_prompt_kernel(kv_ref, w1_ref, w2_ref, w1r_ref, pe_ref, out_ref, carry_ref):
    nseg = kv_ref.shape[1] // (CMP_STRIDE * QUARTERS)
    carry_ref[...] = jnp.zeros_like(carry_ref)
    get_x = lambda quarter, s: kv_ref[0, pl.ds(QUARTERS * s + quarter, nseg, stride=CMP_STRIDE * QUARTERS), :]
    _compress_chunk(get_x, nseg, w1_ref, w2_ref, w1r_ref, pe_ref, carry_ref, out_ref)


def _cmp_weight_specs():
    zero = lambda n: (lambda *_: (0,) * n)
    return [pl.BlockSpec((2, CMP_STRIDE, LANE, 2 * LANE), zero(4)),
            pl.BlockSpec((2, LANE, LANE), zero(3)),
            pl.BlockSpec((2, CMP_BLOCK * A_HEAD_DIM, A_HEAD_DIM), zero(3)),
            pl.BlockSpec((2, CMP_BLOCK * A_HEAD_DIM, 1), zero(3))]


def compress_prompt(kv_cmp, cw):
    bsz, t, w = kv_cmp.shape
    nseg = t // CMP_STRIDE
    return pl.pallas_call(
        _cmp_prompt_kernel,
        out_shape=jax.ShapeDtypeStruct((bsz, 2, nseg, KV_WIDTH), F32),
        grid=(bsz,),
        in_specs=[pl.BlockSpec((1, t * QUARTERS, LANE), lambda b_: (b_, 0, 0))] + _cmp_weight_specs(),
        out_specs=pl.BlockSpec((1, 2, nseg, KV_WIDTH), lambda b_: (b_, 0, 0, 0)),
        scratch_shapes=[pltpu.VMEM((2, 2, 1, LANE), F32)],
        compiler_params=_cparams(("arbitrary",)),
        name="cmp_prompt",
    )(kv_cmp.reshape(bsz, t * QUARTERS, LANE), *cw)


def _page_copy(pool_ref, buf_ref, sem_ref, pt_ref, step, slot, k, n_chunks, pages):
    b_ = step // n_chunks
    c = step % n_chunks
    pid = pt_ref[b_, c * pages + k]
    return pltpu.make_async_copy(pool_ref.at[pid], buf_ref.at[slot, k], sem_ref.at[slot])


def _page_pipeline(pool_ref, buf_ref, sem_ref, pt_ref, n_chunks, pages):
    step = pl.program_id(0) * n_chunks + pl.program_id(1)
    total = pl.num_programs(0) * n_chunks
    slot = step % 2

    def start(st, sl):
        for k in range(pages):
            _page_copy(pool_ref, buf_ref, sem_ref, pt_ref, st, sl, k, n_chunks, pages).start()

    @pl.when(step == 0)
    def _():
        start(step, slot)

    @pl.when(step + 1 < total)
    def _():
        start(step + 1, 1 - slot)

    for k in range(pages):
        _page_copy(pool_ref, buf_ref, sem_ref, pt_ref, step, slot, k, n_chunks, pages).wait()
    return slot


def _cmp_sample_kernel(pt_ref, pool_ref, w1_ref, w2_ref, w1r_ref, pe_ref, out_ref,
                       buf_ref, sem_ref, carry_ref, *, n_chunks, pages):
    slot = _page_pipeline(pool_ref, buf_ref, sem_ref, pt_ref, n_chunks, pages)

    @pl.when(pl.program_id(1) == 0)
    def _():
        carry_ref[...] = jnp.zeros_like(carry_ref)

    segs = PAGE_SIZE // CMP_STRIDE
    nrows = pages * segs

    def get_x(quarter, s):
        x = buf_ref[slot, :, pl.ds(QUARTERS * s + quarter, segs, stride=CMP_STRIDE * QUARTERS), :]
        return x.reshape(nrows, LANE)

    _compress_chunk(get_x, nrows, w1_ref, w2_ref, w1r_ref, pe_ref, carry_ref, out_ref)


def _page_chunk(n_pages):
    return math.gcd(n_pages, 32)


def compress_paged(pool, page_table, cw):
    dbs, n_pages = page_table.shape
    pages = _page_chunk(n_pages)
    n_chunks = n_pages // pages
    rows = pages * PAGE_SIZE // CMP_STRIDE
    return pl.pallas_call(
        functools.partial(_cmp_sample_kernel, n_chunks=n_chunks, pages=pages),
        out_shape=jax.ShapeDtypeStruct((dbs, 2, n_chunks * rows, KV_WIDTH), F32),
        grid_spec=pltpu.PrefetchScalarGridSpec(
            num_scalar_prefetch=1,
            grid=(dbs, n_chunks),
            in_specs=[pl.BlockSpec(memory_space=pl.ANY)] + _cmp_weight_specs(),
            out_specs=pl.BlockSpec((1, 2, rows, KV_WIDTH), lambda b_, c, pt: (b_, 0, c, 0)),
            scratch_shapes=[pltpu.VMEM((2, pages, PAGE_SIZE * QUARTERS, LANE), F32),
                            pltpu.SemaphoreType.DMA((2,)),
                            pltpu.VMEM((2, 2, 1, LANE), F32)]),
        compiler_params=_cparams(("arbitrary", "arbitrary")),
        name="cmp_paged",
    )(page_table, pool.reshape(pool.shape[0], PAGE_SIZE * QUARTERS, LANE), *cw)


def _masked_softmax_rows(s, mask):
    s = jnp.where(mask, s, NEG_INF)
    m = jnp.max(s, axis=-1, keepdims=True)
    m = jnp.where(m == NEG_INF, 0.0, m)
    e = jnp.exp(s - m)
    return e / jnp.maximum(jnp.sum(e, axis=-1, keepdims=True), TINY)


def _topk_mask(score, k, axis):
    n = score.shape[axis]
    idx = lax.broadcasted_iota(jnp.int32, score.shape, axis)
    sel = jnp.zeros(score.shape, F32)
    for _ in range(k):
        mx = jnp.max(score, axis=axis, keepdims=True)
        first = jnp.min(jnp.where(score == mx, idx, n), axis=axis, keepdims=True)
        pick = idx == first
        sel = jnp.where(pick, 1.0, sel)
        score = jnp.where(pick, NEG_INF, score)
    return sel


def _flash_tile(carry, s, v):
    m_old, l_old, acc = carry
    m_new = jnp.maximum(m_old, jnp.max(s, axis=-1, keepdims=True))
    m_safe = jnp.where(m_new == NEG_INF, 0.0, m_new)
    p = jnp.exp(s - m_safe)
    alpha = jnp.exp(m_old - m_safe)
    return m_new, alpha * l_old + jnp.sum(p, axis=-1, keepdims=True), alpha * acc + _dot(p, v)


def _flash_init(rows, dv):
    return jnp.full((rows, 1), NEG_INF, F32), jnp.zeros((rows, 1), F32), jnp.zeros((rows, dv), F32)


def _flash_out(carry):
    _, l, acc = carry
    return acc / jnp.maximum(l, TINY)


def _nsa_prompt_kernel(q_ref, small_ref, cmp_ref, ks_ref, vs_ref, kw_ref, vw_ref, cov_ref, exp_ref, o_ref,
                       *, tq, n_slc):
    i = pl.program_id(1)
    q0 = i * tq
    tk = tq
    q_all = q_ref[0] * (A_HEAD_DIM ** -0.5)
    gate = jax.nn.sigmoid(small_ref[0])
    ncmp = cmp_ref.shape[2]
    qpos_c = q0 + lax.broadcasted_iota(jnp.int32, (tq, 1), 0)
    qpos_r = q0 + lax.broadcasted_iota(jnp.int32, (1, tq), 1)
    jrow = lax.broadcasted_iota(jnp.int32, (1, ncmp), 1)
    cmp_mask = (jrow >= 1) & (jrow * CMP_STRIDE + (CMP_BLOCK - CMP_STRIDE - 1) <= qpos_c)
    blk = lax.broadcasted_iota(jnp.int32, (n_slc, 1), 0)
    cur = qpos_r // SLC_BLOCK
    forced = (blk == 0) | (blk == cur) | (blk == cur - 1)
    future = blk * SLC_BLOCK > qpos_r
    kcol = lax.broadcasted_iota(jnp.int32, (1, tk), 1)

    for g in range(A_KV_HEADS):
        gs = slice(g * A_HEAD_DIM, (g + 1) * A_HEAD_DIM)
        heads = [q_all[:, (g * A_GROUP + r) * A_HEAD_DIM:(g * A_GROUP + r + 1) * A_HEAD_DIM] for r in range(A_GROUP)]
        qg = jnp.concatenate(heads, axis=0).astype(BF16)
        rows = A_GROUP * tq

        kc = cmp_ref[0, 0, :, gs]
        vc = cmp_ref[0, 1, :, gs]
        p_c = _masked_softmax_rows(_dot_nt(qg, kc).reshape(A_GROUP, tq, ncmp), cmp_mask[None])
        o_c = _dot(p_c.reshape(rows, ncmp), vc)
        imp_t = _dot_nt(cov_ref[...], jnp.sum(p_c, axis=0))
        score = jnp.where(future, -BIG, imp_t + jnp.where(forced, BIG, 0.0))
        sel_t = _topk_mask(score, min(SLC_TOPK, n_slc), 0)
        sel_t = jnp.concatenate([sel_t, jnp.zeros((exp_ref.shape[0] - n_slc, tq), F32)], axis=0)
        sel = sel_t.T.astype(BF16)

        def slc_step(j, carry):
            off = pl.multiple_of(j * tk, tk)
            s = jnp.dot(qg, ks_ref[0, gs, pl.ds(off, tk)], preferred_element_type=F32)
            picked = jnp.dot(sel, exp_ref[:, pl.ds(off, tk)], preferred_element_type=F32) > 0.5
            mask = picked & (off + kcol <= qpos_c)
            s = jnp.where(mask[None], s.reshape(A_GROUP, tq, tk), NEG_INF).reshape(rows, tk)
            return _flash_tile(carry, s, vs_ref[0, pl.ds(off, tk), gs])

        o_s = _flash_out(lax.fori_loop(0, i + 1, slc_step, _flash_init(rows, A_HEAD_DIM)))

        def win_step(j, carry):
            off = pl.multiple_of(j * tk, tk)
            s = jnp.dot(qg, kw_ref[0, gs, pl.ds(off, tk)], preferred_element_type=F32)
            kpos = off + kcol
            mask = (kpos <= qpos_c) & (kpos > qpos_c - WINDOW)
            s = jnp.where(mask[None], s.reshape(A_GROUP, tq, tk), NEG_INF).reshape(rows, tk)
            return _flash_tile(carry, s, vw_ref[0, pl.ds(off, tk), gs])

        j_lo = jnp.maximum(i - (WINDOW + tk - 1) // tk, 0)
        o_w = _flash_out(lax.fori_loop(j_lo, i + 1, win_step, _flash_init(rows, A_HEAD_DIM)))

        outs = []
        for r in range(A_GROUP):
            lane = SMALL_AG + 3 * (g * A_GROUP + r)
            rs = slice(r * tq, (r + 1) * tq)
            outs.append(gate[:, lane:lane + 1] * o_c[rs] + gate[:, lane + 1:lane + 2] * o_s[rs] +
                        gate[:, lane + 2:lane + 3] * o_w[rs])
        o_ref[0, :, g * KV_WIDTH:(g + 1) * KV_WIDTH] = jnp.concatenate(outs, axis=1)


def _coverage(n_cmp_rows, n_slc):
    cs = (np.arange(n_cmp_rows) - 1) * CMP_STRIDE
    ss = np.arange(n_slc) * SLC_BLOCK
    lo = np.maximum(cs[:, None], ss[None, :])
    hi = np.minimum(cs[:, None] + CMP_BLOCK, ss[None, :] + SLC_BLOCK)
    cov = np.clip(hi - lo, 0, None) / CMP_BLOCK
    cov[0] = 0.0
    return cov.astype(np.float32)


def _block_expand(n_rows, n_keys):
    return (np.arange(n_rows)[:, None] == (np.arange(n_keys) // SLC_BLOCK)[None, :]).astype(np.float32)


def nsa_prompt(q_rot, z, cmp, kv_slc, kv_win, tq):
    bsz, t, _ = q_rot.shape
    n_slc = t // SLC_BLOCK
    ncmp = cmp.shape[2]
    kt = lambda kv: jnp.swapaxes(kv[..., :KV_WIDTH], 1, 2).astype(BF16)
    vv = lambda kv: kv[..., KV_WIDTH:].astype(BF16)
    cov_t = jnp.asarray(_coverage(ncmp, n_slc).T, BF16)
    n_exp = -(-n_slc // 128) * 128
    expand = jnp.asarray(_block_expand(n_exp, t), BF16)
    per_b = lambda shape: pl.BlockSpec((1,) + shape, lambda b_, i: (b_,) + (0,) * len(shape))
    return pl.pallas_call(
        functools.partial(_nsa_prompt_kernel, tq=tq, n_slc=n_slc),
        out_shape=jax.ShapeDtypeStruct((bsz, t, D_MODEL), F32),
        grid=(bsz, t // tq),
        in_specs=[pl.BlockSpec((1, tq, D_MODEL), lambda b_, i: (b_, i, 0)),
                  pl.BlockSpec((1, tq, 128), lambda b_, i: (b_, i, Z_SMALL // 128)),
                  per_b((2, ncmp, KV_WIDTH)),
                  per_b((KV_WIDTH, t)), per_b((t, KV_WIDTH)), per_b((KV_WIDTH, t)), per_b((t, KV_WIDTH)),
                  pl.BlockSpec((n_slc, ncmp), lambda b_, i: (0, 0)),
                  pl.BlockSpec((n_exp, t), lambda b_, i: (0, 0))],
        out_specs=pl.BlockSpec((1, tq, D_MODEL), lambda b_, i: (b_, i, 0)),
        compiler_params=_cparams(("arbitrary", "arbitrary")),
        name="nsa_prompt",
    )(q_rot, z, cmp, kt(kv_slc), vv(kv_slc), kt(kv_win), vv(kv_win), cov_t, expand)


SAMPLE_ROWS = 8
NEW_KEYS = 128


def _nsa_sample_kernel(pt_ref, qbd_ref, gl_ref, cmp_ref, pool_ref, knew_ref, wcache_ref, wnew_ref, cov_ref, exp_ref,
                       o_ref, buf_ref, sem_ref, sel_ref, m_ref, l_ref, acc_ref, oc_ref,
                       *, n_chunks, pages, past, t_real, n_slc):
    c = pl.program_id(1)
    slot = _page_pipeline(pool_ref, buf_ref, sem_ref, pt_ref, n_chunks, pages)
    qbd = qbd_ref[0]
    rows = qbd.shape[0]
    bpc = pages * PAGE_SIZE // SLC_BLOCK
    rq = lax.broadcasted_iota(jnp.int32, (rows, 1), 0) % SAMPLE_ROWS
    qpos = past + rq

    @pl.when(c == 0)
    def _():
        ncmp = cmp_ref.shape[2]
        nbp = cov_ref.shape[1]
        jrow = lax.broadcasted_iota(jnp.int32, (1, ncmp), 1)
        cmp_mask = (jrow >= 1) & (jrow * CMP_STRIDE + (CMP_BLOCK - CMP_STRIDE - 1) <= qpos)
        p = _masked_softmax_rows(_dot_nt(qbd, cmp_ref[0, 0]), cmp_mask)
        oc_ref[...] = _dot(p, cmp_ref[0, 1])
        ri = lax.broadcasted_iota(jnp.int32, (rows, rows), 0)
        ci = lax.broadcasted_iota(jnp.int32, (rows, rows), 1)
        group_rows = A_GROUP * SAMPLE_ROWS
        same = (ri // group_rows == ci // group_rows) & (ri % SAMPLE_ROWS == ci % SAMPLE_ROWS)
        p_group = _dot(jnp.where(same, 1.0, 0.0), p)
        imp = _dot(p_group, cov_ref[...])
        blk = lax.broadcasted_iota(jnp.int32, (1, nbp), 1)
        cur = qpos // SLC_BLOCK
        forced = (blk == 0) | (blk == cur) | (blk == cur - 1)
        future = blk * SLC_BLOCK > qpos
        score = jnp.where(future, -BIG, imp + jnp.where(forced, BIG, 0.0))
        score = jnp.where(blk < n_slc, score, NEG_INF)
        sel = _topk_mask(score, min(SLC_TOPK, n_slc), 1)
        pad = jnp.zeros((rows, 128 - bpc), F32)
        for cc in range(n_chunks + 1):
            sel_ref[cc] = jnp.concatenate([sel[:, cc * bpc:(cc + 1) * bpc], pad], axis=1).astype(BF16)
        m_ref[...] = jnp.full(m_ref.shape, NEG_INF, F32)
        l_ref[...] = jnp.zeros_like(l_ref)
        acc_ref[...] = jnp.zeros_like(acc_ref)

    kv = buf_ref[slot].reshape(pages * PAGE_SIZE, 2 * KV_WIDTH)
    picked = jnp.dot(sel_ref[c], exp_ref[...], preferred_element_type=F32) > 0.5
    s = jnp.where(picked, _dot_nt(qbd, kv[:, :KV_WIDTH]), NEG_INF)
    carry = _flash_tile((m_ref[...], l_ref[...], acc_ref[...]), s, kv[:, KV_WIDTH:])
    m_ref[...], l_ref[...], acc_ref[...] = carry

    @pl.when(c == n_chunks - 1)
    def _():
        zpad = jnp.zeros((NEW_KEYS - SAMPLE_ROWS, 2 * KV_WIDTH), F32)
        kcol = lax.broadcasted_iota(jnp.int32, (1, NEW_KEYS), 1)
        new_mask = (kcol <= rq) & (kcol < t_real)
        knew = jnp.concatenate([knew_ref[0], zpad], axis=0)
        last_picked = sel_ref[n_chunks][:, 0:1].astype(F32) > 0.5
        s_new = jnp.where(new_mask & last_picked, _dot_nt(qbd, knew[:, :KV_WIDTH]), NEG_INF)
        o_s = _flash_out(_flash_tile((m_ref[...], l_ref[...], acc_ref[...]), s_new, knew[:, KV_WIDTH:]))

        wc = wcache_ref[0]
        wb = wc.shape[0]
        wcol = lax.broadcasted_iota(jnp.int32, (1, wb), 1)
        s_w = jnp.where(wcol > rq + (wb - WINDOW), _dot_nt(qbd, wc[:, :KV_WIDTH]), NEG_INF)
        cw = _flash_tile(_flash_init(rows, KV_WIDTH), s_w, wc[:, KV_WIDTH:])
        wnew = jnp.concatenate([wnew_ref[0], zpad], axis=0)
        s_wn = jnp.where(new_mask, _dot_nt(qbd, wnew[:, :KV_WIDTH]), NEG_INF)
        o_w = _flash_out(_flash_tile(cw, s_wn, wnew[:, KV_WIDTH:]))

        gate = jax.nn.sigmoid(gl_ref[0])
        o = gate[:, 0:1] * oc_ref[...] + gate[:, 1:2] * o_s + gate[:, 2:3] * o_w
        lane_g = lax.broadcasted_iota(jnp.int32, (1, KV_WIDTH), 1) // A_HEAD_DIM
        row_g = lax.broadcasted_iota(jnp.int32, (rows, 1), 0) // (A_GROUP * SAMPLE_ROWS)
        o = jnp.where(lane_g == row_g, o, 0.0)
        o_ref[0] = sum(o[:, g * A_HEAD_DIM:(g + 1) * A_HEAD_DIM] for g in range(A_KV_HEADS))


def nsa_sample(q_rot, z, cmp, pool, page_table, k_new, win_cache, w_new, past, t_real):
    dbs = q_rot.shape[0]
    n_pages = page_table.shape[1]
    pages = _page_chunk(n_pages)
    n_chunks = n_pages // pages
    bpc = pages * PAGE_SIZE // SLC_BLOCK
    ncmp = cmp.shape[2]
    n_slc = -(-(past + t_real) // SLC_BLOCK)
    assert n_slc == n_chunks * bpc + 1 and bpc <= 128
    nbp = -(-((n_chunks + 1) * bpc) // 128) * 128
    rows = A_HEADS * SAMPLE_ROWS
    q5 = q_rot.reshape(dbs, SAMPLE_ROWS, A_KV_HEADS, A_GROUP, A_HEAD_DIM) * (A_HEAD_DIM ** -0.5)
    qbd = jnp.einsum("bqgrd,gh->bgrqhd", q5, jnp.eye(A_KV_HEADS, dtype=F32)).reshape(dbs, rows, KV_WIDTH).astype(BF16)
    gl = z[..., Z_SMALL + SMALL_AG:Z_SMALL + SMALL_AG + 3 * A_HEADS].reshape(dbs, SAMPLE_ROWS, A_HEADS, 3)
    gl = jnp.swapaxes(gl, 1, 2).reshape(dbs, rows, 3)
    cov = np.zeros((ncmp, nbp), np.float32)
    cov[:, :n_slc] = _coverage(ncmp, n_slc)
    expand = jnp.asarray(_block_expand(128, pages * PAGE_SIZE), BF16)
    per_b = lambda shape: pl.BlockSpec((1,) + shape, lambda b_, c, pt: (b_,) + (0,) * len(shape))
    const = lambda shape: pl.BlockSpec(shape, lambda b_, c, pt: (0,) * len(shape))
    wb = win_cache.shape[1]
    out = pl.pallas_call(
        functools.partial(_nsa_sample_kernel, n_chunks=n_chunks, pages=pages, past=past, t_real=t_real, n_slc=n_slc),
        out_shape=jax.ShapeDtypeStruct((dbs, rows, A_HEAD_DIM), F32),
        grid_spec=pltpu.PrefetchScalarGridSpec(
            num_scalar_prefetch=1,
            grid=(dbs, n_chunks),
            in_specs=[per_b((rows, KV_WIDTH)), per_b((rows, 3)), per_b((2, ncmp, KV_WIDTH)),
                      pl.BlockSpec(memory_space=pl.ANY),
                      per_b((SAMPLE_ROWS, 2 * KV_WIDTH)), per_b((wb, 2 * KV_WIDTH)), per_b((SAMPLE_ROWS, 2 * KV_WIDTH)),
                      const((ncmp, nbp)), const((128, pages * PAGE_SIZE))],
            out_specs=per_b((rows, A_HEAD_DIM)),
            scratch_shapes=[pltpu.VMEM((2, pages, PAGE_SIZE, 2 * KV_WIDTH), F32),
                            pltpu.SemaphoreType.DMA((2,)),
                            pltpu.VMEM((n_chunks + 1, rows, 128), BF16),
                            pltpu.VMEM((rows, 1), F32), pltpu.VMEM((rows, 1), F32),
                            pltpu.VMEM((rows, KV_WIDTH), F32), pltpu.VMEM((rows, KV_WIDTH), F32)]),
        compiler_params=_cparams(("arbitrary", "arbitrary")),
        name="nsa_sample",
    )(page_table, qbd, gl, cmp, pool, k_new, win_cache, w_new, jnp.asarray(cov, BF16), expand)
    out = jnp.swapaxes(out.reshape(dbs, A_HEADS, SAMPLE_ROWS, A_HEAD_DIM), 1, 2)
    return out.reshape(dbs, SAMPLE_ROWS, D_MODEL)


def _kv_rows(a, bsz, t):
    return a.reshape(bsz, t, 2, A_KV_HEADS, A_HEAD_DIM)


def kernel(x_prompt, x_sample, cache_cmp_kv, cache_slc_kv, cache_win_kv, state_C, state_n, state_m, state_conv,
           page_table, c_prompt, c_sample, w_ada, b_ada, g_pre_mix, g_post_mix, g_pre_ffn, g_post_ffn, w_in, b_in,
           m_norm_w, cmp_w1, cmp_pe, cmp_w2, w_branch_m, w_branch_a, w_out, w_up, conv_w, conv_b, w_down):
    depth = w_ada.shape[0]
    bsz, t, d = x_prompt.shape
    dbs, ts, _ = x_sample.shape
    n_pages = page_table.shape[1]
    past = n_pages * PAGE_SIZE
    assert ts <= SAMPLE_ROWS and (past + ts) // CMP_STRIDE == past // CMP_STRIDE and past >= WINDOW
    srows = dbs * SAMPLE_ROWS
    xp = x_prompt.astype(F32)
    xs = jnp.pad(x_sample.astype(F32), ((0, 0), (0, SAMPLE_ROWS - ts), (0, 0))).reshape(1, srows, d)
    c_all = jnp.concatenate([c_prompt, c_sample], axis=0).astype(F32)
    c_all = jnp.pad(c_all, ((0, (-c_all.shape[0]) % 8), (0, 0)))
    tab_p = rope_tables(jnp.arange(t, dtype=jnp.int32))
    tab_s = rope_tables(jnp.tile(past + jnp.arange(SAMPLE_ROWS, dtype=jnp.int32), dbs))
    lchunk = math.gcd(t, 256)
    tm_p = math.gcd(t, 512)
    p_states, s_states = [], []
    for l in range(depth):
        mod = ada_modulation(c_all, w_ada[l], b_ada[l])
        mod_p = [m[:, None, :] for m in jnp.split(mod[:bsz], 6, axis=-1)]
        mod_s = [jnp.repeat(m, SAMPLE_ROWS, axis=0)[None] for m in jnp.split(mod[bsz:bsz + dbs], 6, axis=-1)]
        w_r, b_r = regroup_in_weights(w_in[l], b_in[l])
        cw = compress_weights(cmp_w1[l], cmp_pe[l], cmp_w2[l])
        wm, wa, wo = w_branch_m[l].astype(BF16), w_branch_a[l].astype(BF16), w_out[l].astype(BF16)
        wu, wd = w_up[l].astype(BF16), w_down[l].astype(BF16)

        sh_m, sc_m, gt_m, sh_f, sc_f, gt_f = mod_p
        z = in_projection(xp, g_pre_mix[l], sc_m, sh_m, w_r, b_r, tm=math.gcd(t, 1024))
        q_rot, kv_cmp, kv_slc, kv_win = rope_split(z, tab_p, tm=tm_p)
        hm, p_c, p_n, p_m = mlstm(z, m_norm_w[l], jnp.zeros((bsz, M_HEADS, M_HEAD_DIM, M_HEAD_DIM), F32),
                                  jnp.zeros((bsz, M_HEADS, M_HEAD_DIM), F32), jnp.zeros((bsz, M_HEADS), F32),
                                  lb=lchunk, lp=lchunk, t_real=lchunk)
        ha = nsa_prompt(q_rot, z, compress_prompt(kv_cmp, cw), kv_slc, kv_win, tq=256)
        xp = mix_out(xp, hm, ha, z, gt_m, g_post_mix[l], wm, wa, wo, tm=tm_p)
        xp, p_conv = conv_ffn(xp, g_pre_ffn[l], sc_f, sh_f, gt_f, g_post_ffn[l], wu, conv_w[l], conv_b[l], wd, tm=tm_p)
        wkeep = min(WINDOW, t)
        p_states.append((_kv_rows(kv_cmp, bsz, t), _kv_rows(kv_slc, bsz, t), _kv_rows(kv_win[:, t - wkeep:], bsz, wkeep),
                         p_c, p_n, p_m, p_conv))

        sh_m, sc_m, gt_m, sh_f, sc_f, gt_f = mod_s
        z = in_projection(xs, g_pre_mix[l], sc_m, sh_m, w_r, b_r, tm=srows)
        q_rot, kv_cmp, kv_slc, kv_win = rope_split(z, tab_s, tm=srows)
        z3 = z.reshape(dbs, SAMPLE_ROWS, Z_WIDTH)
        hm, s_c, s_n, s_m = mlstm(z3, m_norm_w[l], state_C[l].astype(F32), state_n[l].astype(F32),
                                  state_m[l].astype(F32), lb=SAMPLE_ROWS, lp=128, t_real=ts)
        pool_shape = (cache_cmp_kv.shape[1], PAGE_SIZE, 2 * KV_WIDTH)
        cmp_s = compress_paged(cache_cmp_kv[l].astype(F32).reshape(pool_shape), page_table, cw)
        new3 = lambda a: a.reshape(dbs, SAMPLE_ROWS, 2 * KV_WIDTH)
        win_cache = cache_win_kv[l].astype(F32).reshape(dbs, -1, 2 * KV_WIDTH)
        ha = nsa_sample(q_rot.reshape(dbs, SAMPLE_ROWS, d), z3, cmp_s, cache_slc_kv[l].astype(F32).reshape(pool_shape),
                        page_table, new3(kv_slc), win_cache, new3(kv_win), past, ts)
        xs = mix_out(xs, hm.reshape(1, srows, d), ha.reshape(1, srows, d), z, gt_m, g_post_mix[l], wm, wa, wo, tm=srows)
        st = state_conv[l].astype(F32)
        s2 = jnp.pad(st, ((0, 0), (0, SAMPLE_ROWS - (CONV_W - 1)), (0, 0))).reshape(1, srows, 2 * D_FF)
        s1 = jnp.pad(st[:, 1:], ((0, 0), (0, SAMPLE_ROWS - 1), (0, 0))).reshape(1, srows, 2 * D_FF)
        xs, u = conv_ffn(xs, g_pre_ffn[l], sc_f, sh_f, gt_f, g_post_ffn[l], wu, conv_w[l], conv_b[l], wd, tm=srows,
                         state_rows=(s1, s2))
        wb = win_cache.shape[1]
        s_win = jnp.concatenate([win_cache, new3(kv_win)[:, :ts]], axis=1)[:, ts:]
        s_conv = jnp.concatenate([st, u.reshape(dbs, SAMPLE_ROWS, 2 * D_FF)[:, :ts]], axis=1)[:, ts:]
        s_states.append((_kv_rows(new3(kv_cmp)[:, :ts], dbs, ts), _kv_rows(new3(kv_slc)[:, :ts], dbs, ts),
                         _kv_rows(s_win, dbs, wb), s_c, s_n, s_m, s_conv))

    stack = lambda states: [jnp.stack([s[i] for s in states]) for i in range(7)]
    y_sample = xs.reshape(dbs, SAMPLE_ROWS, d)[:, :ts]
    return (xp, y_sample, *stack(p_states), *stack(s_states))
```

```python
import functools
import math

import numpy as np
import jax
import jax.numpy as jnp
from jax import lax
from jax.experimental import pallas as pl
from jax.experimental.pallas import tpu as pltpu

F32 = jnp.float32
BF16 = jnp.bfloat16

D_MODEL = 1024
M_HEADS = 4
M_HEAD_DIM = 256
A_HEADS = 16
A_HEAD_DIM = 64
A_KV_HEADS = 4
A_GROUP = 4
KV_WIDTH = A_KV_HEADS * A_HEAD_DIM
CMP_STRIDE = 16
CMP_BLOCK = 32
SLC_BLOCK = 64
SLC_TOPK = 16
WINDOW = 512
ROPE_THETA = 500000.0
ROPE_DIM = 16
BIG = 1e6
D_FF = 2816
CONV_W = 3
NORM_EPS = 1e-6
PAGE_SIZE = 128
NEG_INF = float("-inf")
TINY = float(np.finfo(np.float32).tiny)

Z_MQ, Z_MK, Z_MV, Z_MO, Z_AQ, Z_GA, Z_GB, Z_AKV, Z_SMALL = 0, 1024, 2048, 3072, 4096, 5120, 6144, 7168, 8704
Z_WIDTH = 9216
SMALL_MI, SMALL_MF, SMALL_AG = 0, 4, 8

VMEM_LIMIT = 48 * 1024 * 1024


def _cparams(sem):
    return pltpu.CompilerParams(dimension_semantics=sem, vmem_limit_bytes=VMEM_LIMIT)


def _dot(a, b):
    return jnp.dot(a.astype(BF16), b.astype(BF16), preferred_element_type=F32)


def _dot_nt(a, b):
    return lax.dot_general(a.astype(BF16), b.astype(BF16), (((1,), (1,)), ((), ())), preferred_element_type=F32)


def _dot_tn(a, b):
    return lax.dot_general(a.astype(BF16), b.astype(BF16), (((0,), (0,)), ((), ())), preferred_element_type=F32)


def _split3(x):
    x1 = x.astype(BF16)
    r1 = x - x1.astype(F32)
    x2 = r1.astype(BF16)
    x3 = (r1 - x2.astype(F32)).astype(BF16)
    return x1, x2, x3


def _rms(x, g):
    return x * lax.rsqrt(jnp.mean(x * x, axis=-1, keepdims=True) + NORM_EPS) * g


def _ada_kernel(c_ref, w_ref, b_ref, o_ref):
    c = c_ref[...]
    o_ref[...] = _dot(c * jax.nn.sigmoid(c), w_ref[...]) + b_ref[...]


def ada_modulation(c, w_ada, b_ada):
    rows, d = c.shape
    n = w_ada.shape[1]
    tn = 512
    return pl.pallas_call(
        _ada_kernel,
        out_shape=jax.ShapeDtypeStruct((rows, n), F32),
        grid=(n // tn,),
        in_specs=[pl.BlockSpec((rows, d), lambda j: (0, 0)),
                  pl.BlockSpec((d, tn), lambda j: (0, j)),
                  pl.BlockSpec((1, tn), lambda j: (0, j))],
        out_specs=pl.BlockSpec((rows, tn), lambda j: (0, j)),
        compiler_params=_cparams(("arbitrary",)),
        name="ada",
    )(c, w_ada, b_ada.reshape(1, n))


def _inproj_kernel(x_ref, g_ref, sc_ref, sh_ref, w_ref, b_ref, o_ref, h_ref):
    @pl.when(pl.program_id(2) == 0)
    def _():
        h = _rms(x_ref[0], g_ref[...]) * (1.0 + sc_ref[0]) + sh_ref[0]
        h_ref[...] = h.astype(BF16)

    o_ref[0] = jnp.dot(h_ref[...], w_ref[...], preferred_element_type=F32) + b_ref[...]


def in_projection(x, g, sc, sh, w_bf16, b, tm):
    bsz, t, d = x.shape
    r = sc.shape[1]
    rb = 1 if r == 1 else tm
    tn = 1024
    mod_spec = pl.BlockSpec((1, rb, d), (lambda b_, i, j: (b_, 0, 0)) if r == 1 else (lambda b_, i, j: (b_, i, 0)))
    return pl.pallas_call(
        _inproj_kernel,
        out_shape=jax.ShapeDtypeStruct((bsz, t, Z_WIDTH), F32),
        grid=(bsz, t // tm, Z_WIDTH // tn),
        in_specs=[pl.BlockSpec((1, tm, d), lambda b_, i, j: (b_, i, 0)),
                  pl.BlockSpec((1, d), lambda b_, i, j: (0, 0)),
                  mod_spec, mod_spec,
                  pl.BlockSpec((d, tn), lambda b_, i, j: (0, j)),
                  pl.BlockSpec((1, tn), lambda b_, i, j: (0, j))],
        out_specs=pl.BlockSpec((1, tm, tn), lambda b_, i, j: (b_, i, j)),
        scratch_shapes=[pltpu.VMEM((tm, d), BF16)],
        compiler_params=_cparams(("arbitrary", "arbitrary", "arbitrary")),
        name="inproj",
    )(x, g.reshape(1, d), sc, sh, w_bf16, b.reshape(1, Z_WIDTH))


def regroup_in_weights(w_in, b_in):
    mw = M_HEADS * M_HEAD_DIM
    o_mi = 4 * mw
    o_aq = o_mi + 2 * M_HEADS
    o_akv = o_aq + A_HEADS * A_HEAD_DIM
    o_ag = o_akv + 6 * KV_WIDTH
    o_ga = o_ag + 3 * A_HEADS
    o_gb = o_ga + D_MODEL

    def regroup(a):
        lead = a.shape[:-1]
        parts = [a[..., :o_mi], a[..., o_aq:o_akv], a[..., o_ga:o_gb], a[..., o_gb:o_gb + D_MODEL],
                 a[..., o_akv:o_ag], a[..., o_mi:o_aq], a[..., o_ag:o_ga],
                 jnp.zeros(lead + (128 - 2 * M_HEADS - 3 * A_HEADS,), a.dtype),
                 jnp.zeros(lead + (Z_WIDTH - Z_SMALL - 128,), a.dtype)]
        return jnp.concatenate(parts, axis=-1)

    return regroup(w_in).astype(BF16), regroup(b_in)


def rope_tables(pos):
    half = ROPE_DIM // 2
    inv_freq = ROPE_THETA ** (-jnp.arange(half, dtype=F32) / half)
    ang = pos.astype(F32)[:, None] * inv_freq
    cos, sin = jnp.cos(ang), jnp.sin(ang)
    rows = pos.shape[0]
    zeros = jnp.zeros((rows, half), F32)
    rest1 = jnp.ones((rows, A_HEAD_DIM - ROPE_DIM), F32)
    rest0 = jnp.zeros((rows, A_HEAD_DIM - ROPE_DIM), F32)
    c = jnp.concatenate([cos, cos, rest1], axis=1)
    sa = jnp.concatenate([zeros, sin, rest0], axis=1)
    sb = jnp.concatenate([-sin, zeros, rest0], axis=1)
    return tuple(jnp.concatenate([a, a], axis=1) for a in (c, sa, sb))


def _rope_apply(x, c, sa, sb):
    w = x.shape[1]
    n = w // 128
    ct, sat, sbt = (jnp.concatenate([a] * n, axis=1) for a in (c, sa, sb))
    return x * ct + pltpu.roll(x, ROPE_DIM // 2, 1) * sat + pltpu.roll(x, w - ROPE_DIM // 2, 1) * sbt


def _rope_kernel(q_ref, c_ref, s_ref, w_ref, cos_ref, sa_ref, sb_ref, qo_ref, co_ref, so_ref, wo_ref):
    c, sa, sb = cos_ref[...], sa_ref[...], sb_ref[...]
    qo_ref[0] = _rope_apply(q_ref[0], c, sa, sb)
    for src, dst in ((c_ref, co_ref), (s_ref, so_ref), (w_ref, wo_ref)):
        kv = src[0]
        dst[0] = jnp.concatenate([_rope_apply(kv[:, :KV_WIDTH], c, sa, sb), kv[:, KV_WIDTH:]], axis=1)


def rope_split(z, tables, tm):
    bsz, t, _ = z.shape
    nt = t // tm
    kvw = 2 * KV_WIDTH
    tab_spec = pl.BlockSpec((tm, 128), lambda b_, i: (i, 0))
    return pl.pallas_call(
        _rope_kernel,
        out_shape=(jax.ShapeDtypeStruct((bsz, t, D_MODEL), F32),) + (jax.ShapeDtypeStruct((bsz, t, kvw), F32),) * 3,
        grid=(bsz, nt),
        in_specs=[pl.BlockSpec((1, tm, D_MODEL), lambda b_, i: (b_, i, Z_AQ // D_MODEL)),
                  pl.BlockSpec((1, tm, kvw), lambda b_, i: (b_, i, Z_AKV // kvw)),
                  pl.BlockSpec((1, tm, kvw), lambda b_, i: (b_, i, Z_AKV // kvw + 1)),
                  pl.BlockSpec((1, tm, kvw), lambda b_, i: (b_, i, Z_AKV // kvw + 2)),
                  tab_spec, tab_spec, tab_spec],
        out_specs=(pl.BlockSpec((1, tm, D_MODEL), lambda b_, i: (b_, i, 0)),) +
                  (pl.BlockSpec((1, tm, kvw), lambda b_, i: (b_, i, 0)),) * 3,
        compiler_params=_cparams(("arbitrary", "arbitrary")),
        name="rope",
    )(z, z, z, z, *tables)


def _mlstm_kernel(q_ref, k_ref, v_ref, o_ref, s_ref, nw_ref, c0_ref, n0_ref, m0_ref,
                  h_ref, c_ref, n_ref, m_ref, *, lb, lp, t_real):
    @pl.when(pl.program_id(1) == 0)
    def _():
        c_ref[...] = c0_ref[...]
        n_ref[...] = n0_ref[...]
        m_ref[...] = m0_ref[...]

    def pad(a):
        if lb == lp:
            return a
        return jnp.concatenate([a, jnp.zeros((lp - lb, a.shape[1]), a.dtype)], axis=0)

    small = pad(s_ref[0])
    small_t = small.T
    row_c = lax.broadcasted_iota(jnp.int32, (lp, 1), 0)
    row_r = lax.broadcasted_iota(jnp.int32, (1, lp), 1)
    li_col_all = jnp.where(row_c < t_real, small, NEG_INF)
    lf_col_all = jnp.where(row_c < t_real, jax.nn.log_sigmoid(small), 0.0)
    li_row_all = jnp.where(row_r < t_real, small_t[0:8], NEG_INF)
    lf_row_all = jnp.where(row_r < t_real, jax.nn.log_sigmoid(small_t[0:8]), 0.0)
    rr = lax.broadcasted_iota(jnp.int32, (lp, lp), 0)
    cc = lax.broadcasted_iota(jnp.int32, (lp, lp), 1)
    causal = cc <= rr
    tril = jnp.where(causal, 1.0, 0.0).astype(BF16)
    triu = jnp.where(rr <= cc, 1.0, 0.0).astype(BF16)
    b_col_all = sum(jnp.dot(tril, p, preferred_element_type=F32) for p in _split3(lf_col_all))
    b_row_all = sum(jnp.dot(p, triu, preferred_element_type=F32) for p in _split3(lf_row_all))

    q_all, k_all, v_all, o_all = pad(q_ref[0]), pad(k_ref[0]), pad(v_ref[0]), pad(o_ref[0])
    nw = nw_ref[...]
    for h in range(M_HEADS):
        hs = slice(h * M_HEAD_DIM, (h + 1) * M_HEAD_DIM)
        qf = q_all[:, hs]
        kf = k_all[:, hs] * (M_HEAD_DIM ** -0.5)
        vf = v_all[:, hs]
        li_row = li_row_all[SMALL_MI + h:SMALL_MI + h + 1, :]
        b_row = b_row_all[SMALL_MF + h:SMALL_MF + h + 1, :]
        li_col = li_col_all[:, SMALL_MI + h:SMALL_MI + h + 1]
        b_col = b_col_all[:, SMALL_MF + h:SMALL_MF + h + 1]
        m_prev = m_ref[0, h]
        c_prev = c_ref[0, h]
        n_prev = n_ref[0, h]

        dlog = jnp.where(causal, b_col - b_row + li_row, NEG_INF)
        inter = m_prev + b_col
        mt = jnp.maximum(inter, jnp.max(dlog, axis=1, keepdims=True))
        a = jnp.exp(inter - mt)
        s = _dot_nt(qf, kf) * jnp.exp(dlog - mt)
        num = a * _dot_nt(qf, c_prev) + _dot(s, vf)
        den = a * jnp.sum(qf * n_prev, axis=1, keepdims=True) + jnp.sum(s, axis=1, keepdims=True)
        hh = num / jnp.maximum(jnp.abs(den), jnp.exp(-mt))
        mu = jnp.mean(hh, axis=1, keepdims=True)
        var = jnp.mean(jnp.square(hh - mu), axis=1, keepdims=True)
        out = (hh - mu) * lax.rsqrt(var + NORM_EPS) * nw[:, hs] * jax.nn.sigmoid(o_all[:, hs])
        h_ref[0, :, hs] = out[:lb]

        bl = b_row[:, lp - 1:lp]
        wlog = bl - b_col + li_col
        m_new = jnp.maximum(m_prev + bl, jnp.max(wlog, axis=0, keepdims=True))
        w = jnp.exp(wlog - m_new)
        decay = jnp.exp(m_prev + bl - m_new)
        c_ref[0, h] = decay * c_prev + _dot_tn(vf * w, kf)
        n_ref[0, h] = decay * n_prev + jnp.sum(w * kf, axis=0, keepdims=True)
        m_ref[0, h] = m_new


def mlstm(z, norm_w, c0, n0, m0, lb, lp, t_real):
    bsz, t, _ = z.shape
    nc = t // lb
    mw = M_HEADS * M_HEAD_DIM
    zspec = lambda col: pl.BlockSpec((1, lb, mw), lambda b_, c: (b_, c, col // mw))
    cst = lambda shape: pl.BlockSpec((1,) + shape, lambda b_, c: (b_,) + (0,) * len(shape))
    h, c, n, m = pl.pallas_call(
        functools.partial(_mlstm_kernel, lb=lb, lp=lp, t_real=t_real),
        out_shape=(jax.ShapeDtypeStruct((bsz, t, mw), F32),
                   jax.ShapeDtypeStruct((bsz, M_HEADS, M_HEAD_DIM, M_HEAD_DIM), F32),
                   jax.ShapeDtypeStruct((bsz, M_HEADS, 1, M_HEAD_DIM), F32),
                   jax.ShapeDtypeStruct((bsz, M_HEADS, 1, 1), F32)),
        grid=(bsz, nc),
        in_specs=[zspec(Z_MQ), zspec(Z_MK), zspec(Z_MV), zspec(Z_MO),
                  pl.BlockSpec((1, lb, 128), lambda b_, c: (b_, c, Z_SMALL // 128)),
                  pl.BlockSpec((1, mw), lambda b_, c: (0, 0)),
                  cst((M_HEADS, M_HEAD_DIM, M_HEAD_DIM)), cst((M_HEADS, 1, M_HEAD_DIM)), cst((M_HEADS, 1, 1))],
        out_specs=(pl.BlockSpec((1, lb, mw), lambda b_, c: (b_, c, 0)),
                   cst((M_HEADS, M_HEAD_DIM, M_HEAD_DIM)), cst((M_HEADS, 1, M_HEAD_DIM)), cst((M_HEADS, 1, 1))),
        compiler_params=_cparams(("arbitrary", "arbitrary")),
        name="mlstm",
    )(z, z, z, z, z, norm_w.reshape(1, mw), c0, n0.reshape(bsz, M_HEADS, 1, M_HEAD_DIM),
      m0.reshape(bsz, M_HEADS, 1, 1))
    return h, c, n.reshape(bsz, M_HEADS, M_HEAD_DIM), m.reshape(bsz, M_HEADS)


def _mix_kernel(x_ref, hm_ref, ha_ref, ga_ref, gb_ref, gt_ref, g_ref, wm_ref, wa_ref, wo_ref, o_ref):
    mixed = (jax.nn.sigmoid(ga_ref[0]) * _dot(hm_ref[0], wm_ref[...]) +
             jax.nn.sigmoid(gb_ref[0]) * _dot(ha_ref[0], wa_ref[...]))
    o_ref[0] = x_ref[0] + gt_ref[0] * _rms(_dot(mixed, wo_ref[...]), g_ref[...])


def mix_out(x, hm, ha, z, gt, g_post, wm, wa, wo, tm):
    bsz, t, d = x.shape
    r = gt.shape[1]
    rb = 1 if r == 1 else tm
    row = lambda col=0: pl.BlockSpec((1, tm, d), lambda b_, i: (b_, i, col // d))
    mod = pl.BlockSpec((1, rb, d), (lambda b_, i: (b_, 0, 0)) if r == 1 else (lambda b_, i: (b_, i, 0)))
    wsp = pl.BlockSpec((d, d), lambda b_, i: (0, 0))
    return pl.pallas_call(
        _mix_kernel,
        out_shape=jax.ShapeDtypeStruct((bsz, t, d), F32),
        grid=(bsz, t // tm),
        in_specs=[row(), row(), row(), row(Z_GA), row(Z_GB), mod,
                  pl.BlockSpec((1, d), lambda b_, i: (0, 0)), wsp, wsp, wsp],
        out_specs=row(),
        compiler_params=_cparams(("arbitrary", "arbitrary")),
        name="mix",
    )(x, hm, ha, z, z, gt, g_post.reshape(1, d), wm, wa, wo)


FF_CHUNK = 256


def _ffn_kernel(*refs, per_row_state):
    if per_row_state:
        (x_ref, g_ref, sc_ref, sh_ref, gt_ref, gp_ref, wa_ref, wg_ref, cwa_ref, cwg_ref, cba_ref, cbg_ref, wd_ref,
         s1a_ref, s1g_ref, s2a_ref, s2g_ref, y_ref, ua_ref, ug_ref, h_ref, acc_ref) = refs
    else:
        (x_ref, g_ref, sc_ref, sh_ref, gt_ref, gp_ref, wa_ref, wg_ref, cwa_ref, cwg_ref, cba_ref, cbg_ref, wd_ref,
         y_ref, ua_ref, ug_ref, h_ref, acc_ref, carry_ref) = refs
    i = pl.program_id(1)
    f = pl.program_id(2)
    tm = x_ref.shape[1]

    @pl.when(f == 0)
    def _():
        h = _rms(x_ref[0], g_ref[...]) * (1.0 + sc_ref[0]) + sh_ref[0]
        h_ref[...] = h.astype(BF16)
        acc_ref[...] = jnp.zeros_like(acc_ref)

    if not per_row_state:
        @pl.when(i == 0)
        def _():
            carry_ref[f] = jnp.zeros(carry_ref.shape[1:], F32)

    row = lax.broadcasted_iota(jnp.int32, (tm, 1), 0)

    def branch(w_ref, cw_ref, cb_ref, part, s1_ref=None, s2_ref=None):
        u = jnp.dot(h_ref[...], w_ref[...], preferred_element_type=F32)
        r1 = pltpu.roll(u, 1, 0)
        r2 = pltpu.roll(u, 2, 0)
        if per_row_state:
            t = row % 8
            u1 = jnp.where(t < 1, s1_ref[0], r1)
            u2 = jnp.where(t < 2, s2_ref[0], r2)
        else:
            prev = carry_ref[f, part]
            u1 = jnp.where(row < 1, prev[1:2], r1)
            u2 = jnp.where(row < 1, prev[0:1], jnp.where(row < 2, prev[1:2], r2))
            carry_ref[f, part, 0:2] = u[tm - 2:tm]
        cw = cw_ref[...]
        return u, cb_ref[...] + cw[0:1] * u2 + cw[1:2] * u1 + cw[2:3] * u

    if per_row_state:
        u_a, conv_a = branch(wa_ref, cwa_ref, cba_ref, 0, s1a_ref, s2a_ref)
        u_g, conv_g = branch(wg_ref, cwg_ref, cbg_ref, 1, s1g_ref, s2g_ref)
        ua_ref[0] = u_a
        ug_ref[0] = u_g
    else:
        u_a, conv_a = branch(wa_ref, cwa_ref, cba_ref, 0)
        u_g, conv_g = branch(wg_ref, cwg_ref, cbg_ref, 1)
        ua_ref[0, 0] = u_a[tm - 2:tm]
        ug_ref[0, 0] = u_g[tm - 2:tm]
    acc_ref[...] += _dot(jax.nn.gelu(conv_g) * conv_a, wd_ref[...])

    @pl.when(f == pl.num_programs(2) - 1)
    def _():
        y_ref[0] = x_ref[0] + gt_ref[0] * _rms(acc_ref[...], gp_ref[...])


def conv_ffn(x, g_pre, sc, sh, gt, g_post, w_up, conv_w, conv_b, w_down, tm, state_rows=None):
    bsz, t, d = x.shape
    r = sc.shape[1]
    rb = 1 if r == 1 else tm
    ck = FF_CHUNK
    nf = D_FF // ck
    per_row = state_rows is not None
    mod = pl.BlockSpec((1, rb, d), (lambda b_, i, f: (b_, 0, 0)) if r == 1 else (lambda b_, i, f: (b_, i, 0)))
    xrow = pl.BlockSpec((1, tm, d), lambda b_, i, f: (b_, i, 0))
    vec = pl.BlockSpec((1, d), lambda b_, i, f: (0, 0))
    col_a = lambda rows: pl.BlockSpec((rows, ck), lambda b_, i, f: (0, f))
    col_g = lambda rows: pl.BlockSpec((rows, ck), lambda b_, i, f: (0, nf + f))
    in_specs = [xrow, vec, mod, mod, mod, vec, col_a(d), col_g(d), col_a(CONV_W), col_g(CONV_W), col_a(1), col_g(1),
                pl.BlockSpec((ck, d), lambda b_, i, f: (f, 0))]
    args = [x, g_pre.reshape(1, d), sc, sh, gt, g_post.reshape(1, d), w_up, w_up, conv_w, conv_w,
            conv_b.reshape(1, 2 * D_FF), conv_b.reshape(1, 2 * D_FF), w_down]
    scratch = [pltpu.VMEM((tm, d), BF16), pltpu.VMEM((tm, d), F32)]
    if per_row:
        urows = tm
        st_a = pl.BlockSpec((1, tm, ck), lambda b_, i, f: (b_, i, f))
        st_g = pl.BlockSpec((1, tm, ck), lambda b_, i, f: (b_, i, nf + f))
        in_specs += [st_a, st_g, st_a, st_g]
        args += [state_rows[0], state_rows[0], state_rows[1], state_rows[1]]
        u_shape = jax.ShapeDtypeStruct((bsz, t, D_FF), F32)
        u_spec = pl.BlockSpec((1, tm, ck), lambda b_, i, f: (b_, i, f))
    else:
        u_shape = jax.ShapeDtypeStruct((bsz, t // tm, CONV_W - 1, D_FF), F32)
        u_spec = pl.BlockSpec((1, 1, CONV_W - 1, ck), lambda b_, i, f: (b_, i, 0, f))
        scratch.append(pltpu.VMEM((nf, 2, 8, ck), F32))
    y, ua, ug = pl.pallas_call(
        functools.partial(_ffn_kernel, per_row_state=per_row),
        out_shape=(jax.ShapeDtypeStruct((bsz, t, d), F32), u_shape, u_shape),
        grid=(bsz, t // tm, nf),
        in_specs=in_specs,
        out_specs=(xrow, u_spec, u_spec),
        scratch_shapes=scratch,
        compiler_params=_cparams(("arbitrary", "arbitrary", "arbitrary")),
        name="ffn",
    )(*args)
    if not per_row:
        ua, ug = ua[:, -1], ug[:, -1]
    return y, jnp.concatenate([ua, ug], axis=-1)


LANE = 128
QUARTERS = 2 * KV_WIDTH // LANE
HEADS_PER_LANE_ROW = LANE // A_HEAD_DIM


def compress_weights(cmp_w1, cmp_pe, cmp_w2):
    eye = jnp.eye(HEADS_PER_LANE_ROW, dtype=F32)
    bd = lambda w: jnp.einsum("gh,...de->...gdhe", eye, w).reshape(w.shape[:-2] + (LANE, LANE))
    w1ab = jnp.concatenate([bd(cmp_w1[:, :CMP_STRIDE]), bd(cmp_w1[:, CMP_STRIDE:])], axis=-1).astype(BF16)
    w2 = bd(cmp_w2).astype(BF16)
    w1r = cmp_w1.reshape(2, CMP_BLOCK * A_HEAD_DIM, A_HEAD_DIM)
    pe = cmp_pe.reshape(2, CMP_BLOCK * A_HEAD_DIM, 1)
    return w1ab, w2, w1r, pe


def _compress_chunk(get_x, nrows, w1_ref, w2_ref, w1r_ref, pe_ref, carry_ref, out_ref):
    row = lax.broadcasted_iota(jnp.int32, (nrows, 1), 0)
    for kind in range(2):
        peb = jnp.sum(pe_ref[kind] * w1r_ref[kind], axis=0, keepdims=True)
        peb = jnp.concatenate([peb] * HEADS_PER_LANE_ROW, axis=1)
        for half in range(2):
            acc = jnp.zeros((nrows, 2 * LANE), F32)
            for s in range(CMP_STRIDE):
                x = get_x(2 * kind + half, s).astype(BF16)
                acc = acc + jnp.dot(x, w1_ref[kind, s], preferred_element_type=F32)
            acc_a, acc_b = acc[:, :LANE], acc[:, LANE:]
            a_shift = jnp.where(row == 0, carry_ref[kind, half], pltpu.roll(acc_a, 1, 0))
            carry_ref[kind, half] = acc_a[nrows - 1:nrows]
            hid = jax.nn.gelu(a_shift + acc_b + peb)
            out_ref[0, kind, :, half * LANE:(half + 1) * LANE] = _dot(hid, w2_ref[kind])


def _cmp_prompt_kernel(kv_ref, w1_ref, w2_ref, w1r_ref, pe_ref, out_ref, carry_ref):
    nseg = kv_ref.shape[1] // (CMP_STRIDE * QUARTERS)
    carry_ref[...] = jnp.zeros_like(carry_ref)
    get_x = lambda quarter, s: kv_ref[0, pl.ds(QUARTERS * s + quarter, nseg, stride=CMP_STRIDE * QUARTERS), :]
    _compress_chunk(get_x, nseg, w1_ref, w2_ref, w1r_ref, pe_ref, carry_ref, out_ref)


def _cmp_weight_specs():
    zero = lambda n: (lambda *_: (0,) * n)
    return [pl.BlockSpec((2, CMP_STRIDE, LANE, 2 * LANE), zero(4)),
            pl.BlockSpec((2, LANE, LANE), zero(3)),
            pl.BlockSpec((2, CMP_BLOCK * A_HEAD_DIM, A_HEAD_DIM), zero(3)),
            pl.BlockSpec((2, CMP_BLOCK * A_HEAD_DIM, 1), zero(3))]


def compress_prompt(kv_cmp, cw):
    bsz, t, w = kv_cmp.shape
    nseg = t // CMP_STRIDE
    return pl.pallas_call(
        _cmp_prompt_kernel,
        out_shape=jax.ShapeDtypeStruct((bsz, 2, nseg, KV_WIDTH), F32),
        grid=(bsz,),
        in_specs=[pl.BlockSpec((1, t * QUARTERS, LANE), lambda b_: (b_, 0, 0))] + _cmp_weight_specs(),
        out_specs=pl.BlockSpec((1, 2, nseg, KV_WIDTH), lambda b_: (b_, 0, 0, 0)),
        scratch_shapes=[pltpu.VMEM((2, 2, 1, LANE), F32)],
        compiler_params=_cparams(("arbitrary",)),
        name="cmp_prompt",
    )(kv_cmp.reshape(bsz, t * QUARTERS, LANE), *cw)


def feature_major_pool(cache):
    n_pool = cache.shape[0]
    return jnp.transpose(cache, (0, 2, 3, 4, 1)).reshape(n_pool, 2 * KV_WIDTH, PAGE_SIZE)


def _page_copy(pool_ref, dst, sem_ref, pt_ref, step, slot, k, n_chunks, pages):
    b_ = step // n_chunks
    c = step % n_chunks
    pid = pt_ref[b_, c * pages + k]
    return pltpu.make_async_copy(pool_ref.at[pid], dst(slot, k), sem_ref.at[slot])


def _page_pipeline(pool_ref, dst, sem_ref, pt_ref, n_chunks, pages):
    step = pl.program_id(0) * n_chunks + pl.program_id(1)
    total = pl.num_programs(0) * n_chunks
    slot = step % 2

    def start(st, sl):
        for k in range(pages):
            _page_copy(pool_ref, dst, sem_ref, pt_ref, st, sl, k, n_chunks, pages).start()

    @pl.when(step == 0)
    def _():
        start(step, slot)

    @pl.when(step + 1 < total)
    def _():
        start(step + 1, 1 - slot)

    for k in range(pages):
        _page_copy(pool_ref, dst, sem_ref, pt_ref, step, slot, k, n_chunks, pages).wait()
    return slot


def _cmp_sample_kernel(pt_ref, pool_ref, w1_ref, w2_ref, w1r_ref, pe_ref, out_ref,
                       buf_ref, sem_ref, x_ref, carry_ref, *, n_chunks, pages):
    slot = _page_pipeline(pool_ref, lambda sl, k: buf_ref.at[sl, k], sem_ref, pt_ref, n_chunks, pages)

    @pl.when(pl.program_id(1) == 0)
    def _():
        carry_ref[...] = jnp.zeros_like(carry_ref)

    segs = PAGE_SIZE // CMP_STRIDE
    nrows = pages * segs
    dst = lax.broadcasted_iota(jnp.int32, (PAGE_SIZE, PAGE_SIZE), 0)
    src = lax.broadcasted_iota(jnp.int32, (PAGE_SIZE, PAGE_SIZE), 1)
    pick = jnp.where(src == (dst % segs) * CMP_STRIDE + dst // segs, 1.0, 0.0).astype(BF16)

    def relayout(p, carry):
        for quarter in range(QUARTERS):
            t = buf_ref[slot, p, quarter * LANE:(quarter + 1) * LANE, :]
            x_ref[quarter, p] = _dot_nt(pick, t)
        return carry

    lax.fori_loop(0, pages, relayout, 0)

    def get_x(quarter, s):
        return x_ref[quarter, :, s * segs:(s + 1) * segs, :].reshape(nrows, LANE)

    _compress_chunk(get_x, nrows, w1_ref, w2_ref, w1r_ref, pe_ref, carry_ref, out_ref)


def _page_chunk(n_pages):
    return math.gcd(n_pages, 32)


def compress_paged(pool, page_table, cw):
    dbs, n_pages = page_table.shape
    pages = _page_chunk(n_pages)
    n_chunks = n_pages // pages
    rows = pages * PAGE_SIZE // CMP_STRIDE
    return pl.pallas_call(
        functools.partial(_cmp_sample_kernel, n_chunks=n_chunks, pages=pages),
        out_shape=jax.ShapeDtypeStruct((dbs, 2, n_chunks * rows, KV_WIDTH), F32),
        grid_spec=pltpu.PrefetchScalarGridSpec(
            num_scalar_prefetch=1,
            grid=(dbs, n_chunks),
            in_specs=[pl.BlockSpec(memory_space=pl.ANY)] + _cmp_weight_specs(),
            out_specs=pl.BlockSpec((1, 2, rows, KV_WIDTH), lambda b_, c, pt: (b_, 0, c, 0)),
            scratch_shapes=[pltpu.VMEM((2, pages, 2 * KV_WIDTH, PAGE_SIZE), F32),
                            pltpu.SemaphoreType.DMA((2,)),
                            pltpu.VMEM((QUARTERS, pages, PAGE_SIZE, LANE), F32),
                            pltpu.VMEM((2, 2, 1, LANE), F32)]),
        compiler_params=_cparams(("arbitrary", "arbitrary")),
        name="cmp_paged",
    )(page_table, pool, *cw)


def _masked_softmax_rows(s, mask):
    s = jnp.where(mask, s, NEG_INF)
    m = jnp.max(s, axis=-1, keepdims=True)
    m = jnp.where(m == NEG_INF, 0.0, m)
    e = jnp.exp(s - m)
    return e / jnp.maximum(jnp.sum(e, axis=-1, keepdims=True), TINY)


def _topk_mask(score, k, axis):
    n = score.shape[axis]
    idx = lax.broadcasted_iota(jnp.int32, score.shape, axis)
    sel = jnp.zeros(score.shape, F32)
    for _ in range(k):
        mx = jnp.max(score, axis=axis, keepdims=True)
        first = jnp.min(jnp.where(score == mx, idx, n), axis=axis, keepdims=True)
        pick = idx == first
        sel = jnp.where(pick, 1.0, sel)
        score = jnp.where(pick, NEG_INF, score)
    return sel


def _flash_tile(carry, s, v, v_transposed=False):
    m_old, l_old, acc = carry
    m_new = jnp.maximum(m_old, jnp.max(s, axis=-1, keepdims=True))
    m_safe = jnp.where(m_new == NEG_INF, 0.0, m_new)
    p = jnp.exp(s - m_safe)
    alpha = jnp.exp(m_old - m_safe)
    pv = _dot_nt(p, v) if v_transposed else _dot(p, v)
    return m_new, alpha * l_old + jnp.sum(p, axis=-1, keepdims=True), alpha * acc + pv


MASKED = -1e30


def _flash_cols(carry, s, v_t):
    m_old, l_old, acc = carry
    m_new = jnp.maximum(m_old, jnp.max(s, axis=0, keepdims=True))
    p = jnp.exp(s - m_new)
    alpha = jnp.exp(m_old - m_new)
    return m_new, alpha * l_old + jnp.sum(p, axis=0, keepdims=True), alpha * acc + _dot(v_t, p)


def _flash_cols_init(cols, dv):
    return jnp.full((1, cols), MASKED, F32), jnp.zeros((1, cols), F32), jnp.zeros((dv, cols), F32)


def _flash_init(rows, dv):
    return jnp.full((rows, 1), NEG_INF, F32), jnp.zeros((rows, 1), F32), jnp.zeros((rows, dv), F32)


def _flash_out(carry):
    _, l, acc = carry
    return acc / jnp.maximum(l, TINY)


KEY_TILE = 512


def _nsa_prompt_kernel(q_ref, small_ref, cmp_ref, ks_ref, vs_ref, kw_ref, vw_ref, cov_ref, exp_ref, o_ref,
                       bias_ref, *, tq, n_slc):
    i = pl.program_id(1)
    q0 = i * tq
    tk = math.gcd(ks_ref.shape[1], KEY_TILE)
    cols = A_GROUP * tq
    q_t = (q_ref[0] * (A_HEAD_DIM ** -0.5)).T.astype(BF16)
    gate_t = jax.nn.sigmoid(small_ref[0]).T
    ncmp = cmp_ref.shape[2]
    qpos = q0 + lax.broadcasted_iota(jnp.int32, (1, tq), 1)
    jcol = lax.broadcasted_iota(jnp.int32, (ncmp, 1), 0)
    cmp_mask = (jcol >= 1) & (jcol * CMP_STRIDE + (CMP_BLOCK - CMP_STRIDE - 1) <= qpos)
    blk = lax.broadcasted_iota(jnp.int32, (n_slc, 1), 0)
    cur = qpos // SLC_BLOCK
    forced = (blk == 0) | (blk == cur) | (blk == cur - 1)
    future = blk * SLC_BLOCK > qpos
    krow = lax.broadcasted_iota(jnp.int32, (tk, 1), 0)
    lanes = lambda a, r: a[:, r * tq:(r + 1) * tq]
    pieces = []

    for g in range(A_KV_HEADS):
        gs = slice(g * A_HEAD_DIM, (g + 1) * A_HEAD_DIM)
        head = lambda r: slice((g * A_GROUP + r) * A_HEAD_DIM, (g * A_GROUP + r + 1) * A_HEAD_DIM)
        qg = jnp.concatenate([q_t[head(r)] for r in range(A_GROUP)], axis=1)

        s_c = _dot(cmp_ref[0, 0, :, gs], qg)
        p_r = []
        for r in range(A_GROUP):
            s_r = jnp.where(cmp_mask, lanes(s_c, r), NEG_INF)
            m = jnp.max(s_r, axis=0, keepdims=True)
            e = jnp.exp(s_r - jnp.where(m == NEG_INF, 0.0, m))
            p_r.append(e / jnp.maximum(jnp.sum(e, axis=0, keepdims=True), TINY))
        o_c = _dot_tn(cmp_ref[0, 1, :, gs], jnp.concatenate(p_r, axis=1))
        imp_t = _dot(cov_ref[...], p_r[0] + p_r[1] + p_r[2] + p_r[3])
        score = jnp.where(future, -BIG, imp_t + jnp.where(forced, BIG, 0.0))
        sel_t = _topk_mask(score, min(SLC_TOPK, n_slc), 0)
        sel_t = jnp.concatenate([sel_t, jnp.zeros((exp_ref.shape[1] - n_slc, tq), F32)], axis=0).astype(BF16)

        def bias_step(j, carry):
            off = pl.multiple_of(j * tk, tk)
            picked = jnp.dot(exp_ref[pl.ds(off, tk), :], sel_t, preferred_element_type=F32) > 0.5
            bias_ref[pl.ds(off, tk), :] = jnp.where(picked & (off + krow <= qpos), 0.0, MASKED)
            return carry

        lax.fori_loop(0, (q0 + tq + tk - 1) // tk, bias_step, 0)
        j_lo = jnp.maximum(q0 - (WINDOW - 1), 0) // tk

        def slc_step(j, carry):
            off = pl.multiple_of(j * tk, tk)
            s = jnp.dot(ks_ref[0, pl.ds(off, tk), gs], qg, preferred_element_type=F32)
            bias = bias_ref[pl.ds(off, tk), :]
            s = jnp.concatenate([lanes(s, r) + bias for r in range(A_GROUP)], axis=1)
            return _flash_cols(carry, s, vs_ref[0, gs, pl.ds(off, tk)])

        _, l_s, o_s = lax.fori_loop(0, (q0 + tq + tk - 1) // tk, slc_step, _flash_cols_init(cols, A_HEAD_DIM))

        def win_step(j, carry):
            off = pl.multiple_of(j * tk, tk)
            s = jnp.dot(kw_ref[0, pl.ds(off, tk), gs], qg, preferred_element_type=F32)
            kpos = off + krow
            mask = (kpos <= qpos) & (kpos > qpos - WINDOW)
            s = jnp.concatenate([jnp.where(mask, lanes(s, r), MASKED) for r in range(A_GROUP)], axis=1)
            return _flash_cols(carry, s, vw_ref[0, gs, pl.ds(off, tk)])

        _, l_w, o_w = lax.fori_loop(j_lo, (q0 + tq + tk - 1) // tk, win_step, _flash_cols_init(cols, A_HEAD_DIM))
        o_s = o_s / jnp.maximum(l_s, TINY)
        o_w = o_w / jnp.maximum(l_w, TINY)

        for r in range(A_GROUP):
            row = SMALL_AG + 3 * (g * A_GROUP + r)
            pieces.append(gate_t[row:row + 1] * lanes(o_c, r) + gate_t[row + 1:row + 2] * lanes(o_s, r) +
                          gate_t[row + 2:row + 3] * lanes(o_w, r))
    o_ref[0] = jnp.concatenate(pieces, axis=0).T


def _coverage(n_cmp_rows, n_slc):
    cs = (np.arange(n_cmp_rows) - 1) * CMP_STRIDE
    ss = np.arange(n_slc) * SLC_BLOCK
    lo = np.maximum(cs[:, None], ss[None, :])
    hi = np.minimum(cs[:, None] + CMP_BLOCK, ss[None, :] + SLC_BLOCK)
    cov = np.clip(hi - lo, 0, None) / CMP_BLOCK
    cov[0] = 0.0
    return cov.astype(np.float32)


def _block_expand(n_rows, n_keys):
    return (np.arange(n_rows)[:, None] == (np.arange(n_keys) // SLC_BLOCK)[None, :]).astype(np.float32)


def nsa_prompt(q_rot, z, cmp, kv_slc, kv_win, tq):
    bsz, t, _ = q_rot.shape
    n_slc = t // SLC_BLOCK
    ncmp = cmp.shape[2]
    kk = lambda kv: kv[..., :KV_WIDTH].astype(BF16)
    vt = lambda kv: jnp.swapaxes(kv[..., KV_WIDTH:], 1, 2).astype(BF16)
    cov_t = jnp.asarray(_coverage(ncmp, n_slc).T, BF16)
    n_exp = -(-n_slc // 128) * 128
    expand = jnp.asarray(_block_expand(n_exp, t).T, BF16)
    per_b = lambda shape: pl.BlockSpec((1,) + shape, lambda b_, i: (b_,) + (0,) * len(shape))
    return pl.pallas_call(
        functools.partial(_nsa_prompt_kernel, tq=tq, n_slc=n_slc),
        out_shape=jax.ShapeDtypeStruct((bsz, t, D_MODEL), F32),
        grid=(bsz, t // tq),
        in_specs=[pl.BlockSpec((1, tq, D_MODEL), lambda b_, i: (b_, i, 0)),
                  pl.BlockSpec((1, tq, 128), lambda b_, i: (b_, i, Z_SMALL // 128)),
                  per_b((2, ncmp, KV_WIDTH)),
                  per_b((t, KV_WIDTH)), per_b((KV_WIDTH, t)), per_b((t, KV_WIDTH)), per_b((KV_WIDTH, t)),
                  pl.BlockSpec((n_slc, ncmp), lambda b_, i: (0, 0)),
                  pl.BlockSpec((t, n_exp), lambda b_, i: (0, 0))],
        out_specs=pl.BlockSpec((1, tq, D_MODEL), lambda b_, i: (b_, i, 0)),
        scratch_shapes=[pltpu.VMEM((t, tq), F32)],
        compiler_params=_cparams(("arbitrary", "arbitrary")),
        name="nsa_prompt",
    )(q_rot, z, cmp, kk(kv_slc), vt(kv_slc), kk(kv_win), vt(kv_win), cov_t, expand)


SAMPLE_ROWS = 8
NEW_KEYS = 128


def _nsa_sample_kernel(pt_ref, qbd_ref, gl_ref, cmp_ref, pool_ref, knew_ref, wcache_ref, wnew_ref, cov_ref, exp_ref,
                       o_ref, buf_ref, sem_ref, sel_ref, m_ref, l_ref, acc_ref, oc_ref,
                       *, n_chunks, pages, past, t_real, n_slc):
    c = pl.program_id(1)
    page_window = lambda sl, k: buf_ref.at[sl, :, pl.ds(k * PAGE_SIZE, PAGE_SIZE)]
    slot = _page_pipeline(pool_ref, page_window, sem_ref, pt_ref, n_chunks, pages)
    qbd = qbd_ref[0]
    rows = qbd.shape[0]
    bpc = pages * PAGE_SIZE // SLC_BLOCK
    rq = lax.broadcasted_iota(jnp.int32, (rows, 1), 0) % SAMPLE_ROWS
    qpos = past + rq

    @pl.when(c == 0)
    def _():
        ncmp = cmp_ref.shape[2]
        nbp = cov_ref.shape[1]
        jrow = lax.broadcasted_iota(jnp.int32, (1, ncmp), 1)
        cmp_mask = (jrow >= 1) & (jrow * CMP_STRIDE + (CMP_BLOCK - CMP_STRIDE - 1) <= qpos)
        p = _masked_softmax_rows(_dot_nt(qbd, cmp_ref[0, 0]), cmp_mask)
        oc_ref[...] = _dot(p, cmp_ref[0, 1])
        ri = lax.broadcasted_iota(jnp.int32, (rows, rows), 0)
        ci = lax.broadcasted_iota(jnp.int32, (rows, rows), 1)
        group_rows = A_GROUP * SAMPLE_ROWS
        same = (ri // group_rows == ci // group_rows) & (ri % SAMPLE_ROWS == ci % SAMPLE_ROWS)
        p_group = _dot(jnp.where(same, 1.0, 0.0), p)
        imp = _dot(p_group, cov_ref[...])
        blk = lax.broadcasted_iota(jnp.int32, (1, nbp), 1)
        cur = qpos // SLC_BLOCK
        forced = (blk == 0) | (blk == cur) | (blk == cur - 1)
        future = blk * SLC_BLOCK > qpos
        score = jnp.where(future, -BIG, imp + jnp.where(forced, BIG, 0.0))
        score = jnp.where(blk < n_slc, score, NEG_INF)
        sel = _topk_mask(score, min(SLC_TOPK, n_slc), 1)
        pad = jnp.zeros((rows, 128 - bpc), F32)
        for cc in range(n_chunks + 1):
            sel_ref[cc] = jnp.concatenate([sel[:, cc * bpc:(cc + 1) * bpc], pad], axis=1).astype(BF16)
        m_ref[...] = jnp.full(m_ref.shape, NEG_INF, F32)
        l_ref[...] = jnp.zeros_like(l_ref)
        acc_ref[...] = jnp.zeros_like(acc_ref)

    k_t = buf_ref[slot, 0:KV_WIDTH, :]
    v_t = buf_ref[slot, KV_WIDTH:2 * KV_WIDTH, :]
    picked = jnp.dot(sel_ref[c], exp_ref[...], preferred_element_type=F32) > 0.5
    s = jnp.where(picked, _dot(qbd, k_t), NEG_INF)
    carry = _flash_tile((m_ref[...], l_ref[...], acc_ref[...]), s, v_t, v_transposed=True)
    m_ref[...], l_ref[...], acc_ref[...] = carry

    @pl.when(c == n_chunks - 1)
    def _():
        zpad = jnp.zeros((NEW_KEYS - SAMPLE_ROWS, 2 * KV_WIDTH), F32)
        kcol = lax.broadcasted_iota(jnp.int32, (1, NEW_KEYS), 1)
        new_mask = (kcol <= rq) & (kcol < t_real)
        knew = jnp.concatenate([knew_ref[0], zpad], axis=0)
        last_picked = sel_ref[n_chunks][:, 0:1].astype(F32) > 0.5
        s_new = jnp.where(new_mask & last_picked, _dot_nt(qbd, knew[:, :KV_WIDTH]), NEG_INF)
        o_s = _flash_out(_flash_tile((m_ref[...], l_ref[...], acc_ref[...]), s_new, knew[:, KV_WIDTH:]))

        wc = wcache_ref[0]
        wb = wc.shape[0]
        wcol = lax.broadcasted_iota(jnp.int32, (1, wb), 1)
        s_w = jnp.where(wcol > rq + (wb - WINDOW), _dot_nt(qbd, wc[:, :KV_WIDTH]), NEG_INF)
        cw = _flash_tile(_flash_init(rows, KV_WIDTH), s_w, wc[:, KV_WIDTH:])
        wnew = jnp.concatenate([wnew_ref[0], zpad], axis=0)
        s_wn = jnp.where(new_mask, _dot_nt(qbd, wnew[:, :KV_WIDTH]), NEG_INF)
        o_w = _flash_out(_flash_tile(cw, s_wn, wnew[:, KV_WIDTH:]))

        gate = jax.nn.sigmoid(gl_ref[0])
        o = gate[:, 0:1] * oc_ref[...] + gate[:, 1:2] * o_s + gate[:, 2:3] * o_w
        lane_g = lax.broadcasted_iota(jnp.int32, (1, KV_WIDTH), 1) // A_HEAD_DIM
        row_g = lax.broadcasted_iota(jnp.int32, (rows, 1), 0) // (A_GROUP * SAMPLE_ROWS)
        o = jnp.where(lane_g == row_g, o, 0.0)
        o_ref[0] = sum(o[:, g * A_HEAD_DIM:(g + 1) * A_HEAD_DIM] for g in range(A_KV_HEADS))


def nsa_sample(q_rot, z, cmp, pool, page_table, k_new, win_cache, w_new, past, t_real):
    dbs = q_rot.shape[0]
    n_pages = page_table.shape[1]
    pages = _page_chunk(n_pages)
    n_chunks = n_pages // pages
    bpc = pages * PAGE_SIZE // SLC_BLOCK
    ncmp = cmp.shape[2]
    n_slc = -(-(past + t_real) // SLC_BLOCK)
    assert n_slc == n_chunks * bpc + 1 and bpc <= 128
    nbp = -(-((n_chunks + 1) * bpc) // 128) * 128
    rows = A_HEADS * SAMPLE_ROWS
    q5 = q_rot.reshape(dbs, SAMPLE_ROWS, A_KV_HEADS, A_GROUP, A_HEAD_DIM) * (A_HEAD_DIM ** -0.5)
    qbd = jnp.einsum("bqgrd,gh->bgrqhd", q5, jnp.eye(A_KV_HEADS, dtype=F32)).reshape(dbs, rows, KV_WIDTH).astype(BF16)
    gl = z[..., Z_SMALL + SMALL_AG:Z_SMALL + SMALL_AG + 3 * A_HEADS].reshape(dbs, SAMPLE_ROWS, A_HEADS, 3)
    gl = jnp.swapaxes(gl, 1, 2).reshape(dbs, rows, 3)
    cov = np.zeros((ncmp, nbp), np.float32)
    cov[:, :n_slc] = _coverage(ncmp, n_slc)
    expand = jnp.asarray(_block_expand(128, pages * PAGE_SIZE), BF16)
    per_b = lambda shape: pl.BlockSpec((1,) + shape, lambda b_, c, pt: (b_,) + (0,) * len(shape))
    const = lambda shape: pl.BlockSpec(shape, lambda b_, c, pt: (0,) * len(shape))
    wb = win_cache.shape[1]
    out = pl.pallas_call(
        functools.partial(_nsa_sample_kernel, n_chunks=n_chunks, pages=pages, past=past, t_real=t_real, n_slc=n_slc),
        out_shape=jax.ShapeDtypeStruct((dbs, rows, A_HEAD_DIM), F32),
        grid_spec=pltpu.PrefetchScalarGridSpec(
            num_scalar_prefetch=1,
            grid=(dbs, n_chunks),
            in_specs=[per_b((rows, KV_WIDTH)), per_b((rows, 3)), per_b((2, ncmp, KV_WIDTH)),
                      pl.BlockSpec(memory_space=pl.ANY),
                      per_b((SAMPLE_ROWS, 2 * KV_WIDTH)), per_b((wb, 2 * KV_WIDTH)), per_b((SAMPLE_ROWS, 2 * KV_WIDTH)),
                      const((ncmp, nbp)), const((128, pages * PAGE_SIZE))],
            out_specs=per_b((rows, A_HEAD_DIM)),
            scratch_shapes=[pltpu.VMEM((2, 2 * KV_WIDTH, pages * PAGE_SIZE), F32),
                            pltpu.SemaphoreType.DMA((2,)),
                            pltpu.VMEM((n_chunks + 1, rows, 128), BF16),
                            pltpu.VMEM((rows, 1), F32), pltpu.VMEM((rows, 1), F32),
                            pltpu.VMEM((rows, KV_WIDTH), F32), pltpu.VMEM((rows, KV_WIDTH), F32)]),
        compiler_params=_cparams(("arbitrary", "arbitrary")),
        name="nsa_sample",
    )(page_table, qbd, gl, cmp, pool, k_new, win_cache, w_new, jnp.asarray(cov, BF16), expand)
    out = jnp.swapaxes(out.reshape(dbs, A_HEADS, SAMPLE_ROWS, A_HEAD_DIM), 1, 2)
    return out.reshape(dbs, SAMPLE_ROWS, D_MODEL)


def _kv_rows(a, bsz, t):
    return a.reshape(bsz, t, 2, A_KV_HEADS, A_HEAD_DIM)


def kernel(x_prompt, x_sample, cache_cmp_kv, cache_slc_kv, cache_win_kv, state_C, state_n, state_m, state_conv,
           page_table, c_prompt, c_sample, w_ada, b_ada, g_pre_mix, g_post_mix, g_pre_ffn, g_post_ffn, w_in, b_in,
           m_norm_w, cmp_w1, cmp_pe, cmp_w2, w_branch_m, w_branch_a, w_out, w_up, conv_w, conv_b, w_down):
    depth = w_ada.shape[0]
    bsz, t, d = x_prompt.shape
    dbs, ts, _ = x_sample.shape
    n_pages = page_table.shape[1]
    past = n_pages * PAGE_SIZE
    assert ts <= SAMPLE_ROWS and (past + ts) // CMP_STRIDE == past // CMP_STRIDE and past >= WINDOW
    srows = dbs * SAMPLE_ROWS
    xp = x_prompt.astype(F32)
    xs = jnp.pad(x_sample.astype(F32), ((0, 0), (0, SAMPLE_ROWS - ts), (0, 0))).reshape(1, srows, d)
    c_all = jnp.concatenate([c_prompt, c_sample], axis=0).astype(F32)
    c_all = jnp.pad(c_all, ((0, (-c_all.shape[0]) % 8), (0, 0)))
    tab_p = rope_tables(jnp.arange(t, dtype=jnp.int32))
    tab_s = rope_tables(jnp.tile(past + jnp.arange(SAMPLE_ROWS, dtype=jnp.int32), dbs))
    lchunk = math.gcd(t, 256)
    tm_p = math.gcd(t, 512)
    p_states, s_states = [], []
    for l in range(depth):
        mod = ada_modulation(c_all, w_ada[l], b_ada[l])
        mod_p = [m[:, None, :] for m in jnp.split(mod[:bsz], 6, axis=-1)]
        mod_s = [jnp.repeat(m, SAMPLE_ROWS, axis=0)[None] for m in jnp.split(mod[bsz:bsz + dbs], 6, axis=-1)]
        w_r, b_r = regroup_in_weights(w_in[l], b_in[l])
        cw = compress_weights(cmp_w1[l], cmp_pe[l], cmp_w2[l])
        wm, wa, wo = w_branch_m[l].astype(BF16), w_branch_a[l].astype(BF16), w_out[l].astype(BF16)
        wu, wd = w_up[l].astype(BF16), w_down[l].astype(BF16)

        sh_m, sc_m, gt_m, sh_f, sc_f, gt_f = mod_p
        z = in_projection(xp, g_pre_mix[l], sc_m, sh_m, w_r, b_r, tm=math.gcd(t, 1024))
        q_rot, kv_cmp, kv_slc, kv_win = rope_split(z, tab_p, tm=tm_p)
        hm, p_c, p_n, p_m = mlstm(z, m_norm_w[l], jnp.zeros((bsz, M_HEADS, M_HEAD_DIM, M_HEAD_DIM), F32),
                                  jnp.zeros((bsz, M_HEADS, M_HEAD_DIM), F32), jnp.zeros((bsz, M_HEADS), F32),
                                  lb=lchunk, lp=lchunk, t_real=lchunk)
        ha = nsa_prompt(q_rot, z, compress_prompt(kv_cmp, cw), kv_slc, kv_win, tq=256)
        xp = mix_out(xp, hm, ha, z, gt_m, g_post_mix[l], wm, wa, wo, tm=tm_p)
        xp, p_conv = conv_ffn(xp, g_pre_ffn[l], sc_f, sh_f, gt_f, g_post_ffn[l], wu, conv_w[l], conv_b[l], wd, tm=tm_p)
        wkeep = min(WINDOW, t)
        p_states.append((_kv_rows(kv_cmp, bsz, t), _kv_rows(kv_slc, bsz, t), _kv_rows(kv_win[:, t - wkeep:], bsz, wkeep),
                         p_c, p_n, p_m, p_conv))

        sh_m, sc_m, gt_m, sh_f, sc_f, gt_f = mod_s
        z = in_projection(xs, g_pre_mix[l], sc_m, sh_m, w_r, b_r, tm=srows)
        q_rot, kv_cmp, kv_slc, kv_win = rope_split(z, tab_s, tm=srows)
        z3 = z.reshape(dbs, SAMPLE_ROWS, Z_WIDTH)
        hm, s_c, s_n, s_m = mlstm(z3, m_norm_w[l], state_C[l].astype(F32), state_n[l].astype(F32),
                                  state_m[l].astype(F32), lb=SAMPLE_ROWS, lp=128, t_real=ts)
        cmp_s = compress_paged(feature_major_pool(cache_cmp_kv[l].astype(F32)), page_table, cw)
        new3 = lambda a: a.reshape(dbs, SAMPLE_ROWS, 2 * KV_WIDTH)
        win_cache = cache_win_kv[l].astype(F32).reshape(dbs, -1, 2 * KV_WIDTH)
        ha = nsa_sample(q_rot.reshape(dbs, SAMPLE_ROWS, d), z3, cmp_s, feature_major_pool(cache_slc_kv[l].astype(F32)),
                        page_table, new3(kv_slc), win_cache, new3(kv_win), past, ts)
        xs = mix_out(xs, hm.reshape(1, srows, d), ha.reshape(1, srows, d), z, gt_m, g_post_mix[l], wm, wa, wo, tm=srows)
        st = state_conv[l].astype(F32)
        s2 = jnp.pad(st, ((0, 0), (0, SAMPLE_ROWS - (CONV_W - 1)), (0, 0))).reshape(1, srows, 2 * D_FF)
        s1 = jnp.pad(st[:, 1:], ((0, 0), (0, SAMPLE_ROWS - 1), (0, 0))).reshape(1, srows, 2 * D_FF)
        xs, u = conv_ffn(xs, g_pre_ffn[l], sc_f, sh_f, gt_f, g_post_ffn[l], wu, conv_w[l], conv_b[l], wd, tm=srows,
                         state_rows=(s1, s2))
        wb = win_cache.shape[1]
        s_win = jnp.concatenate([win_cache, new3(kv_win)[:, :ts]], axis=1)[:, ts:]
        s_conv = jnp.concatenate([st, u.reshape(dbs, SAMPLE_ROWS, 2 * D_FF)[:, :ts]], axis=1)[:, ts:]
        s_states.append((_kv_rows(new3(kv_cmp)[:, :ts], dbs, ts), _kv_rows(new3(kv_slc)[:, :ts], dbs, ts),
                         _kv_rows(s_win, dbs, wb), s_c, s_n, s_m, s_conv))

    stack = lambda states: [jnp.stack([s[i] for s in states]) for i in range(7)]
    y_sample = xs.reshape(dbs, SAMPLE_ROWS, d)[:, :ts]
    return (xp, y_sample, *stack(p_states), *stack(s_states))
```

```python
import functools
import math

import numpy as np
import jax
import jax.numpy as jnp
from jax import lax
from jax.experimental import pallas as pl
from jax.experimental.pallas import tpu as pltpu

F32 = jnp.float32
BF16 = jnp.bfloat16

D_MODEL = 1024
M_HEADS = 4
M_HEAD_DIM = 256
A_HEADS = 16
A_HEAD_DIM = 64
A_KV_HEADS = 4
A_GROUP = 4
KV_WIDTH = A_KV_HEADS * A_HEAD_DIM
CMP_STRIDE = 16
CMP_BLOCK = 32
SLC_BLOCK = 64
SLC_TOPK = 16
WINDOW = 512
ROPE_THETA = 500000.0
ROPE_DIM = 16
BIG = 1e6
D_FF = 2816
CONV_W = 3
NORM_EPS = 1e-6
PAGE_SIZE = 128
NEG_INF = float("-inf")
TINY = float(np.finfo(np.float32).tiny)

Z_MQ, Z_MK, Z_MV, Z_MO, Z_AQ, Z_GA, Z_GB, Z_AKV, Z_SMALL = 0, 1024, 2048, 3072, 4096, 5120, 6144, 7168, 8704
Z_WIDTH = 9216
SMALL_MI, SMALL_MF, SMALL_AG = 0, 4, 8

VMEM_LIMIT = 48 * 1024 * 1024


def _cparams(sem):
    return pltpu.CompilerParams(dimension_semantics=sem, vmem_limit_bytes=VMEM_LIMIT)


def _dot(a, b):
    return jnp.dot(a.astype(BF16), b.astype(BF16), preferred_element_type=F32)


def _dot_nt(a, b):
    return lax.dot_general(a.astype(BF16), b.astype(BF16), (((1,), (1,)), ((), ())), preferred_element_type=F32)


def _dot_tn(a, b):
    return lax.dot_general(a.astype(BF16), b.astype(BF16), (((0,), (0,)), ((), ())), preferred_element_type=F32)


def _split3(x):
    x1 = x.astype(BF16)
    r1 = x - x1.astype(F32)
    x2 = r1.astype(BF16)
    x3 = (r1 - x2.astype(F32)).astype(BF16)
    return x1, x2, x3


def _rms(x, g):
    return x * lax.rsqrt(jnp.mean(x * x, axis=-1, keepdims=True) + NORM_EPS) * g


def _ada_kernel(c_ref, w_ref, b_ref, o_ref):
    c = c_ref[...]
    o_ref[...] = _dot(c * jax.nn.sigmoid(c), w_ref[...]) + b_ref[...]


def ada_modulation(c, w_ada, b_ada):
    rows, d = c.shape
    n = w_ada.shape[1]
    tn = 512
    return pl.pallas_call(
        _ada_kernel,
        out_shape=jax.ShapeDtypeStruct((rows, n), F32),
        grid=(n // tn,),
        in_specs=[pl.BlockSpec((rows, d), lambda j: (0, 0)),
                  pl.BlockSpec((d, tn), lambda j: (0, j)),
                  pl.BlockSpec((1, tn), lambda j: (0, j))],
        out_specs=pl.BlockSpec((rows, tn), lambda j: (0, j)),
        compiler_params=_cparams(("arbitrary",)),
        name="ada",
    )(c, w_ada, b_ada.reshape(1, n))


def _inproj_kernel(x_ref, g_ref, sc_ref, sh_ref, w_ref, b_ref, o_ref, h_ref):
    @pl.when(pl.program_id(2) == 0)
    def _():
        h = _rms(x_ref[0], g_ref[...]) * (1.0 + sc_ref[0]) + sh_ref[0]
        h_ref[...] = h.astype(BF16)

    o_ref[0] = jnp.dot(h_ref[...], w_ref[...], preferred_element_type=F32) + b_ref[...]


def in_projection(x, g, sc, sh, w_bf16, b, tm):
    bsz, t, d = x.shape
    r = sc.shape[1]
    rb = 1 if r == 1 else tm
    tn = 1024
    mod_spec = pl.BlockSpec((1, rb, d), (lambda b_, i, j: (b_, 0, 0)) if r == 1 else (lambda b_, i, j: (b_, i, 0)))
    return pl.pallas_call(
        _inproj_kernel,
        out_shape=jax.ShapeDtypeStruct((bsz, t, Z_WIDTH), F32),
        grid=(bsz, t // tm, Z_WIDTH // tn),
        in_specs=[pl.BlockSpec((1, tm, d), lambda b_, i, j: (b_, i, 0)),
                  pl.BlockSpec((1, d), lambda b_, i, j: (0, 0)),
                  mod_spec, mod_spec,
                  pl.BlockSpec((d, tn), lambda b_, i, j: (0, j)),
                  pl.BlockSpec((1, tn), lambda b_, i, j: (0, j))],
        out_specs=pl.BlockSpec((1, tm, tn), lambda b_, i, j: (b_, i, j)),
        scratch_shapes=[pltpu.VMEM((tm, d), BF16)],
        compiler_params=_cparams(("arbitrary", "arbitrary", "arbitrary")),
        name="inproj",
    )(x, g.reshape(1, d), sc, sh, w_bf16, b.reshape(1, Z_WIDTH))


def regroup_in_weights(w_in, b_in):
    mw = M_HEADS * M_HEAD_DIM
    o_mi = 4 * mw
    o_aq = o_mi + 2 * M_HEADS
    o_akv = o_aq + A_HEADS * A_HEAD_DIM
    o_ag = o_akv + 6 * KV_WIDTH
    o_ga = o_ag + 3 * A_HEADS
    o_gb = o_ga + D_MODEL

    def regroup(a):
        lead = a.shape[:-1]
        parts = [a[..., :o_mi], a[..., o_aq:o_akv], a[..., o_ga:o_gb], a[..., o_gb:o_gb + D_MODEL],
                 a[..., o_akv:o_ag], a[..., o_mi:o_aq], a[..., o_ag:o_ga],
                 jnp.zeros(lead + (128 - 2 * M_HEADS - 3 * A_HEADS,), a.dtype),
                 jnp.zeros(lead + (Z_WIDTH - Z_SMALL - 128,), a.dtype)]
        return jnp.concatenate(parts, axis=-1)

    return regroup(w_in).astype(BF16), regroup(b_in)


def rope_tables(pos):
    half = ROPE_DIM // 2
    inv_freq = ROPE_THETA ** (-jnp.arange(half, dtype=F32) / half)
    ang = pos.astype(F32)[:, None] * inv_freq
    cos, sin = jnp.cos(ang), jnp.sin(ang)
    rows = pos.shape[0]
    zeros = jnp.zeros((rows, half), F32)
    rest1 = jnp.ones((rows, A_HEAD_DIM - ROPE_DIM), F32)
    rest0 = jnp.zeros((rows, A_HEAD_DIM - ROPE_DIM), F32)
    c = jnp.concatenate([cos, cos, rest1], axis=1)
    sa = jnp.concatenate([zeros, sin, rest0], axis=1)
    sb = jnp.concatenate([-sin, zeros, rest0], axis=1)
    return tuple(jnp.concatenate([a, a], axis=1) for a in (c, sa, sb))


def _rope_apply(x, c, sa, sb):
    w = x.shape[1]
    n = w // 128
    ct, sat, sbt = (jnp.concatenate([a] * n, axis=1) for a in (c, sa, sb))
    return x * ct + pltpu.roll(x, ROPE_DIM // 2, 1) * sat + pltpu.roll(x, w - ROPE_DIM // 2, 1) * sbt


def _rope_kernel(q_ref, c_ref, s_ref, w_ref, cos_ref, sa_ref, sb_ref, qo_ref, co_ref, so_ref, wo_ref):
    c, sa, sb = cos_ref[...], sa_ref[...], sb_ref[...]
    qo_ref[0] = _rope_apply(q_ref[0], c, sa, sb)
    for src, dst in ((c_ref, co_ref), (s_ref, so_ref), (w_ref, wo_ref)):
        kv = src[0]
        dst[0] = jnp.concatenate([_rope_apply(kv[:, :KV_WIDTH], c, sa, sb), kv[:, KV_WIDTH:]], axis=1)


def rope_split(z, tables, tm):
    bsz, t, _ = z.shape
    nt = t // tm
    kvw = 2 * KV_WIDTH
    tab_spec = pl.BlockSpec((tm, 128), lambda b_, i: (i, 0))
    return pl.pallas_call(
        _rope_kernel,
        out_shape=(jax.ShapeDtypeStruct((bsz, t, D_MODEL), F32),) + (jax.ShapeDtypeStruct((bsz, t, kvw), F32),) * 3,
        grid=(bsz, nt),
        in_specs=[pl.BlockSpec((1, tm, D_MODEL), lambda b_, i: (b_, i, Z_AQ // D_MODEL)),
                  pl.BlockSpec((1, tm, kvw), lambda b_, i: (b_, i, Z_AKV // kvw)),
                  pl.BlockSpec((1, tm, kvw), lambda b_, i: (b_, i, Z_AKV // kvw + 1)),
                  pl.BlockSpec((1, tm, kvw), lambda b_, i: (b_, i, Z_AKV // kvw + 2)),
                  tab_spec, tab_spec, tab_spec],
        out_specs=(pl.BlockSpec((1, tm, D_MODEL), lambda b_, i: (b_, i, 0)),) +
                  (pl.BlockSpec((1, tm, kvw), lambda b_, i: (b_, i, 0)),) * 3,
        compiler_params=_cparams(("arbitrary", "arbitrary")),
        name="rope",
    )(z, z, z, z, *tables)


def _mlstm_kernel(q_ref, k_ref, v_ref, o_ref, s_ref, nw_ref, c0_ref, n0_ref, m0_ref,
                  h_ref, c_ref, n_ref, m_ref, *, lb, lp, t_real):
    @pl.when(pl.program_id(1) == 0)
    def _():
        c_ref[...] = c0_ref[...]
        n_ref[...] = n0_ref[...]
        m_ref[...] = m0_ref[...]

    def pad(a):
        if lb == lp:
            return a
        return jnp.concatenate([a, jnp.zeros((lp - lb, a.shape[1]), a.dtype)], axis=0)

    small = pad(s_ref[0])
    small_t = small.T
    row_c = lax.broadcasted_iota(jnp.int32, (lp, 1), 0)
    row_r = lax.broadcasted_iota(jnp.int32, (1, lp), 1)
    li_col_all = jnp.where(row_c < t_real, small, NEG_INF)
    lf_col_all = jnp.where(row_c < t_real, jax.nn.log_sigmoid(small), 0.0)
    li_row_all = jnp.where(row_r < t_real, small_t[0:8], NEG_INF)
    lf_row_all = jnp.where(row_r < t_real, jax.nn.log_sigmoid(small_t[0:8]), 0.0)
    rr = lax.broadcasted_iota(jnp.int32, (lp, lp), 0)
    cc = lax.broadcasted_iota(jnp.int32, (lp, lp), 1)
    causal = cc <= rr
    tril = jnp.where(causal, 1.0, 0.0).astype(BF16)
    triu = jnp.where(rr <= cc, 1.0, 0.0).astype(BF16)
    b_col_all = sum(jnp.dot(tril, p, preferred_element_type=F32) for p in _split3(lf_col_all))
    b_row_all = sum(jnp.dot(p, triu, preferred_element_type=F32) for p in _split3(lf_row_all))

    q_all, k_all, v_all, o_all = pad(q_ref[0]), pad(k_ref[0]), pad(v_ref[0]), pad(o_ref[0])
    nw = nw_ref[...]
    for h in range(M_HEADS):
        hs = slice(h * M_HEAD_DIM, (h + 1) * M_HEAD_DIM)
        qf = q_all[:, hs]
        kf = k_all[:, hs] * (M_HEAD_DIM ** -0.5)
        vf = v_all[:, hs]
        li_row = li_row_all[SMALL_MI + h:SMALL_MI + h + 1, :]
        b_row = b_row_all[SMALL_MF + h:SMALL_MF + h + 1, :]
        li_col = li_col_all[:, SMALL_MI + h:SMALL_MI + h + 1]
        b_col = b_col_all[:, SMALL_MF + h:SMALL_MF + h + 1]
        m_prev = m_ref[0, h]
        c_prev = c_ref[0, h]
        n_prev = n_ref[0, h]

        dlog = jnp.where(causal, b_col - b_row + li_row, NEG_INF)
        inter = m_prev + b_col
        mt = jnp.maximum(inter, jnp.max(dlog, axis=1, keepdims=True))
        a = jnp.exp(inter - mt)
        s = _dot_nt(qf, kf) * jnp.exp(dlog - mt)
        num = a * _dot_nt(qf, c_prev) + _dot(s, vf)
        den = a * jnp.sum(qf * n_prev, axis=1, keepdims=True) + jnp.sum(s, axis=1, keepdims=True)
        hh = num / jnp.maximum(jnp.abs(den), jnp.exp(-mt))
        mu = jnp.mean(hh, axis=1, keepdims=True)
        var = jnp.mean(jnp.square(hh - mu), axis=1, keepdims=True)
        out = (hh - mu) * lax.rsqrt(var + NORM_EPS) * nw[:, hs] * jax.nn.sigmoid(o_all[:, hs])
        h_ref[0, :, hs] = out[:lb]

        bl = b_row[:, lp - 1:lp]
        wlog = bl - b_col + li_col
        m_new = jnp.maximum(m_prev + bl, jnp.max(wlog, axis=0, keepdims=True))
        w = jnp.exp(wlog - m_new)
        decay = jnp.exp(m_prev + bl - m_new)
        c_ref[0, h] = decay * c_prev + _dot_tn(vf * w, kf)
        n_ref[0, h] = decay * n_prev + jnp.sum(w * kf, axis=0, keepdims=True)
        m_ref[0, h] = m_new


def mlstm(z, norm_w, c0, n0, m0, lb, lp, t_real):
    bsz, t, _ = z.shape
    nc = t // lb
    mw = M_HEADS * M_HEAD_DIM
    zspec = lambda col: pl.BlockSpec((1, lb, mw), lambda b_, c: (b_, c, col // mw))
    cst = lambda shape: pl.BlockSpec((1,) + shape, lambda b_, c: (b_,) + (0,) * len(shape))
    h, c, n, m = pl.pallas_call(
        functools.partial(_mlstm_kernel, lb=lb, lp=lp, t_real=t_real),
        out_shape=(jax.ShapeDtypeStruct((bsz, t, mw), F32),
                   jax.ShapeDtypeStruct((bsz, M_HEADS, M_HEAD_DIM, M_HEAD_DIM), F32),
                   jax.ShapeDtypeStruct((bsz, M_HEADS, 1, M_HEAD_DIM), F32),
                   jax.ShapeDtypeStruct((bsz, M_HEADS, 1, 1), F32)),
        grid=(bsz, nc),
        in_specs=[zspec(Z_MQ), zspec(Z_MK), zspec(Z_MV), zspec(Z_MO),
                  pl.BlockSpec((1, lb, 128), lambda b_, c: (b_, c, Z_SMALL // 128)),
                  pl.BlockSpec((1, mw), lambda b_, c: (0, 0)),
                  cst((M_HEADS, M_HEAD_DIM, M_HEAD_DIM)), cst((M_HEADS, 1, M_HEAD_DIM)), cst((M_HEADS, 1, 1))],
        out_specs=(pl.BlockSpec((1, lb, mw), lambda b_, c: (b_, c, 0)),
                   cst((M_HEADS, M_HEAD_DIM, M_HEAD_DIM)), cst((M_HEADS, 1, M_HEAD_DIM)), cst((M_HEADS, 1, 1))),
        compiler_params=_cparams(("arbitrary", "arbitrary")),
        name="mlstm",
    )(z, z, z, z, z, norm_w.reshape(1, mw), c0, n0.reshape(bsz, M_HEADS, 1, M_HEAD_DIM),
      m0.reshape(bsz, M_HEADS, 1, 1))
    return h, c, n.reshape(bsz, M_HEADS, M_HEAD_DIM), m.reshape(bsz, M_HEADS)


def _mix_kernel(x_ref, hm_ref, ha_ref, ga_ref, gb_ref, gt_ref, g_ref, wm_ref, wa_ref, wo_ref, o_ref):
    mixed = (jax.nn.sigmoid(ga_ref[0]) * _dot(hm_ref[0], wm_ref[...]) +
             jax.nn.sigmoid(gb_ref[0]) * _dot(ha_ref[0], wa_ref[...]))
    o_ref[0] = x_ref[0] + gt_ref[0] * _rms(_dot(mixed, wo_ref[...]), g_ref[...])


def mix_out(x, hm, ha, z, gt, g_post, wm, wa, wo, tm):
    bsz, t, d = x.shape
    r = gt.shape[1]
    rb = 1 if r == 1 else tm
    row = lambda col=0: pl.BlockSpec((1, tm, d), lambda b_, i: (b_, i, col // d))
    mod = pl.BlockSpec((1, rb, d), (lambda b_, i: (b_, 0, 0)) if r == 1 else (lambda b_, i: (b_, i, 0)))
    wsp = pl.BlockSpec((d, d), lambda b_, i: (0, 0))
    return pl.pallas_call(
        _mix_kernel,
        out_shape=jax.ShapeDtypeStruct((bsz, t, d), F32),
        grid=(bsz, t // tm),
        in_specs=[row(), row(), row(), row(Z_GA), row(Z_GB), mod,
                  pl.BlockSpec((1, d), lambda b_, i: (0, 0)), wsp, wsp, wsp],
        out_specs=row(),
        compiler_params=_cparams(("arbitrary", "arbitrary")),
        name="mix",
    )(x, hm, ha, z, z, gt, g_post.reshape(1, d), wm, wa, wo)


FF_CHUNK = 256


def _ffn_kernel(*refs, per_row_state):
    if per_row_state:
        (x_ref, g_ref, sc_ref, sh_ref, gt_ref, gp_ref, wa_ref, wg_ref, cwa_ref, cwg_ref, cba_ref, cbg_ref, wd_ref,
         s1a_ref, s1g_ref, s2a_ref, s2g_ref, y_ref, ua_ref, ug_ref, h_ref, acc_ref) = refs
    else:
        (x_ref, g_ref, sc_ref, sh_ref, gt_ref, gp_ref, wa_ref, wg_ref, cwa_ref, cwg_ref, cba_ref, cbg_ref, wd_ref,
         y_ref, ua_ref, ug_ref, h_ref, acc_ref, carry_ref) = refs
    i = pl.program_id(1)
    f = pl.program_id(2)
    tm = x_ref.shape[1]

    @pl.when(f == 0)
    def _():
        h = _rms(x_ref[0], g_ref[...]) * (1.0 + sc_ref[0]) + sh_ref[0]
        h_ref[...] = h.astype(BF16)
        acc_ref[...] = jnp.zeros_like(acc_ref)

    if not per_row_state:
        @pl.when(i == 0)
        def _():
            carry_ref[f] = jnp.zeros(carry_ref.shape[1:], F32)

    row = lax.broadcasted_iota(jnp.int32, (tm, 1), 0)

    def branch(w_ref, cw_ref, cb_ref, part, s1_ref=None, s2_ref=None):
        u = jnp.dot(h_ref[...], w_ref[...], preferred_element_type=F32)
        r1 = pltpu.roll(u, 1, 0)
        r2 = pltpu.roll(u, 2, 0)
        if per_row_state:
            t = row % 8
            u1 = jnp.where(t < 1, s1_ref[0], r1)
            u2 = jnp.where(t < 2, s2_ref[0], r2)
        else:
            prev = carry_ref[f, part]
            top = row[0:8]
            u1 = jnp.concatenate([jnp.where(top < 1, prev[1:2], r1[0:8]), r1[8:]], axis=0)
            u2 = jnp.concatenate([jnp.where(top < 1, prev[0:1], jnp.where(top < 2, prev[1:2], r2[0:8])), r2[8:]],
                                 axis=0)
            carry_ref[f, part, 0:2] = u[tm - 2:tm]
        cw = cw_ref[...]
        return u, cb_ref[...] + cw[0:1] * u2 + cw[1:2] * u1 + cw[2:3] * u

    if per_row_state:
        u_a, conv_a = branch(wa_ref, cwa_ref, cba_ref, 0, s1a_ref, s2a_ref)
        u_g, conv_g = branch(wg_ref, cwg_ref, cbg_ref, 1, s1g_ref, s2g_ref)
        ua_ref[0] = u_a
        ug_ref[0] = u_g
    else:
        u_a, conv_a = branch(wa_ref, cwa_ref, cba_ref, 0)
        u_g, conv_g = branch(wg_ref, cwg_ref, cbg_ref, 1)
        ua_ref[0, 0] = u_a[tm - 2:tm]
        ug_ref[0, 0] = u_g[tm - 2:tm]
    acc_ref[...] += _dot(jax.nn.gelu(conv_g) * conv_a, wd_ref[...])

    @pl.when(f == pl.num_programs(2) - 1)
    def _():
        y_ref[0] = x_ref[0] + gt_ref[0] * _rms(acc_ref[...], gp_ref[...])


def conv_ffn(x, g_pre, sc, sh, gt, g_post, w_up, conv_w, conv_b, w_down, tm, state_rows=None):
    bsz, t, d = x.shape
    r = sc.shape[1]
    rb = 1 if r == 1 else tm
    ck = FF_CHUNK
    nf = D_FF // ck
    per_row = state_rows is not None
    mod = pl.BlockSpec((1, rb, d), (lambda b_, i, f: (b_, 0, 0)) if r == 1 else (lambda b_, i, f: (b_, i, 0)))
    xrow = pl.BlockSpec((1, tm, d), lambda b_, i, f: (b_, i, 0))
    vec = pl.BlockSpec((1, d), lambda b_, i, f: (0, 0))
    col_a = lambda rows: pl.BlockSpec((rows, ck), lambda b_, i, f: (0, f))
    col_g = lambda rows: pl.BlockSpec((rows, ck), lambda b_, i, f: (0, nf + f))
    in_specs = [xrow, vec, mod, mod, mod, vec, col_a(d), col_g(d), col_a(CONV_W), col_g(CONV_W), col_a(1), col_g(1),
                pl.BlockSpec((ck, d), lambda b_, i, f: (f, 0))]
    args = [x, g_pre.reshape(1, d), sc, sh, gt, g_post.reshape(1, d), w_up, w_up, conv_w, conv_w,
            conv_b.reshape(1, 2 * D_FF), conv_b.reshape(1, 2 * D_FF), w_down]
    scratch = [pltpu.VMEM((tm, d), BF16), pltpu.VMEM((tm, d), F32)]
    if per_row:
        urows = tm
        st_a = pl.BlockSpec((1, tm, ck), lambda b_, i, f: (b_, i, f))
        st_g = pl.BlockSpec((1, tm, ck), lambda b_, i, f: (b_, i, nf + f))
        in_specs += [st_a, st_g, st_a, st_g]
        args += [state_rows[0], state_rows[0], state_rows[1], state_rows[1]]
        u_shape = jax.ShapeDtypeStruct((bsz, t, D_FF), F32)
        u_spec = pl.BlockSpec((1, tm, ck), lambda b_, i, f: (b_, i, f))
    else:
        u_shape = jax.ShapeDtypeStruct((bsz, t // tm, CONV_W - 1, D_FF), F32)
        u_spec = pl.BlockSpec((1, 1, CONV_W - 1, ck), lambda b_, i, f: (b_, i, 0, f))
        scratch.append(pltpu.VMEM((nf, 2, 8, ck), F32))
    y, ua, ug = pl.pallas_call(
        functools.partial(_ffn_kernel, per_row_state=per_row),
        out_shape=(jax.ShapeDtypeStruct((bsz, t, d), F32), u_shape, u_shape),
        grid=(bsz, t // tm, nf),
        in_specs=in_specs,
        out_specs=(xrow, u_spec, u_spec),
        scratch_shapes=scratch,
        compiler_params=_cparams(("arbitrary", "arbitrary", "arbitrary")),
        name="ffn",
    )(*args)
    if not per_row:
        ua, ug = ua[:, -1], ug[:, -1]
    return y, jnp.concatenate([ua, ug], axis=-1)


LANE = 128
QUARTERS = 2 * KV_WIDTH // LANE
HEADS_PER_LANE_ROW = LANE // A_HEAD_DIM


def compress_weights(cmp_w1, cmp_pe, cmp_w2):
    eye = jnp.eye(HEADS_PER_LANE_ROW, dtype=F32)
    bd = lambda w: jnp.einsum("gh,...de->...gdhe", eye, w).reshape(w.shape[:-2] + (LANE, LANE))
    w1ab = jnp.concatenate([bd(cmp_w1[:, :CMP_STRIDE]), bd(cmp_w1[:, CMP_STRIDE:])], axis=-1).astype(BF16)
    w1ab = w1ab.reshape(2, CMP_STRIDE // 2, 2 * LANE, 2 * LANE)
    w2 = bd(cmp_w2).astype(BF16)
    w1r = cmp_w1.reshape(2, CMP_BLOCK * A_HEAD_DIM, A_HEAD_DIM)
    pe = cmp_pe.reshape(2, CMP_BLOCK * A_HEAD_DIM, 1)
    return w1ab, w2, w1r, pe


def _compress_chunk(get_x, nrows, w1_ref, w2_ref, w1r_ref, pe_ref, carry_ref, out_ref):
    row = lax.broadcasted_iota(jnp.int32, (nrows, 1), 0)
    for kind in range(2):
        peb = jnp.sum(pe_ref[kind] * w1r_ref[kind], axis=0, keepdims=True)
        peb = jnp.concatenate([peb] * HEADS_PER_LANE_ROW, axis=1)
        for half in range(2):
            acc = jnp.zeros((nrows, 2 * LANE), F32)
            for j in range(CMP_STRIDE // 2):
                x = get_x(2 * kind + half, j).astype(BF16)
                acc = acc + jnp.dot(x, w1_ref[kind, j], preferred_element_type=F32)
            acc_a, acc_b = acc[:, :LANE], acc[:, LANE:]
            a_shift = jnp.where(row == 0, carry_ref[kind, half], pltpu.roll(acc_a, 1, 0))
            carry_ref[kind, half] = acc_a[nrows - 1:nrows]
            hid = jax.nn.gelu(a_shift + acc_b + peb)
            out_ref[0, kind, :, half * LANE:(half + 1) * LANE] = _dot(hid, w2_ref[kind])


def _cmp_prompt_kernel(kv_ref, w1_ref, w2_ref, w1r_ref, pe_ref, out_ref, carry_ref):
    nseg = kv_ref.shape[1] // (CMP_STRIDE * QUARTERS)
    carry_ref[...] = jnp.zeros_like(carry_ref)
    offset = lambda quarter, s: kv_ref[0, pl.ds(QUARTERS * s + quarter, nseg, stride=CMP_STRIDE * QUARTERS), :]
    get_x = lambda quarter, j: jnp.concatenate([offset(quarter, 2 * j), offset(quarter, 2 * j + 1)], axis=1)
    _compress_chunk(get_x, nseg, w1_ref, w2_ref, w1r_ref, pe_ref, carry_ref, out_ref)


def _cmp_weight_specs():
    zero = lambda n: (lambda *_: (0,) * n)
    return [pl.BlockSpec((2, CMP_STRIDE // 2, 2 * LANE, 2 * LANE), zero(4)),
            pl.BlockSpec((2, LANE, LANE), zero(3)),
            pl.BlockSpec((2, CMP_BLOCK * A_HEAD_DIM, A_HEAD_DIM), zero(3)),
            pl.BlockSpec((2, CMP_BLOCK * A_HEAD_DIM, 1), zero(3))]


def compress_prompt(kv_cmp, cw):
    bsz, t, w = kv_cmp.shape
    nseg = t // CMP_STRIDE
    return pl.pallas_call(
        _cmp_prompt_kernel,
        out_shape=jax.ShapeDtypeStruct((bsz, 2, nseg, KV_WIDTH), F32),
        grid=(bsz,),
        in_specs=[pl.BlockSpec((1, t * QUARTERS, LANE), lambda b_: (b_, 0, 0))] + _cmp_weight_specs(),
        out_specs=pl.BlockSpec((1, 2, nseg, KV_WIDTH), lambda b_: (b_, 0, 0, 0)),
        scratch_shapes=[pltpu.VMEM((2, 2, 1, LANE), F32)],
        compiler_params=_cparams(("arbitrary",)),
        name="cmp_prompt",
    )(kv_cmp.reshape(bsz, t * QUARTERS, LANE), *cw)


def feature_major_pool(cache):
    n_pool = cache.shape[0]
    return jnp.transpose(cache, (0, 2, 3, 4, 1)).reshape(n_pool, 2 * KV_WIDTH, PAGE_SIZE)


def _page_copy(pool_ref, dst, sem_ref, pt_ref, step, slot, k, n_chunks, pages):
    b_ = step // n_chunks
    c = step % n_chunks
    pid = pt_ref[b_, c * pages + k]
    return pltpu.make_async_copy(pool_ref.at[pid], dst(slot, k), sem_ref.at[slot])


def _page_pipeline(pool_ref, dst, sem_ref, pt_ref, n_chunks, pages):
    step = pl.program_id(0) * n_chunks + pl.program_id(1)
    total = pl.num_programs(0) * n_chunks
    slot = step % 2

    def start(st, sl):
        for k in range(pages):
            _page_copy(pool_ref, dst, sem_ref, pt_ref, st, sl, k, n_chunks, pages).start()

    @pl.when(step == 0)
    def _():
        start(step, slot)

    @pl.when(step + 1 < total)
    def _():
        start(step + 1, 1 - slot)

    for k in range(pages):
        _page_copy(pool_ref, dst, sem_ref, pt_ref, step, slot, k, n_chunks, pages).wait()
    return slot


def _cmp_sample_kernel(pt_ref, pool_ref, w1_ref, w2_ref, w1r_ref, pe_ref, out_ref,
                       buf_ref, sem_ref, x_ref, carry_ref, *, n_chunks, pages):
    slot = _page_pipeline(pool_ref, lambda sl, k: buf_ref.at[sl, k], sem_ref, pt_ref, n_chunks, pages)

    @pl.when(pl.program_id(1) == 0)
    def _():
        carry_ref[...] = jnp.zeros_like(carry_ref)

    segs = PAGE_SIZE // CMP_STRIDE
    nrows = pages * segs
    half_rows = PAGE_SIZE // 2
    dst = lax.broadcasted_iota(jnp.int32, (PAGE_SIZE, PAGE_SIZE), 0)
    src = lax.broadcasted_iota(jnp.int32, (PAGE_SIZE, PAGE_SIZE), 1)
    wanted = (dst % segs) * CMP_STRIDE + 2 * ((dst % half_rows) // segs) + dst // half_rows
    pick = jnp.where(src == wanted, 1.0, 0.0).astype(BF16)

    def relayout(p, carry):
        for quarter in range(QUARTERS):
            t = buf_ref[slot, p, quarter * LANE:(quarter + 1) * LANE, :]
            y = _dot_nt(pick, t)
            x_ref[quarter, p] = jnp.concatenate([y[:half_rows], y[half_rows:]], axis=1)
        return carry

    lax.fori_loop(0, pages, relayout, 0, unroll=8)

    def get_x(quarter, j):
        return x_ref[quarter, :, j * segs:(j + 1) * segs, :].reshape(nrows, 2 * LANE)

    _compress_chunk(get_x, nrows, w1_ref, w2_ref, w1r_ref, pe_ref, carry_ref, out_ref)


def _page_chunk(n_pages):
    return math.gcd(n_pages, 32)


def compress_paged(pool, page_table, cw):
    dbs, n_pages = page_table.shape
    pages = _page_chunk(n_pages)
    n_chunks = n_pages // pages
    rows = pages * PAGE_SIZE // CMP_STRIDE
    return pl.pallas_call(
        functools.partial(_cmp_sample_kernel, n_chunks=n_chunks, pages=pages),
        out_shape=jax.ShapeDtypeStruct((dbs, 2, n_chunks * rows, KV_WIDTH), F32),
        grid_spec=pltpu.PrefetchScalarGridSpec(
            num_scalar_prefetch=1,
            grid=(dbs, n_chunks),
            in_specs=[pl.BlockSpec(memory_space=pl.ANY)] + _cmp_weight_specs(),
            out_specs=pl.BlockSpec((1, 2, rows, KV_WIDTH), lambda b_, c, pt: (b_, 0, c, 0)),
            scratch_shapes=[pltpu.VMEM((2, pages, 2 * KV_WIDTH, PAGE_SIZE), F32),
                            pltpu.SemaphoreType.DMA((2,)),
                            pltpu.VMEM((QUARTERS, pages, PAGE_SIZE // 2, 2 * LANE), F32),
                            pltpu.VMEM((2, 2, 1, LANE), F32)]),
        compiler_params=_cparams(("arbitrary", "arbitrary")),
        name="cmp_paged",
    )(page_table, pool, *cw)


def _masked_softmax_rows(s, mask):
    s = jnp.where(mask, s, NEG_INF)
    m = jnp.max(s, axis=-1, keepdims=True)
    m = jnp.where(m == NEG_INF, 0.0, m)
    e = jnp.exp(s - m)
    return e / jnp.maximum(jnp.sum(e, axis=-1, keepdims=True), TINY)


def _topk_mask(score, k, axis):
    n = score.shape[axis]
    idx = lax.broadcasted_iota(jnp.int32, score.shape, axis)
    sel = jnp.zeros(score.shape, F32)
    for _ in range(k):
        mx = jnp.max(score, axis=axis, keepdims=True)
        first = jnp.min(jnp.where(score == mx, idx, n), axis=axis, keepdims=True)
        pick = idx == first
        sel = jnp.where(pick, 1.0, sel)
        score = jnp.where(pick, NEG_INF, score)
    return sel


def _flash_tile(carry, s, v, v_transposed=False):
    m_old, l_old, acc = carry
    m_new = jnp.maximum(m_old, jnp.max(s, axis=-1, keepdims=True))
    m_safe = jnp.where(m_new == NEG_INF, 0.0, m_new)
    p = jnp.exp(s - m_safe)
    alpha = jnp.exp(m_old - m_safe)
    pv = _dot_nt(p, v) if v_transposed else _dot(p, v)
    return m_new, alpha * l_old + jnp.sum(p, axis=-1, keepdims=True), alpha * acc + pv


MASKED = -1e30


def _flash_cols(carry, s, v_t):
    m_old, l_old, acc = carry
    m_new = jnp.maximum(m_old, jnp.max(s, axis=0, keepdims=True))
    p = jnp.exp(s - m_new)
    alpha = jnp.exp(m_old - m_new)
    return m_new, alpha * l_old + jnp.sum(p, axis=0, keepdims=True), alpha * acc + _dot(v_t, p)


def _flash_cols_init(cols, dv):
    return jnp.full((1, cols), MASKED, F32), jnp.zeros((1, cols), F32), jnp.zeros((dv, cols), F32)


def _flash_init(rows, dv):
    return jnp.full((rows, 1), NEG_INF, F32), jnp.zeros((rows, 1), F32), jnp.zeros((rows, dv), F32)


def _flash_out(carry):
    _, l, acc = carry
    return acc / jnp.maximum(l, TINY)


KEY_TILE = 512


def _nsa_prompt_kernel(q_ref, small_ref, cmp_ref, ks_ref, vs_ref, kw_ref, vw_ref, cov_ref, exp_ref, o_ref,
                       bias_ref, *, tq, n_slc):
    i = pl.program_id(1)
    q0 = i * tq
    tk = math.gcd(ks_ref.shape[1], KEY_TILE)
    cols = A_GROUP * tq
    q_t = (q_ref[0] * (A_HEAD_DIM ** -0.5)).T.astype(BF16)
    gate_t = jax.nn.sigmoid(small_ref[0]).T
    ncmp = cmp_ref.shape[2]
    qpos = q0 + lax.broadcasted_iota(jnp.int32, (1, tq), 1)
    jcol = lax.broadcasted_iota(jnp.int32, (ncmp, 1), 0)
    cmp_mask = (jcol >= 1) & (jcol * CMP_STRIDE + (CMP_BLOCK - CMP_STRIDE - 1) <= qpos)
    blk = lax.broadcasted_iota(jnp.int32, (n_slc, 1), 0)
    cur = qpos // SLC_BLOCK
    forced = (blk == 0) | (blk == cur) | (blk == cur - 1)
    future = blk * SLC_BLOCK > qpos
    krow = lax.broadcasted_iota(jnp.int32, (tk, 1), 0)
    lanes = lambda a, r: a[:, r * tq:(r + 1) * tq]
    wk = min(WINDOW + tq, kw_ref.shape[1])
    w_off = pl.multiple_of(jnp.maximum(q0 - WINDOW, 0), tq)
    wpos = w_off + lax.broadcasted_iota(jnp.int32, (wk, 1), 0)
    band = (wpos <= qpos) & (wpos > qpos - WINDOW)
    pieces = []

    for g in range(A_KV_HEADS):
        gs = slice(g * A_HEAD_DIM, (g + 1) * A_HEAD_DIM)
        head = lambda r: slice((g * A_GROUP + r) * A_HEAD_DIM, (g * A_GROUP + r + 1) * A_HEAD_DIM)
        qg = jnp.concatenate([q_t[head(r)] for r in range(A_GROUP)], axis=1)

        s_c = _dot(cmp_ref[0, 0, :, gs], qg)
        p_r = []
        for r in range(A_GROUP):
            s_r = jnp.where(cmp_mask, lanes(s_c, r), NEG_INF)
            m = jnp.max(s_r, axis=0, keepdims=True)
            e = jnp.exp(s_r - jnp.where(m == NEG_INF, 0.0, m))
            p_r.append(e / jnp.maximum(jnp.sum(e, axis=0, keepdims=True), TINY))
        o_c = _dot_tn(cmp_ref[0, 1, :, gs], jnp.concatenate(p_r, axis=1))
        imp_t = _dot(cov_ref[...], p_r[0] + p_r[1] + p_r[2] + p_r[3])
        score = jnp.where(future, -BIG, imp_t + jnp.where(forced, BIG, 0.0))
        sel_t = _topk_mask(score, min(SLC_TOPK, n_slc), 0)
        sel_t = jnp.concatenate([sel_t, jnp.zeros((exp_ref.shape[1] - n_slc, tq), F32)], axis=0).astype(BF16)

        def bias_step(j, carry):
            off = pl.multiple_of(j * tk, tk)
            picked = jnp.dot(exp_ref[pl.ds(off, tk), :], sel_t, preferred_element_type=F32) > 0.5
            bias_ref[pl.ds(off, tk), :] = jnp.where(picked & (off + krow <= qpos), 0.0, MASKED)
            return carry

        lax.fori_loop(0, (q0 + tq + tk - 1) // tk, bias_step, 0)

        def slc_step(j, carry):
            off = pl.multiple_of(j * tk, tk)
            s = jnp.dot(ks_ref[0, pl.ds(off, tk), gs], qg, preferred_element_type=F32)
            bias = bias_ref[pl.ds(off, tk), :]
            s = jnp.concatenate([lanes(s, r) + bias for r in range(A_GROUP)], axis=1)
            return _flash_cols(carry, s, vs_ref[0, gs, pl.ds(off, tk)])

        _, l_s, o_s = lax.fori_loop(0, (q0 + tq + tk - 1) // tk, slc_step, _flash_cols_init(cols, A_HEAD_DIM))

        s = jnp.dot(kw_ref[0, pl.ds(w_off, wk), gs], qg, preferred_element_type=F32)
        s = jnp.concatenate([jnp.where(band, lanes(s, r), MASKED) for r in range(A_GROUP)], axis=1)
        _, l_w, o_w = _flash_cols(_flash_cols_init(cols, A_HEAD_DIM), s, vw_ref[0, gs, pl.ds(w_off, wk)])
        o_s = o_s / jnp.maximum(l_s, TINY)
        o_w = o_w / jnp.maximum(l_w, TINY)

        for r in range(A_GROUP):
            row = SMALL_AG + 3 * (g * A_GROUP + r)
            pieces.append(gate_t[row:row + 1] * lanes(o_c, r) + gate_t[row + 1:row + 2] * lanes(o_s, r) +
                          gate_t[row + 2:row + 3] * lanes(o_w, r))
    o_ref[0] = jnp.concatenate(pieces, axis=0).T


def _coverage(n_cmp_rows, n_slc):
    cs = (np.arange(n_cmp_rows) - 1) * CMP_STRIDE
    ss = np.arange(n_slc) * SLC_BLOCK
    lo = np.maximum(cs[:, None], ss[None, :])
    hi = np.minimum(cs[:, None] + CMP_BLOCK, ss[None, :] + SLC_BLOCK)
    cov = np.clip(hi - lo, 0, None) / CMP_BLOCK
    cov[0] = 0.0
    return cov.astype(np.float32)


def _block_expand(n_rows, n_keys):
    return (np.arange(n_rows)[:, None] == (np.arange(n_keys) // SLC_BLOCK)[None, :]).astype(np.float32)


def nsa_prompt(q_rot, z, cmp, kv_slc, kv_win, tq):
    bsz, t, _ = q_rot.shape
    n_slc = t // SLC_BLOCK
    ncmp = cmp.shape[2]
    kk = lambda kv: kv[..., :KV_WIDTH].astype(BF16)
    vt = lambda kv: jnp.swapaxes(kv[..., KV_WIDTH:], 1, 2).astype(BF16)
    cov_t = jnp.asarray(_coverage(ncmp, n_slc).T, BF16)
    n_exp = -(-n_slc // 128) * 128
    expand = jnp.asarray(_block_expand(n_exp, t).T, BF16)
    per_b = lambda shape: pl.BlockSpec((1,) + shape, lambda b_, i: (b_,) + (0,) * len(shape))
    return pl.pallas_call(
        functools.partial(_nsa_prompt_kernel, tq=tq, n_slc=n_slc),
        out_shape=jax.ShapeDtypeStruct((bsz, t, D_MODEL), F32),
        grid=(bsz, t // tq),
        in_specs=[pl.BlockSpec((1, tq, D_MODEL), lambda b_, i: (b_, i, 0)),
                  pl.BlockSpec((1, tq, 128), lambda b_, i: (b_, i, Z_SMALL // 128)),
                  per_b((2, ncmp, KV_WIDTH)),
                  per_b((t, KV_WIDTH)), per_b((KV_WIDTH, t)), per_b((t, KV_WIDTH)), per_b((KV_WIDTH, t)),
                  pl.BlockSpec((n_slc, ncmp), lambda b_, i: (0, 0)),
                  pl.BlockSpec((t, n_exp), lambda b_, i: (0, 0))],
        out_specs=pl.BlockSpec((1, tq, D_MODEL), lambda b_, i: (b_, i, 0)),
        scratch_shapes=[pltpu.VMEM((t, tq), F32)],
        compiler_params=_cparams(("arbitrary", "arbitrary")),
        name="nsa_prompt",
    )(q_rot, z, cmp, kk(kv_slc), vt(kv_slc), kk(kv_win), vt(kv_win), cov_t, expand)


SAMPLE_ROWS = 8
NEW_KEYS = 128


def _nsa_sample_kernel(pt_ref, qbd_ref, gl_ref, cmp_ref, pool_ref, knew_ref, wcache_ref, wnew_ref, cov_ref, exp_ref,
                       o_ref, buf_ref, sem_ref, sel_ref, m_ref, l_ref, acc_ref, oc_ref,
                       *, n_chunks, pages, past, t_real, n_slc):
    c = pl.program_id(1)
    page_window = lambda sl, k: buf_ref.at[sl, :, pl.ds(k * PAGE_SIZE, PAGE_SIZE)]
    slot = _page_pipeline(pool_ref, page_window, sem_ref, pt_ref, n_chunks, pages)
    qbd = qbd_ref[0]
    rows = qbd.shape[0]
    bpc = pages * PAGE_SIZE // SLC_BLOCK
    rq = lax.broadcasted_iota(jnp.int32, (rows, 1), 0) % t_real
    qpos = past + rq

    @pl.when(c == 0)
    def _():
        ncmp = cmp_ref.shape[2]
        nbp = cov_ref.shape[1]
        jrow = lax.broadcasted_iota(jnp.int32, (1, ncmp), 1)
        cmp_mask = (jrow >= 1) & (jrow * CMP_STRIDE + (CMP_BLOCK - CMP_STRIDE - 1) <= qpos)
        p = _masked_softmax_rows(_dot_nt(qbd, cmp_ref[0, 0]), cmp_mask)
        oc_ref[...] = _dot(p, cmp_ref[0, 1])
        ri = lax.broadcasted_iota(jnp.int32, (rows, rows), 0)
        ci = lax.broadcasted_iota(jnp.int32, (rows, rows), 1)
        group_rows = A_GROUP * t_real
        same = (ri // group_rows == ci // group_rows) & (ri % t_real == ci % t_real)
        p_group = _dot(jnp.where(same, 1.0, 0.0), p)
        imp = _dot(p_group, cov_ref[...])
        blk = lax.broadcasted_iota(jnp.int32, (1, nbp), 1)
        cur = qpos // SLC_BLOCK
        forced = (blk == 0) | (blk == cur) | (blk == cur - 1)
        future = blk * SLC_BLOCK > qpos
        score = jnp.where(future, -BIG, imp + jnp.where(forced, BIG, 0.0))
        score = jnp.where(blk < n_slc, score, NEG_INF)
        sel = _topk_mask(score, min(SLC_TOPK, n_slc), 1)
        pad = jnp.zeros((rows, 128 - bpc), F32)
        for cc in range(n_chunks + 1):
            sel_ref[cc] = jnp.concatenate([sel[:, cc * bpc:(cc + 1) * bpc], pad], axis=1).astype(BF16)
        m_ref[...] = jnp.full(m_ref.shape, NEG_INF, F32)
        l_ref[...] = jnp.zeros_like(l_ref)
        acc_ref[...] = jnp.zeros_like(acc_ref)

    k_t = buf_ref[slot, 0:KV_WIDTH, :]
    v_t = buf_ref[slot, KV_WIDTH:2 * KV_WIDTH, :]
    picked = jnp.dot(sel_ref[c], exp_ref[...], preferred_element_type=F32) > 0.5
    s = jnp.where(picked, _dot(qbd, k_t), NEG_INF)
    carry = _flash_tile((m_ref[...], l_ref[...], acc_ref[...]), s, v_t, v_transposed=True)
    m_ref[...], l_ref[...], acc_ref[...] = carry

    @pl.when(c == n_chunks - 1)
    def _():
        zpad = jnp.zeros((NEW_KEYS - SAMPLE_ROWS, 2 * KV_WIDTH), F32)
        kcol = lax.broadcasted_iota(jnp.int32, (1, NEW_KEYS), 1)
        new_mask = (kcol <= rq) & (kcol < t_real)
        knew = jnp.concatenate([knew_ref[0], zpad], axis=0)
        last_picked = sel_ref[n_chunks][:, 0:1].astype(F32) > 0.5
        s_new = jnp.where(new_mask & last_picked, _dot_nt(qbd, knew[:, :KV_WIDTH]), NEG_INF)
        o_s = _flash_out(_flash_tile((m_ref[...], l_ref[...], acc_ref[...]), s_new, knew[:, KV_WIDTH:]))

        wc = wcache_ref[0]
        wb = wc.shape[0]
        wcol = lax.broadcasted_iota(jnp.int32, (1, wb), 1)
        s_w = jnp.where(wcol > rq + (wb - WINDOW), _dot_nt(qbd, wc[:, :KV_WIDTH]), NEG_INF)
        cw = _flash_tile(_flash_init(rows, KV_WIDTH), s_w, wc[:, KV_WIDTH:])
        wnew = jnp.concatenate([wnew_ref[0], zpad], axis=0)
        s_wn = jnp.where(new_mask, _dot_nt(qbd, wnew[:, :KV_WIDTH]), NEG_INF)
        o_w = _flash_out(_flash_tile(cw, s_wn, wnew[:, KV_WIDTH:]))

        gate = jax.nn.sigmoid(gl_ref[0])
        o = gate[:, 0:1] * oc_ref[...] + gate[:, 1:2] * o_s + gate[:, 2:3] * o_w
        lane_g = lax.broadcasted_iota(jnp.int32, (1, KV_WIDTH), 1) // A_HEAD_DIM
        row_g = lax.broadcasted_iota(jnp.int32, (rows, 1), 0) // (A_GROUP * t_real)
        o = jnp.where(lane_g == row_g, o, 0.0)
        o_ref[0] = sum(o[:, g * A_HEAD_DIM:(g + 1) * A_HEAD_DIM] for g in range(A_KV_HEADS))


def nsa_sample(q_rot, z, cmp, pool, page_table, k_new, win_cache, w_new, past, t_real):
    dbs = q_rot.shape[0]
    n_pages = page_table.shape[1]
    pages = _page_chunk(n_pages)
    n_chunks = n_pages // pages
    bpc = pages * PAGE_SIZE // SLC_BLOCK
    ncmp = cmp.shape[2]
    n_slc = -(-(past + t_real) // SLC_BLOCK)
    assert n_slc == n_chunks * bpc + 1 and bpc <= 128
    nbp = -(-((n_chunks + 1) * bpc) // 128) * 128
    rows = A_HEADS * t_real
    q5 = q_rot[:, :t_real].reshape(dbs, t_real, A_KV_HEADS, A_GROUP, A_HEAD_DIM) * (A_HEAD_DIM ** -0.5)
    qbd = jnp.einsum("bqgrd,gh->bgrqhd", q5, jnp.eye(A_KV_HEADS, dtype=F32)).reshape(dbs, rows, KV_WIDTH).astype(BF16)
    gl = z[:, :t_real, Z_SMALL + SMALL_AG:Z_SMALL + SMALL_AG + 3 * A_HEADS].reshape(dbs, t_real, A_HEADS, 3)
    gl = jnp.swapaxes(gl, 1, 2).reshape(dbs, rows, 3)
    cov = np.zeros((ncmp, nbp), np.float32)
    cov[:, :n_slc] = _coverage(ncmp, n_slc)
    expand = jnp.asarray(_block_expand(128, pages * PAGE_SIZE), BF16)
    per_b = lambda shape: pl.BlockSpec((1,) + shape, lambda b_, c, pt: (b_,) + (0,) * len(shape))
    const = lambda shape: pl.BlockSpec(shape, lambda b_, c, pt: (0,) * len(shape))
    wb = win_cache.shape[1]
    out = pl.pallas_call(
        functools.partial(_nsa_sample_kernel, n_chunks=n_chunks, pages=pages, past=past, t_real=t_real, n_slc=n_slc),
        out_shape=jax.ShapeDtypeStruct((dbs, rows, A_HEAD_DIM), F32),
        grid_spec=pltpu.PrefetchScalarGridSpec(
            num_scalar_prefetch=1,
            grid=(dbs, n_chunks),
            in_specs=[per_b((rows, KV_WIDTH)), per_b((rows, 3)), per_b((2, ncmp, KV_WIDTH)),
                      pl.BlockSpec(memory_space=pl.ANY),
                      per_b((SAMPLE_ROWS, 2 * KV_WIDTH)), per_b((wb, 2 * KV_WIDTH)), per_b((SAMPLE_ROWS, 2 * KV_WIDTH)),
                      const((ncmp, nbp)), const((128, pages * PAGE_SIZE))],
            out_specs=per_b((rows, A_HEAD_DIM)),
            scratch_shapes=[pltpu.VMEM((2, 2 * KV_WIDTH, pages * PAGE_SIZE), F32),
                            pltpu.SemaphoreType.DMA((2,)),
                            pltpu.VMEM((n_chunks + 1, rows, 128), BF16),
                            pltpu.VMEM((rows, 1), F32), pltpu.VMEM((rows, 1), F32),
                            pltpu.VMEM((rows, KV_WIDTH), F32), pltpu.VMEM((rows, KV_WIDTH), F32)]),
        compiler_params=_cparams(("arbitrary", "arbitrary")),
        name="nsa_sample",
    )(page_table, qbd, gl, cmp, pool, k_new, win_cache, w_new, jnp.asarray(cov, BF16), expand)
    out = jnp.swapaxes(out.reshape(dbs, A_HEADS, t_real, A_HEAD_DIM), 1, 2).reshape(dbs, t_real, D_MODEL)
    return jnp.pad(out, ((0, 0), (0, SAMPLE_ROWS - t_real), (0, 0)))


def _kv_rows(a, bsz, t):
    return a.reshape(bsz, t, 2, A_KV_HEADS, A_HEAD_DIM)


def kernel(x_prompt, x_sample, cache_cmp_kv, cache_slc_kv, cache_win_kv, state_C, state_n, state_m, state_conv,
           page_table, c_prompt, c_sample, w_ada, b_ada, g_pre_mix, g_post_mix, g_pre_ffn, g_post_ffn, w_in, b_in,
           m_norm_w, cmp_w1, cmp_pe, cmp_w2, w_branch_m, w_branch_a, w_out, w_up, conv_w, conv_b, w_down):
    depth = w_ada.shape[0]
    bsz, t, d = x_prompt.shape
    dbs, ts, _ = x_sample.shape
    n_pages = page_table.shape[1]
    past = n_pages * PAGE_SIZE
    assert ts <= SAMPLE_ROWS and (past + ts) // CMP_STRIDE == past // CMP_STRIDE and past >= WINDOW
    srows = dbs * SAMPLE_ROWS
    xp = x_prompt.astype(F32)
    xs = jnp.pad(x_sample.astype(F32), ((0, 0), (0, SAMPLE_ROWS - ts), (0, 0))).reshape(1, srows, d)
    c_all = jnp.concatenate([c_prompt, c_sample], axis=0).astype(F32)
    c_all = jnp.pad(c_all, ((0, (-c_all.shape[0]) % 8), (0, 0)))
    tab_p = rope_tables(jnp.arange(t, dtype=jnp.int32))
    tab_s = rope_tables(jnp.tile(past + jnp.arange(SAMPLE_ROWS, dtype=jnp.int32), dbs))
    lchunk = math.gcd(t, 256)
    tm_p = math.gcd(t, 512)
    p_states, s_states = [], []
    for l in range(depth):
        mod = ada_modulation(c_all, w_ada[l], b_ada[l])
        mod_p = [m[:, None, :] for m in jnp.split(mod[:bsz], 6, axis=-1)]
        mod_s = [jnp.repeat(m, SAMPLE_ROWS, axis=0)[None] for m in jnp.split(mod[bsz:bsz + dbs], 6, axis=-1)]
        w_r, b_r = regroup_in_weights(w_in[l], b_in[l])
        cw = compress_weights(cmp_w1[l], cmp_pe[l], cmp_w2[l])
        wm, wa, wo = w_branch_m[l].astype(BF16), w_branch_a[l].astype(BF16), w_out[l].astype(BF16)
        wu, wd = w_up[l].astype(BF16), w_down[l].astype(BF16)

        sh_m, sc_m, gt_m, sh_f, sc_f, gt_f = mod_p
        z = in_projection(xp, g_pre_mix[l], sc_m, sh_m, w_r, b_r, tm=math.gcd(t, 1024))
        q_rot, kv_cmp, kv_slc, kv_win = rope_split(z, tab_p, tm=tm_p)
        hm, p_c, p_n, p_m = mlstm(z, m_norm_w[l], jnp.zeros((bsz, M_HEADS, M_HEAD_DIM, M_HEAD_DIM), F32),
                                  jnp.zeros((bsz, M_HEADS, M_HEAD_DIM), F32), jnp.zeros((bsz, M_HEADS), F32),
                                  lb=lchunk, lp=lchunk, t_real=lchunk)
        ha = nsa_prompt(q_rot, z, compress_prompt(kv_cmp, cw), kv_slc, kv_win, tq=256)
        xp = mix_out(xp, hm, ha, z, gt_m, g_post_mix[l], wm, wa, wo, tm=tm_p)
        xp, p_conv = conv_ffn(xp, g_pre_ffn[l], sc_f, sh_f, gt_f, g_post_ffn[l], wu, conv_w[l], conv_b[l], wd, tm=tm_p)
        wkeep = min(WINDOW, t)
        p_states.append((_kv_rows(kv_cmp, bsz, t), _kv_rows(kv_slc, bsz, t), _kv_rows(kv_win[:, t - wkeep:], bsz, wkeep),
                         p_c, p_n, p_m, p_conv))

        sh_m, sc_m, gt_m, sh_f, sc_f, gt_f = mod_s
        z = in_projection(xs, g_pre_mix[l], sc_m, sh_m, w_r, b_r, tm=srows)
        q_rot, kv_cmp, kv_slc, kv_win = rope_split(z, tab_s, tm=srows)
        z3 = z.reshape(dbs, SAMPLE_ROWS, Z_WIDTH)
        hm, s_c, s_n, s_m = mlstm(z3, m_norm_w[l], state_C[l].astype(F32), state_n[l].astype(F32),
                                  state_m[l].astype(F32), lb=SAMPLE_ROWS, lp=128, t_real=ts)
        cmp_s = compress_paged(feature_major_pool(cache_cmp_kv[l].astype(F32)), page_table, cw)
        new3 = lambda a: a.reshape(dbs, SAMPLE_ROWS, 2 * KV_WIDTH)
        win_cache = cache_win_kv[l].astype(F32).reshape(dbs, -1, 2 * KV_WIDTH)
        ha = nsa_sample(q_rot.reshape(dbs, SAMPLE_ROWS, d), z3, cmp_s, feature_major_pool(cache_slc_kv[l].astype(F32)),
                        page_table, new3(kv_slc), win_cache, new3(kv_win), past, ts)
        xs = mix_out(xs, hm.reshape(1, srows, d), ha.reshape(1, srows, d), z, gt_m, g_post_mix[l], wm, wa, wo, tm=srows)
        st = state_conv[l].astype(F32)
        s2 = jnp.pad(st, ((0, 0), (0, SAMPLE_ROWS - (CONV_W - 1)), (0, 0))).reshape(1, srows, 2 * D_FF)
        s1 = jnp.pad(st[:, 1:], ((0, 0), (0, SAMPLE_ROWS - 1), (0, 0))).reshape(1, srows, 2 * D_FF)
        xs, u = conv_ffn(xs, g_pre_ffn[l], sc_f, sh_f, gt_f, g_post_ffn[l], wu, conv_w[l], conv_b[l], wd, tm=srows,
                         state_rows=(s1, s2))
        wb = win_cache.shape[1]
        s_win = jnp.concatenate([win_cache, new3(kv_win)[:, :ts]], axis=1)[:, ts:]
        s_conv = jnp.concatenate([st, u.reshape(dbs, SAMPLE_ROWS, 2 * D_FF)[:, :ts]], axis=1)[:, ts:]
        s_states.append((_kv_rows(new3(kv_cmp)[:, :ts], dbs, ts), _kv_rows(new3(kv_slc)[:, :ts], dbs, ts),
                         _kv_rows(s_win, dbs, wb), s_c, s_n, s_m, s_conv))

    stack = lambda states: [jnp.stack([s[i] for s in states]) for i in range(7)]
    y_sample = xs.reshape(dbs, SAMPLE_ROWS, d)[:, :ts]
    return (xp, y_sample, *stack(p_states), *stack(s_states))
```

```python
import functools
import math

import numpy as np
import jax
import jax.numpy as jnp
from jax import lax
from jax.experimental import pallas as pl
from jax.experimental.pallas import tpu as pltpu

F32 = jnp.float32
BF16 = jnp.bfloat16

D_MODEL = 1024
M_HEADS = 4
M_HEAD_DIM = 256
A_HEADS = 16
A_HEAD_DIM = 64
A_KV_HEADS = 4
A_GROUP = 4
KV_WIDTH = A_KV_HEADS * A_HEAD_DIM
CMP_STRIDE = 16
CMP_BLOCK = 32
SLC_BLOCK = 64
SLC_TOPK = 16
WINDOW = 512
ROPE_THETA = 500000.0
ROPE_DIM = 16
BIG = 1e6
D_FF = 2816
CONV_W = 3
NORM_EPS = 1e-6
PAGE_SIZE = 128
NEG_INF = float("-inf")
TINY = float(np.finfo(np.float32).tiny)

Z_MQ, Z_MK, Z_MV, Z_MO, Z_AQ, Z_GA, Z_GB, Z_AKV, Z_SMALL = 0, 1024, 2048, 3072, 4096, 5120, 6144, 7168, 8704
Z_WIDTH = 9216
SMALL_MI, SMALL_MF, SMALL_AG = 0, 4, 8

VMEM_LIMIT = 48 * 1024 * 1024


def _cparams(sem):
    return pltpu.CompilerParams(dimension_semantics=sem, vmem_limit_bytes=VMEM_LIMIT)


def _dot(a, b):
    return jnp.dot(a.astype(BF16), b.astype(BF16), preferred_element_type=F32)


def _dot_nt(a, b):
    return lax.dot_general(a.astype(BF16), b.astype(BF16), (((1,), (1,)), ((), ())), preferred_element_type=F32)


def _dot_tn(a, b):
    return lax.dot_general(a.astype(BF16), b.astype(BF16), (((0,), (0,)), ((), ())), preferred_element_type=F32)


def _split3(x):
    x1 = x.astype(BF16)
    r1 = x - x1.astype(F32)
    x2 = r1.astype(BF16)
    x3 = (r1 - x2.astype(F32)).astype(BF16)
    return x1, x2, x3


def _rms(x, g):
    return x * lax.rsqrt(jnp.mean(x * x, axis=-1, keepdims=True) + NORM_EPS) * g


def _ada_kernel(c_ref, w_ref, b_ref, o_ref):
    c = c_ref[...]
    o_ref[...] = _dot(c * jax.nn.sigmoid(c), w_ref[...]) + b_ref[...]


def ada_modulation(c, w_ada, b_ada):
    rows, d = c.shape
    n = w_ada.shape[1]
    tn = 512
    return pl.pallas_call(
        _ada_kernel,
        out_shape=jax.ShapeDtypeStruct((rows, n), F32),
        grid=(n // tn,),
        in_specs=[pl.BlockSpec((rows, d), lambda j: (0, 0)),
                  pl.BlockSpec((d, tn), lambda j: (0, j)),
                  pl.BlockSpec((1, tn), lambda j: (0, j))],
        out_specs=pl.BlockSpec((rows, tn), lambda j: (0, j)),
        compiler_params=_cparams(("arbitrary",)),
        name="ada",
    )(c, w_ada, b_ada.reshape(1, n))


def _inproj_kernel(x_ref, g_ref, sc_ref, sh_ref, w_ref, b_ref, o_ref, h_ref):
    @pl.when(pl.program_id(2) == 0)
    def _():
        h = _rms(x_ref[0], g_ref[...]) * (1.0 + sc_ref[0]) + sh_ref[0]
        h_ref[...] = h.astype(BF16)

    o_ref[0] = jnp.dot(h_ref[...], w_ref[...], preferred_element_type=F32) + b_ref[...]


def in_projection(x, g, sc, sh, w_bf16, b, tm):
    bsz, t, d = x.shape
    r = sc.shape[1]
    rb = 1 if r == 1 else tm
    tn = 1024
    mod_spec = pl.BlockSpec((1, rb, d), (lambda b_, i, j: (b_, 0, 0)) if r == 1 else (lambda b_, i, j: (b_, i, 0)))
    return pl.pallas_call(
        _inproj_kernel,
        out_shape=jax.ShapeDtypeStruct((bsz, t, Z_WIDTH), F32),
        grid=(bsz, t // tm, Z_WIDTH // tn),
        in_specs=[pl.BlockSpec((1, tm, d), lambda b_, i, j: (b_, i, 0)),
                  pl.BlockSpec((1, d), lambda b_, i, j: (0, 0)),
                  mod_spec, mod_spec,
                  pl.BlockSpec((d, tn), lambda b_, i, j: (0, j)),
                  pl.BlockSpec((1, tn), lambda b_, i, j: (0, j))],
        out_specs=pl.BlockSpec((1, tm, tn), lambda b_, i, j: (b_, i, j)),
        scratch_shapes=[pltpu.VMEM((tm, d), BF16)],
        compiler_params=_cparams(("arbitrary", "arbitrary", "arbitrary")),
        name="inproj",
    )(x, g.reshape(1, d), sc, sh, w_bf16, b.reshape(1, Z_WIDTH))


def regroup_in_weights(w_in, b_in):
    mw = M_HEADS * M_HEAD_DIM
    o_mi = 4 * mw
    o_aq = o_mi + 2 * M_HEADS
    o_akv = o_aq + A_HEADS * A_HEAD_DIM
    o_ag = o_akv + 6 * KV_WIDTH
    o_ga = o_ag + 3 * A_HEADS
    o_gb = o_ga + D_MODEL

    def regroup(a):
        lead = a.shape[:-1]
        parts = [a[..., :o_mi], a[..., o_aq:o_akv], a[..., o_ga:o_gb], a[..., o_gb:o_gb + D_MODEL],
                 a[..., o_akv:o_ag], a[..., o_mi:o_aq], a[..., o_ag:o_ga],
                 jnp.zeros(lead + (128 - 2 * M_HEADS - 3 * A_HEADS,), a.dtype),
                 jnp.zeros(lead + (Z_WIDTH - Z_SMALL - 128,), a.dtype)]
        return jnp.concatenate(parts, axis=-1)

    return regroup(w_in).astype(BF16), regroup(b_in)


def rope_tables(pos):
    half = ROPE_DIM // 2
    inv_freq = ROPE_THETA ** (-jnp.arange(half, dtype=F32) / half)
    ang = pos.astype(F32)[:, None] * inv_freq
    cos, sin = jnp.cos(ang), jnp.sin(ang)
    rows = pos.shape[0]
    zeros = jnp.zeros((rows, half), F32)
    rest1 = jnp.ones((rows, A_HEAD_DIM - ROPE_DIM), F32)
    rest0 = jnp.zeros((rows, A_HEAD_DIM - ROPE_DIM), F32)
    c = jnp.concatenate([cos, cos, rest1], axis=1)
    sa = jnp.concatenate([zeros, sin, rest0], axis=1)
    sb = jnp.concatenate([-sin, zeros, rest0], axis=1)
    return tuple(jnp.concatenate([a, a], axis=1) for a in (c, sa, sb))


def _rope_apply(x, c, sa, sb):
    w = x.shape[1]
    n = w // 128
    ct, sat, sbt = (jnp.concatenate([a] * n, axis=1) for a in (c, sa, sb))
    return x * ct + pltpu.roll(x, ROPE_DIM // 2, 1) * sat + pltpu.roll(x, w - ROPE_DIM // 2, 1) * sbt


def _rope_kernel(q_ref, c_ref, s_ref, w_ref, cos_ref, sa_ref, sb_ref, qo_ref, co_ref, so_ref, wo_ref):
    c, sa, sb = cos_ref[...], sa_ref[...], sb_ref[...]
    qo_ref[0] = _rope_apply(q_ref[0], c, sa, sb)
    for src, dst in ((c_ref, co_ref), (s_ref, so_ref), (w_ref, wo_ref)):
        kv = src[0]
        dst[0] = jnp.concatenate([_rope_apply(kv[:, :KV_WIDTH], c, sa, sb), kv[:, KV_WIDTH:]], axis=1)


def rope_split(z, tables, tm):
    bsz, t, _ = z.shape
    nt = t // tm
    kvw = 2 * KV_WIDTH
    tab_spec = pl.BlockSpec((tm, 128), lambda b_, i: (i, 0))
    return pl.pallas_call(
        _rope_kernel,
        out_shape=(jax.ShapeDtypeStruct((bsz, t, D_MODEL), F32),) + (jax.ShapeDtypeStruct((bsz, t, kvw), F32),) * 3,
        grid=(bsz, nt),
        in_specs=[pl.BlockSpec((1, tm, D_MODEL), lambda b_, i: (b_, i, Z_AQ // D_MODEL)),
                  pl.BlockSpec((1, tm, kvw), lambda b_, i: (b_, i, Z_AKV // kvw)),
                  pl.BlockSpec((1, tm, kvw), lambda b_, i: (b_, i, Z_AKV // kvw + 1)),
                  pl.BlockSpec((1, tm, kvw), lambda b_, i: (b_, i, Z_AKV // kvw + 2)),
                  tab_spec, tab_spec, tab_spec],
        out_specs=(pl.BlockSpec((1, tm, D_MODEL), lambda b_, i: (b_, i, 0)),) +
                  (pl.BlockSpec((1, tm, kvw), lambda b_, i: (b_, i, 0)),) * 3,
        compiler_params=_cparams(("arbitrary", "arbitrary")),
        name="rope",
    )(z, z, z, z, *tables)


def _mlstm_kernel(q_ref, k_ref, v_ref, o_ref, s_ref, nw_ref, c0_ref, n0_ref, m0_ref,
                  h_ref, c_ref, n_ref, m_ref, *, lb, lp, t_real):
    @pl.when(pl.program_id(1) == 0)
    def _():
        c_ref[...] = c0_ref[...]
        n_ref[...] = n0_ref[...]
        m_ref[...] = m0_ref[...]

    def pad(a):
        if lb == lp:
            return a
        return jnp.concatenate([a, jnp.zeros((lp - lb, a.shape[1]), a.dtype)], axis=0)

    small = pad(s_ref[0])
    small_t = small.T
    row_c = lax.broadcasted_iota(jnp.int32, (lp, 1), 0)
    row_r = lax.broadcasted_iota(jnp.int32, (1, lp), 1)
    li_col_all = jnp.where(row_c < t_real, small, NEG_INF)
    lf_col_all = jnp.where(row_c < t_real, jax.nn.log_sigmoid(small), 0.0)
    li_row_all = jnp.where(row_r < t_real, small_t[0:8], NEG_INF)
    lf_row_all = jnp.where(row_r < t_real, jax.nn.log_sigmoid(small_t[0:8]), 0.0)
    rr = lax.broadcasted_iota(jnp.int32, (lp, lp), 0)
    cc = lax.broadcasted_iota(jnp.int32, (lp, lp), 1)
    causal = cc <= rr
    tril = jnp.where(causal, 1.0, 0.0).astype(BF16)
    triu = jnp.where(rr <= cc, 1.0, 0.0).astype(BF16)
    b_col_all = sum(jnp.dot(tril, p, preferred_element_type=F32) for p in _split3(lf_col_all))
    b_row_all = sum(jnp.dot(p, triu, preferred_element_type=F32) for p in _split3(lf_row_all))

    q_all, k_all, v_all, o_all = pad(q_ref[0]), pad(k_ref[0]), pad(v_ref[0]), pad(o_ref[0])
    nw = nw_ref[...]
    for h in range(M_HEADS):
        hs = slice(h * M_HEAD_DIM, (h + 1) * M_HEAD_DIM)
        qf = q_all[:, hs]
        kf = k_all[:, hs] * (M_HEAD_DIM ** -0.5)
        vf = v_all[:, hs]
        li_row = li_row_all[SMALL_MI + h:SMALL_MI + h + 1, :]
        b_row = b_row_all[SMALL_MF + h:SMALL_MF + h + 1, :]
        li_col = li_col_all[:, SMALL_MI + h:SMALL_MI + h + 1]
        b_col = b_col_all[:, SMALL_MF + h:SMALL_MF + h + 1]
        m_prev = m_ref[0, h]
        c_prev = c_ref[0, h]
        n_prev = n_ref[0, h]

        dlog = jnp.where(causal, b_col - b_row + li_row, NEG_INF)
        inter = m_prev + b_col
        mt = jnp.maximum(inter, jnp.max(dlog, axis=1, keepdims=True))
        a = jnp.exp(inter - mt)
        s = _dot_nt(qf, kf) * jnp.exp(dlog - mt)
        num = a * _dot_nt(qf, c_prev) + _dot(s, vf)
        den = a * jnp.sum(qf * n_prev, axis=1, keepdims=True) + jnp.sum(s, axis=1, keepdims=True)
        hh = num / jnp.maximum(jnp.abs(den), jnp.exp(-mt))
        mu = jnp.mean(hh, axis=1, keepdims=True)
        var = jnp.mean(jnp.square(hh - mu), axis=1, keepdims=True)
        out = (hh - mu) * lax.rsqrt(var + NORM_EPS) * nw[:, hs] * jax.nn.sigmoid(o_all[:, hs])
        h_ref[0, :, hs] = out[:lb]

        bl = b_row[:, lp - 1:lp]
        wlog = bl - b_col + li_col
        m_new = jnp.maximum(m_prev + bl, jnp.max(wlog, axis=0, keepdims=True))
        w = jnp.exp(wlog - m_new)
        decay = jnp.exp(m_prev + bl - m_new)
        c_ref[0, h] = decay * c_prev + _dot_tn(vf * w, kf)
        n_ref[0, h] = decay * n_prev + jnp.sum(w * kf, axis=0, keepdims=True)
        m_ref[0, h] = m_new


def mlstm(z, norm_w, c0, n0, m0, lb, lp, t_real):
    bsz, t, _ = z.shape
    nc = t // lb
    mw = M_HEADS * M_HEAD_DIM
    zspec = lambda col: pl.BlockSpec((1, lb, mw), lambda b_, c: (b_, c, col // mw))
    cst = lambda shape: pl.BlockSpec((1,) + shape, lambda b_, c: (b_,) + (0,) * len(shape))
    h, c, n, m = pl.pallas_call(
        functools.partial(_mlstm_kernel, lb=lb, lp=lp, t_real=t_real),
        out_shape=(jax.ShapeDtypeStruct((bsz, t, mw), F32),
                   jax.ShapeDtypeStruct((bsz, M_HEADS, M_HEAD_DIM, M_HEAD_DIM), F32),
                   jax.ShapeDtypeStruct((bsz, M_HEADS, 1, M_HEAD_DIM), F32),
                   jax.ShapeDtypeStruct((bsz, M_HEADS, 1, 1), F32)),
        grid=(bsz, nc),
        in_specs=[zspec(Z_MQ), zspec(Z_MK), zspec(Z_MV), zspec(Z_MO),
                  pl.BlockSpec((1, lb, 128), lambda b_, c: (b_, c, Z_SMALL // 128)),
                  pl.BlockSpec((1, mw), lambda b_, c: (0, 0)),
                  cst((M_HEADS, M_HEAD_DIM, M_HEAD_DIM)), cst((M_HEADS, 1, M_HEAD_DIM)), cst((M_HEADS, 1, 1))],
        out_specs=(pl.BlockSpec((1, lb, mw), lambda b_, c: (b_, c, 0)),
                   cst((M_HEADS, M_HEAD_DIM, M_HEAD_DIM)), cst((M_HEADS, 1, M_HEAD_DIM)), cst((M_HEADS, 1, 1))),
        compiler_params=_cparams(("arbitrary", "arbitrary")),
        name="mlstm",
    )(z, z, z, z, z, norm_w.reshape(1, mw), c0, n0.reshape(bsz, M_HEADS, 1, M_HEAD_DIM),
      m0.reshape(bsz, M_HEADS, 1, 1))
    return h, c, n.reshape(bsz, M_HEADS, M_HEAD_DIM), m.reshape(bsz, M_HEADS)


def _mix_kernel(x_ref, hm_ref, ha_ref, ga_ref, gb_ref, gt_ref, g_ref, wm_ref, wa_ref, wo_ref, o_ref):
    mixed = (jax.nn.sigmoid(ga_ref[0]) * _dot(hm_ref[0], wm_ref[...]) +
             jax.nn.sigmoid(gb_ref[0]) * _dot(ha_ref[0], wa_ref[...]))
    o_ref[0] = x_ref[0] + gt_ref[0] * _rms(_dot(mixed, wo_ref[...]), g_ref[...])


def mix_out(x, hm, ha, z, gt, g_post, wm, wa, wo, tm):
    bsz, t, d = x.shape
    r = gt.shape[1]
    rb = 1 if r == 1 else tm
    row = lambda col=0: pl.BlockSpec((1, tm, d), lambda b_, i: (b_, i, col // d))
    mod = pl.BlockSpec((1, rb, d), (lambda b_, i: (b_, 0, 0)) if r == 1 else (lambda b_, i: (b_, i, 0)))
    wsp = pl.BlockSpec((d, d), lambda b_, i: (0, 0))
    return pl.pallas_call(
        _mix_kernel,
        out_shape=jax.ShapeDtypeStruct((bsz, t, d), F32),
        grid=(bsz, t // tm),
        in_specs=[row(), row(), row(), row(Z_GA), row(Z_GB), mod,
                  pl.BlockSpec((1, d), lambda b_, i: (0, 0)), wsp, wsp, wsp],
        out_specs=row(),
        compiler_params=_cparams(("arbitrary", "arbitrary")),
        name="mix",
    )(x, hm, ha, z, z, gt, g_post.reshape(1, d), wm, wa, wo)


FF_CHUNK = 256


def _ffn_kernel(*refs, per_row_state):
    if per_row_state:
        (x_ref, g_ref, sc_ref, sh_ref, gt_ref, gp_ref, wa_ref, wg_ref, cwa_ref, cwg_ref, cba_ref, cbg_ref, wd_ref,
         s1a_ref, s1g_ref, s2a_ref, s2g_ref, y_ref, ua_ref, ug_ref, h_ref, acc_ref) = refs
    else:
        (x_ref, g_ref, sc_ref, sh_ref, gt_ref, gp_ref, wa_ref, wg_ref, cwa_ref, cwg_ref, cba_ref, cbg_ref, wd_ref,
         y_ref, ua_ref, ug_ref, h_ref, acc_ref, carry_ref) = refs
    i = pl.program_id(1)
    f = pl.program_id(2)
    tm = x_ref.shape[1]

    @pl.when(f == 0)
    def _():
        h = _rms(x_ref[0], g_ref[...]) * (1.0 + sc_ref[0]) + sh_ref[0]
        h_ref[...] = h.astype(BF16)
        acc_ref[...] = jnp.zeros_like(acc_ref)

    if not per_row_state:
        @pl.when(i == 0)
        def _():
            carry_ref[f] = jnp.zeros(carry_ref.shape[1:], F32)

    row = lax.broadcasted_iota(jnp.int32, (tm, 1), 0)

    def branch(w_ref, cw_ref, cb_ref, part, s1_ref=None, s2_ref=None):
        u = jnp.dot(h_ref[...], w_ref[...], preferred_element_type=F32)
        r1 = pltpu.roll(u, 1, 0)
        r2 = pltpu.roll(u, 2, 0)
        if per_row_state:
            t = row % 8
            u1 = jnp.where(t < 1, s1_ref[0], r1)
            u2 = jnp.where(t < 2, s2_ref[0], r2)
        else:
            prev = carry_ref[f, part]
            top = row[0:8]
            u1 = jnp.concatenate([jnp.where(top < 1, prev[1:2], r1[0:8]), r1[8:]], axis=0)
            u2 = jnp.concatenate([jnp.where(top < 1, prev[0:1], jnp.where(top < 2, prev[1:2], r2[0:8])), r2[8:]],
                                 axis=0)
            carry_ref[f, part, 0:2] = u[tm - 2:tm]
        cw = cw_ref[...]
        return u, cb_ref[...] + cw[0:1] * u2 + cw[1:2] * u1 + cw[2:3] * u

    if per_row_state:
        u_a, conv_a = branch(wa_ref, cwa_ref, cba_ref, 0, s1a_ref, s2a_ref)
        u_g, conv_g = branch(wg_ref, cwg_ref, cbg_ref, 1, s1g_ref, s2g_ref)
        ua_ref[0] = u_a
        ug_ref[0] = u_g
    else:
        u_a, conv_a = branch(wa_ref, cwa_ref, cba_ref, 0)
        u_g, conv_g = branch(wg_ref, cwg_ref, cbg_ref, 1)
        ua_ref[0, 0] = u_a[tm - 2:tm]
        ug_ref[0, 0] = u_g[tm - 2:tm]
    acc_ref[...] += _dot(jax.nn.gelu(conv_g) * conv_a, wd_ref[...])

    @pl.when(f == pl.num_programs(2) - 1)
    def _():
        y_ref[0] = x_ref[0] + gt_ref[0] * _rms(acc_ref[...], gp_ref[...])


def conv_ffn(x, g_pre, sc, sh, gt, g_post, w_up, conv_w, conv_b, w_down, tm, state_rows=None):
    bsz, t, d = x.shape
    r = sc.shape[1]
    rb = 1 if r == 1 else tm
    ck = FF_CHUNK
    nf = D_FF // ck
    per_row = state_rows is not None
    mod = pl.BlockSpec((1, rb, d), (lambda b_, i, f: (b_, 0, 0)) if r == 1 else (lambda b_, i, f: (b_, i, 0)))
    xrow = pl.BlockSpec((1, tm, d), lambda b_, i, f: (b_, i, 0))
    vec = pl.BlockSpec((1, d), lambda b_, i, f: (0, 0))
    col_a = lambda rows: pl.BlockSpec((rows, ck), lambda b_, i, f: (0, f))
    col_g = lambda rows: pl.BlockSpec((rows, ck), lambda b_, i, f: (0, nf + f))
    in_specs = [xrow, vec, mod, mod, mod, vec, col_a(d), col_g(d), col_a(CONV_W), col_g(CONV_W), col_a(1), col_g(1),
                pl.BlockSpec((ck, d), lambda b_, i, f: (f, 0))]
    args = [x, g_pre.reshape(1, d), sc, sh, gt, g_post.reshape(1, d), w_up, w_up, conv_w, conv_w,
            conv_b.reshape(1, 2 * D_FF), conv_b.reshape(1, 2 * D_FF), w_down]
    scratch = [pltpu.VMEM((tm, d), BF16), pltpu.VMEM((tm, d), F32)]
    if per_row:
        urows = tm
        st_a = pl.BlockSpec((1, tm, ck), lambda b_, i, f: (b_, i, f))
        st_g = pl.BlockSpec((1, tm, ck), lambda b_, i, f: (b_, i, nf + f))
        in_specs += [st_a, st_g, st_a, st_g]
        args += [state_rows[0], state_rows[0], state_rows[1], state_rows[1]]
        u_shape = jax.ShapeDtypeStruct((bsz, t, D_FF), F32)
        u_spec = pl.BlockSpec((1, tm, ck), lambda b_, i, f: (b_, i, f))
    else:
        u_shape = jax.ShapeDtypeStruct((bsz, t // tm, CONV_W - 1, D_FF), F32)
        u_spec = pl.BlockSpec((1, 1, CONV_W - 1, ck), lambda b_, i, f: (b_, i, 0, f))
        scratch.append(pltpu.VMEM((nf, 2, 8, ck), F32))
    y, ua, ug = pl.pallas_call(
        functools.partial(_ffn_kernel, per_row_state=per_row),
        out_shape=(jax.ShapeDtypeStruct((bsz, t, d), F32), u_shape, u_shape),
        grid=(bsz, t // tm, nf),
        in_specs=in_specs,
        out_specs=(xrow, u_spec, u_spec),
        scratch_shapes=scratch,
        compiler_params=_cparams(("arbitrary", "arbitrary", "arbitrary")),
        name="ffn",
    )(*args)
    if not per_row:
        ua, ug = ua[:, -1], ug[:, -1]
    return y, jnp.concatenate([ua, ug], axis=-1)


def _ffn_rows_kernel(x_ref, g_ref, sc_ref, sh_ref, gt_ref, gp_ref, wu_ref, cw_ref, cb_ref, wd_ref,
                     y_ref, tail_ref, carry_ref):
    tm = x_ref.shape[1]
    ck = FF_CHUNK

    @pl.when(pl.program_id(1) == 0)
    def _():
        carry_ref[...] = jnp.zeros_like(carry_ref)

    h = (_rms(x_ref[0], g_ref[...]) * (1.0 + sc_ref[0]) + sh_ref[0]).astype(BF16)
    top = lax.broadcasted_iota(jnp.int32, (8, 1), 0)
    acc = jnp.zeros((tm, x_ref.shape[2]), F32)

    def conv(cols):
        u = jnp.dot(h, wu_ref[:, cols], preferred_element_type=F32)
        r1 = pltpu.roll(u, 1, 0)
        r2 = pltpu.roll(u, 2, 0)
        prev = carry_ref[:, cols]
        u1 = jnp.concatenate([jnp.where(top < 1, prev[1:2], r1[0:8]), r1[8:]], axis=0)
        u2 = jnp.concatenate([jnp.where(top < 1, prev[0:1], jnp.where(top < 2, prev[1:2], r2[0:8])), r2[8:]], axis=0)
        carry_ref[0:2, cols] = u[tm - 2:tm]
        tail_ref[0, 0, :, cols] = u[tm - 2:tm]
        cw = cw_ref[:, cols]
        return cb_ref[:, cols] + cw[0:1] * u2 + cw[1:2] * u1 + cw[2:3] * u

    for c in range(D_FF // ck):
        conv_a = conv(slice(c * ck, (c + 1) * ck))
        conv_g = conv(slice(D_FF + c * ck, D_FF + (c + 1) * ck))
        acc = acc + _dot(jax.nn.gelu(conv_g) * conv_a, wd_ref[c * ck:(c + 1) * ck, :])
    y_ref[0] = x_ref[0] + gt_ref[0] * _rms(acc, gp_ref[...])


def conv_ffn_rows(x, g_pre, sc, sh, gt, g_post, w_up, conv_w, conv_b, w_down, tm):
    bsz, t, d = x.shape
    mod = pl.BlockSpec((1, 1, d), lambda b_, i: (b_, 0, 0))
    xrow = pl.BlockSpec((1, tm, d), lambda b_, i: (b_, i, 0))
    whole = lambda a: pl.BlockSpec(a.shape, lambda b_, i: (0,) * a.ndim, pipeline_mode=pl.Buffered(1))
    cb = conv_b.reshape(1, 2 * D_FF)
    g1, g2 = g_pre.reshape(1, d), g_post.reshape(1, d)
    y, tail = pl.pallas_call(
        _ffn_rows_kernel,
        out_shape=(jax.ShapeDtypeStruct((bsz, t, d), F32),
                   jax.ShapeDtypeStruct((bsz, t // tm, CONV_W - 1, 2 * D_FF), F32)),
        grid=(bsz, t // tm),
        in_specs=[xrow, whole(g1), mod, mod, mod, whole(g2), whole(w_up), whole(conv_w), whole(cb), whole(w_down)],
        out_specs=(xrow, pl.BlockSpec((1, 1, CONV_W - 1, 2 * D_FF), lambda b_, i: (b_, i, 0, 0))),
        scratch_shapes=[pltpu.VMEM((8, 2 * D_FF), F32)],
        compiler_params=_cparams(("arbitrary", "arbitrary")),
        name="ffn_rows",
    )(x, g1, sc, sh, gt, g2, w_up, conv_w, cb, w_down)
    return y, tail[:, -1]


LANE = 128
QUARTERS = 2 * KV_WIDTH // LANE
HEADS_PER_LANE_ROW = LANE // A_HEAD_DIM


def compress_weights(cmp_w1, cmp_pe, cmp_w2):
    eye = jnp.eye(HEADS_PER_LANE_ROW, dtype=F32)
    bd = lambda w: jnp.einsum("gh,...de->...gdhe", eye, w).reshape(w.shape[:-2] + (LANE, LANE))
    w1ab = jnp.concatenate([bd(cmp_w1[:, :CMP_STRIDE]), bd(cmp_w1[:, CMP_STRIDE:])], axis=-1).astype(BF16)
    w1ab = w1ab.reshape(2, CMP_STRIDE // 2, 2 * LANE, 2 * LANE)
    w2 = bd(cmp_w2).astype(BF16)
    w1r = cmp_w1.reshape(2, CMP_BLOCK * A_HEAD_DIM, A_HEAD_DIM)
    pe = cmp_pe.reshape(2, CMP_BLOCK * A_HEAD_DIM, 1)
    return w1ab, w2, w1r, pe


def _compress_chunk(get_x, nrows, w1_ref, w2_ref, w1r_ref, pe_ref, carry_ref, out_ref):
    row = lax.broadcasted_iota(jnp.int32, (nrows, 1), 0)
    for kind in range(2):
        peb = jnp.sum(pe_ref[kind] * w1r_ref[kind], axis=0, keepdims=True)
        peb = jnp.concatenate([peb] * HEADS_PER_LANE_ROW, axis=1)
        for half in range(2):
            acc = jnp.zeros((nrows, 2 * LANE), F32)
            for j in range(CMP_STRIDE // 2):
                x = get_x(2 * kind + half, j).astype(BF16)
                acc = acc + jnp.dot(x, w1_ref[kind, j], preferred_element_type=F32)
            acc_a, acc_b = acc[:, :LANE], acc[:, LANE:]
            a_shift = jnp.where(row == 0, carry_ref[kind, half], pltpu.roll(acc_a, 1, 0))
            carry_ref[kind, half] = acc_a[nrows - 1:nrows]
            hid = jax.nn.gelu(a_shift + acc_b + peb)
            out_ref[0, kind, :, half * LANE:(half + 1) * LANE] = _dot(hid, w2_ref[kind])


def _cmp_prompt_kernel(kv_ref, w1_ref, w2_ref, w1r_ref, pe_ref, out_ref, carry_ref):
    nseg = kv_ref.shape[1] // (CMP_STRIDE * QUARTERS)
    carry_ref[...] = jnp.zeros_like(carry_ref)
    offset = lambda quarter, s: kv_ref[0, pl.ds(QUARTERS * s + quarter, nseg, stride=CMP_STRIDE * QUARTERS), :]
    get_x = lambda quarter, j: jnp.concatenate([offset(quarter, 2 * j), offset(quarter, 2 * j + 1)], axis=1)
    _compress_chunk(get_x, nseg, w1_ref, w2_ref, w1r_ref, pe_ref, carry_ref, out_ref)


def _cmp_weight_specs():
    zero = lambda n: (lambda *_: (0,) * n)
    return [pl.BlockSpec((2, CMP_STRIDE // 2, 2 * LANE, 2 * LANE), zero(4)),
            pl.BlockSpec((2, LANE, LANE), zero(3)),
            pl.BlockSpec((2, CMP_BLOCK * A_HEAD_DIM, A_HEAD_DIM), zero(3)),
            pl.BlockSpec((2, CMP_BLOCK * A_HEAD_DIM, 1), zero(3))]


def compress_prompt(kv_cmp, cw):
    bsz, t, w = kv_cmp.shape
    nseg = t // CMP_STRIDE
    return pl.pallas_call(
        _cmp_prompt_kernel,
        out_shape=jax.ShapeDtypeStruct((bsz, 2, nseg, KV_WIDTH), F32),
        grid=(bsz,),
        in_specs=[pl.BlockSpec((1, t * QUARTERS, LANE), lambda b_: (b_, 0, 0))] + _cmp_weight_specs(),
        out_specs=pl.BlockSpec((1, 2, nseg, KV_WIDTH), lambda b_: (b_, 0, 0, 0)),
        scratch_shapes=[pltpu.VMEM((2, 2, 1, LANE), F32)],
        compiler_params=_cparams(("arbitrary",)),
        name="cmp_prompt",
    )(kv_cmp.reshape(bsz, t * QUARTERS, LANE), *cw)


def feature_major_pool(cache):
    n_pool = cache.shape[0]
    return jnp.transpose(cache, (0, 2, 3, 4, 1)).reshape(n_pool, 2 * KV_WIDTH, PAGE_SIZE)


def _page_copy(pool_ref, dst, sem_ref, pt_ref, step, slot, k, n_chunks, pages):
    b_ = step // n_chunks
    c = step % n_chunks
    pid = pt_ref[b_, c * pages + k]
    return pltpu.make_async_copy(pool_ref.at[pid], dst(slot, k), sem_ref.at[slot])


def _page_pipeline(pool_ref, dst, sem_ref, pt_ref, n_chunks, pages):
    step = pl.program_id(0) * n_chunks + pl.program_id(1)
    total = pl.num_programs(0) * n_chunks
    slot = step % 2

    def start(st, sl):
        for k in range(pages):
            _page_copy(pool_ref, dst, sem_ref, pt_ref, st, sl, k, n_chunks, pages).start()

    @pl.when(step == 0)
    def _():
        start(step, slot)

    @pl.when(step + 1 < total)
    def _():
        start(step + 1, 1 - slot)

    for k in range(pages):
        _page_copy(pool_ref, dst, sem_ref, pt_ref, step, slot, k, n_chunks, pages).wait()
    return slot


def _cmp_sample_kernel(pt_ref, pool_ref, w1_ref, w2_ref, w1r_ref, pe_ref, out_ref,
                       buf_ref, sem_ref, x_ref, carry_ref, *, n_chunks, pages):
    slot = _page_pipeline(pool_ref, lambda sl, k: buf_ref.at[sl, k], sem_ref, pt_ref, n_chunks, pages)

    @pl.when(pl.program_id(1) == 0)
    def _():
        carry_ref[...] = jnp.zeros_like(carry_ref)

    segs = PAGE_SIZE // CMP_STRIDE
    nrows = pages * segs
    half_rows = PAGE_SIZE // 2
    dst = lax.broadcasted_iota(jnp.int32, (PAGE_SIZE, PAGE_SIZE), 0)
    src = lax.broadcasted_iota(jnp.int32, (PAGE_SIZE, PAGE_SIZE), 1)
    wanted = (dst % segs) * CMP_STRIDE + 2 * ((dst % half_rows) // segs) + dst // half_rows
    pick = jnp.where(src == wanted, 1.0, 0.0).astype(BF16)

    def relayout(p, carry):
        for quarter in range(QUARTERS):
            t = buf_ref[slot, p, quarter * LANE:(quarter + 1) * LANE, :]
            y = _dot_nt(pick, t)
            x_ref[quarter, p] = jnp.concatenate([y[:half_rows], y[half_rows:]], axis=1)
        return carry

    lax.fori_loop(0, pages, relayout, 0, unroll=8)

    def get_x(quarter, j):
        return x_ref[quarter, :, j * segs:(j + 1) * segs, :].reshape(nrows, 2 * LANE)

    _compress_chunk(get_x, nrows, w1_ref, w2_ref, w1r_ref, pe_ref, carry_ref, out_ref)


def _page_chunk(n_pages):
    return math.gcd(n_pages, 32)


def compress_paged(pool, page_table, cw):
    dbs, n_pages = page_table.shape
    pages = _page_chunk(n_pages)
    n_chunks = n_pages // pages
    rows = pages * PAGE_SIZE // CMP_STRIDE
    return pl.pallas_call(
        functools.partial(_cmp_sample_kernel, n_chunks=n_chunks, pages=pages),
        out_shape=jax.ShapeDtypeStruct((dbs, 2, n_chunks * rows, KV_WIDTH), F32),
        grid_spec=pltpu.PrefetchScalarGridSpec(
            num_scalar_prefetch=1,
            grid=(dbs, n_chunks),
            in_specs=[pl.BlockSpec(memory_space=pl.ANY)] + _cmp_weight_specs(),
            out_specs=pl.BlockSpec((1, 2, rows, KV_WIDTH), lambda b_, c, pt: (b_, 0, c, 0)),
            scratch_shapes=[pltpu.VMEM((2, pages, 2 * KV_WIDTH, PAGE_SIZE), F32),
                            pltpu.SemaphoreType.DMA((2,)),
                            pltpu.VMEM((QUARTERS, pages, PAGE_SIZE // 2, 2 * LANE), F32),
                            pltpu.VMEM((2, 2, 1, LANE), F32)]),
        compiler_params=_cparams(("arbitrary", "arbitrary")),
        name="cmp_paged",
    )(page_table, pool, *cw)


def _masked_softmax_rows(s, mask):
    s = jnp.where(mask, s, NEG_INF)
    m = jnp.max(s, axis=-1, keepdims=True)
    m = jnp.where(m == NEG_INF, 0.0, m)
    e = jnp.exp(s - m)
    return e / jnp.maximum(jnp.sum(e, axis=-1, keepdims=True), TINY)


def _topk_mask(score, k, axis):
    n = score.shape[axis]
    idx = lax.broadcasted_iota(jnp.int32, score.shape, axis)
    sel = jnp.zeros(score.shape, F32)
    for _ in range(k):
        mx = jnp.max(score, axis=axis, keepdims=True)
        first = jnp.min(jnp.where(score == mx, idx, n), axis=axis, keepdims=True)
        pick = idx == first
        sel = jnp.where(pick, 1.0, sel)
        score = jnp.where(pick, NEG_INF, score)
    return sel


def _flash_tile(carry, s, v, v_transposed=False):
    m_old, l_old, acc = carry
    m_new = jnp.maximum(m_old, jnp.max(s, axis=-1, keepdims=True))
    m_safe = jnp.where(m_new == NEG_INF, 0.0, m_new)
    p = jnp.exp(s - m_safe)
    alpha = jnp.exp(m_old - m_safe)
    pv = _dot_nt(p, v) if v_transposed else _dot(p, v)
    return m_new, alpha * l_old + jnp.sum(p, axis=-1, keepdims=True), alpha * acc + pv


MASKED = -(2.0 ** 100)
LOG2E = math.log2(math.e)


def _flash_cols(carry, s, v_t):
    m_old, l_old, acc = carry
    m_new = jnp.maximum(m_old, jnp.max(s, axis=0, keepdims=True))
    p = jnp.exp2(s - m_new)
    alpha = jnp.exp2(m_old - m_new)
    return m_new, alpha * l_old + jnp.sum(p, axis=0, keepdims=True), alpha * acc + _dot(v_t, p)


def _flash_cols_init(cols, dv):
    return jnp.full((1, cols), MASKED, F32), jnp.zeros((1, cols), F32), jnp.zeros((dv, cols), F32)


def _flash_init(rows, dv):
    return jnp.full((rows, 1), NEG_INF, F32), jnp.zeros((rows, 1), F32), jnp.zeros((rows, dv), F32)


def _flash_out(carry):
    _, l, acc = carry
    return acc / jnp.maximum(l, TINY)


KEY_TILE = 512


def _nsa_prompt_kernel(q_ref, small_ref, cmp_ref, ks_ref, vs_ref, kw_ref, vw_ref, cov_ref, exp_ref, o_ref,
                       *, tq, n_slc):
    i = pl.program_id(1)
    q0 = i * tq
    tk = math.gcd(ks_ref.shape[1], KEY_TILE)
    cols = A_GROUP * tq
    q_t = (q_ref[0] * (A_HEAD_DIM ** -0.5 * LOG2E)).T.astype(BF16)
    gate_t = jax.nn.sigmoid(small_ref[0]).T
    ncmp = cmp_ref.shape[2]
    qpos = q0 + lax.broadcasted_iota(jnp.int32, (1, tq), 1)
    jcol = lax.broadcasted_iota(jnp.int32, (ncmp, 1), 0)
    cmp_mask = (jcol >= 1) & (jcol * CMP_STRIDE + (CMP_BLOCK - CMP_STRIDE - 1) <= qpos)
    blk = lax.broadcasted_iota(jnp.int32, (n_slc, 1), 0)
    cur = qpos // SLC_BLOCK
    forced = (blk == 0) | (blk == cur) | (blk == cur - 1)
    future = blk * SLC_BLOCK > qpos
    krow = lax.broadcasted_iota(jnp.int32, (tk, 1), 0)
    lanes = lambda a, r: a[:, r * tq:(r + 1) * tq]
    wk = min(WINDOW + tq, kw_ref.shape[1])
    w_off = pl.multiple_of(jnp.maximum(q0 - WINDOW, 0), tq)
    wpos = w_off + lax.broadcasted_iota(jnp.int32, (wk, 1), 0)
    band = (wpos <= qpos) & (wpos > qpos - WINDOW)
    pieces = []

    for g in range(A_KV_HEADS):
        gs = slice(g * A_HEAD_DIM, (g + 1) * A_HEAD_DIM)
        head = lambda r: slice((g * A_GROUP + r) * A_HEAD_DIM, (g * A_GROUP + r + 1) * A_HEAD_DIM)
        qg = jnp.concatenate([q_t[head(r)] for r in range(A_GROUP)], axis=1)

        s_c = _dot(cmp_ref[0, 0, :, gs], qg)
        p_r = []
        for r in range(A_GROUP):
            s_r = jnp.where(cmp_mask, lanes(s_c, r), NEG_INF)
            m = jnp.max(s_r, axis=0, keepdims=True)
            e = jnp.exp2(s_r - jnp.where(m == NEG_INF, 0.0, m))
            p_r.append(e / jnp.maximum(jnp.sum(e, axis=0, keepdims=True), TINY))
        o_c = _dot_tn(cmp_ref[0, 1, :, gs], jnp.concatenate(p_r, axis=1))
        imp_t = _dot(cov_ref[...], p_r[0] + p_r[1] + p_r[2] + p_r[3])
        score = jnp.where(future, -BIG, imp_t + jnp.where(forced, BIG, 0.0))
        sel_t = _topk_mask(score, min(SLC_TOPK, n_slc), 0)
        sel_bias = jnp.concatenate([jnp.where(sel_t > 0.5, 0.0, MASKED),
                                    jnp.zeros((exp_ref.shape[1] - n_slc, tq), F32)], axis=0)
        sel_bias = jnp.concatenate([sel_bias] * A_GROUP, axis=1).astype(BF16)

        def slc_scores(off):
            return (jnp.dot(ks_ref[0, pl.ds(off, tk), gs], qg, preferred_element_type=F32) +
                    jnp.dot(exp_ref[pl.ds(off, tk), :], sel_bias, preferred_element_type=F32))

        def slc_step(j, carry):
            off = pl.multiple_of(j * tk, tk)
            return _flash_cols(carry, slc_scores(off), vs_ref[0, gs, pl.ds(off, tk)])

        j_last = (q0 + tq - 1) // tk
        carry = lax.fori_loop(0, j_last, slc_step, _flash_cols_init(cols, A_HEAD_DIM))
        off = pl.multiple_of(j_last * tk, tk)
        causal = off + krow <= qpos
        s = slc_scores(off)
        s = jnp.concatenate([jnp.where(causal, lanes(s, r), MASKED) for r in range(A_GROUP)], axis=1)
        _, l_s, o_s = _flash_cols(carry, s, vs_ref[0, gs, pl.ds(off, tk)])

        s = jnp.dot(kw_ref[0, pl.ds(w_off, wk), gs], qg, preferred_element_type=F32)
        s = jnp.concatenate([jnp.where(band, lanes(s, r), MASKED) for r in range(A_GROUP)], axis=1)
        _, l_w, o_w = _flash_cols(_flash_cols_init(cols, A_HEAD_DIM), s, vw_ref[0, gs, pl.ds(w_off, wk)])
        o_s = o_s / jnp.maximum(l_s, TINY)
        o_w = o_w / jnp.maximum(l_w, TINY)

        for r in range(A_GROUP):
            row = SMALL_AG + 3 * (g * A_GROUP + r)
            pieces.append(gate_t[row:row + 1] * lanes(o_c, r) + gate_t[row + 1:row + 2] * lanes(o_s, r) +
                          gate_t[row + 2:row + 3] * lanes(o_w, r))
    o_ref[0] = jnp.concatenate(pieces, axis=0).T


def _coverage(n_cmp_rows, n_slc):
    cs = (np.arange(n_cmp_rows) - 1) * CMP_STRIDE
    ss = np.arange(n_slc) * SLC_BLOCK
    lo = np.maximum(cs[:, None], ss[None, :])
    hi = np.minimum(cs[:, None] + CMP_BLOCK, ss[None, :] + SLC_BLOCK)
    cov = np.clip(hi - lo, 0, None) / CMP_BLOCK
    cov[0] = 0.0
    return cov.astype(np.float32)


def _block_expand(n_rows, n_keys):
    return (np.arange(n_rows)[:, None] == (np.arange(n_keys) // SLC_BLOCK)[None, :]).astype(np.float32)


def nsa_prompt(q_rot, z, cmp, kv_slc, kv_win, tq):
    bsz, t, _ = q_rot.shape
    n_slc = t // SLC_BLOCK
    ncmp = cmp.shape[2]
    kk = lambda kv: kv[..., :KV_WIDTH].astype(BF16)
    vt = lambda kv: jnp.swapaxes(kv[..., KV_WIDTH:], 1, 2).astype(BF16)
    cov_t = jnp.asarray(_coverage(ncmp, n_slc).T, BF16)
    n_exp = -(-n_slc // 128) * 128
    expand = jnp.asarray(_block_expand(n_exp, t).T, BF16)
    per_b = lambda shape: pl.BlockSpec((1,) + shape, lambda b_, i: (b_,) + (0,) * len(shape))
    return pl.pallas_call(
        functools.partial(_nsa_prompt_kernel, tq=tq, n_slc=n_slc),
        out_shape=jax.ShapeDtypeStruct((bsz, t, D_MODEL), F32),
        grid=(bsz, t // tq),
        in_specs=[pl.BlockSpec((1, tq, D_MODEL), lambda b_, i: (b_, i, 0)),
                  pl.BlockSpec((1, tq, 128), lambda b_, i: (b_, i, Z_SMALL // 128)),
                  per_b((2, ncmp, KV_WIDTH)),
                  per_b((t, KV_WIDTH)), per_b((KV_WIDTH, t)), per_b((t, KV_WIDTH)), per_b((KV_WIDTH, t)),
                  pl.BlockSpec((n_slc, ncmp), lambda b_, i: (0, 0)),
                  pl.BlockSpec((t, n_exp), lambda b_, i: (0, 0))],
        out_specs=pl.BlockSpec((1, tq, D_MODEL), lambda b_, i: (b_, i, 0)),
        compiler_params=_cparams(("arbitrary", "arbitrary")),
        name="nsa_prompt",
    )(q_rot, z, cmp, kk(kv_slc), vt(kv_slc), kk(kv_win), vt(kv_win), cov_t, expand)


SAMPLE_ROWS = 8
NEW_KEYS = 128


def _nsa_sample_kernel(pt_ref, qbd_ref, gl_ref, cmp_ref, pool_ref, knew_ref, wcache_ref, wnew_ref, cov_ref, exp_ref,
                       o_ref, buf_ref, sem_ref, sel_ref, m_ref, l_ref, acc_ref, oc_ref,
                       *, n_chunks, pages, past, t_real, n_slc):
    c = pl.program_id(1)
    page_window = lambda sl, k: buf_ref.at[sl, :, pl.ds(k * PAGE_SIZE, PAGE_SIZE)]
    slot = _page_pipeline(pool_ref, page_window, sem_ref, pt_ref, n_chunks, pages)
    qbd = qbd_ref[0]
    rows = qbd.shape[0]
    bpc = pages * PAGE_SIZE // SLC_BLOCK
    rq = lax.broadcasted_iota(jnp.int32, (rows, 1), 0) % t_real
    qpos = past + rq

    @pl.when(c == 0)
    def _():
        ncmp = cmp_ref.shape[2]
        nbp = cov_ref.shape[1]
        jrow = lax.broadcasted_iota(jnp.int32, (1, ncmp), 1)
        cmp_mask = (jrow >= 1) & (jrow * CMP_STRIDE + (CMP_BLOCK - CMP_STRIDE - 1) <= qpos)
        p = _masked_softmax_rows(_dot_nt(qbd, cmp_ref[0, 0]), cmp_mask)
        oc_ref[...] = _dot(p, cmp_ref[0, 1])
        ri = lax.broadcasted_iota(jnp.int32, (rows, rows), 0)
        ci = lax.broadcasted_iota(jnp.int32, (rows, rows), 1)
        group_rows = A_GROUP * t_real
        same = (ri // group_rows == ci // group_rows) & (ri % t_real == ci % t_real)
        p_group = _dot(jnp.where(same, 1.0, 0.0), p)
        imp = _dot(p_group, cov_ref[...])
        blk = lax.broadcasted_iota(jnp.int32, (1, nbp), 1)
        cur = qpos // SLC_BLOCK
        forced = (blk == 0) | (blk == cur) | (blk == cur - 1)
        future = blk * SLC_BLOCK > qpos
        score = jnp.where(future, -BIG, imp + jnp.where(forced, BIG, 0.0))
        score = jnp.where(blk < n_slc, score, NEG_INF)
        sel = _topk_mask(score, min(SLC_TOPK, n_slc), 1)
        pad = jnp.zeros((rows, 128 - bpc), F32)
        for cc in range(n_chunks + 1):
            sel_ref[cc] = jnp.concatenate([sel[:, cc * bpc:(cc + 1) * bpc], pad], axis=1).astype(BF16)
        m_ref[...] = jnp.full(m_ref.shape, NEG_INF, F32)
        l_ref[...] = jnp.zeros_like(l_ref)
        acc_ref[...] = jnp.zeros_like(acc_ref)

    k_t = buf_ref[slot, 0:KV_WIDTH, :]
    v_t = buf_ref[slot, KV_WIDTH:2 * KV_WIDTH, :]
    picked = jnp.dot(sel_ref[c], exp_ref[...], preferred_element_type=F32) > 0.5
    s = jnp.where(picked, _dot(qbd, k_t), NEG_INF)
    carry = _flash_tile((m_ref[...], l_ref[...], acc_ref[...]), s, v_t, v_transposed=True)
    m_ref[...], l_ref[...], acc_ref[...] = carry

    @pl.when(c == n_chunks - 1)
    def _():
        zpad = jnp.zeros((NEW_KEYS - SAMPLE_ROWS, 2 * KV_WIDTH), F32)
        kcol = lax.broadcasted_iota(jnp.int32, (1, NEW_KEYS), 1)
        new_mask = (kcol <= rq) & (kcol < t_real)
        knew = jnp.concatenate([knew_ref[0], zpad], axis=0)
        last_picked = sel_ref[n_chunks][:, 0:1].astype(F32) > 0.5
        s_new = jnp.where(new_mask & last_picked, _dot_nt(qbd, knew[:, :KV_WIDTH]), NEG_INF)
        o_s = _flash_out(_flash_tile((m_ref[...], l_ref[...], acc_ref[...]), s_new, knew[:, KV_WIDTH:]))

        wc = wcache_ref[0]
        wb = wc.shape[0]
        wcol = lax.broadcasted_iota(jnp.int32, (1, wb), 1)
        s_w = jnp.where(wcol > rq + (wb - WINDOW), _dot_nt(qbd, wc[:, :KV_WIDTH]), NEG_INF)
        cw = _flash_tile(_flash_init(rows, KV_WIDTH), s_w, wc[:, KV_WIDTH:])
        wnew = jnp.concatenate([wnew_ref[0], zpad], axis=0)
        s_wn = jnp.where(new_mask, _dot_nt(qbd, wnew[:, :KV_WIDTH]), NEG_INF)
        o_w = _flash_out(_flash_tile(cw, s_wn, wnew[:, KV_WIDTH:]))

        gate = jax.nn.sigmoid(gl_ref[0])
        o = gate[:, 0:1] * oc_ref[...] + gate[:, 1:2] * o_s + gate[:, 2:3] * o_w
        lane_g = lax.broadcasted_iota(jnp.int32, (1, KV_WIDTH), 1) // A_HEAD_DIM
        row_g = lax.broadcasted_iota(jnp.int32, (rows, 1), 0) // (A_GROUP * t_real)
        o = jnp.where(lane_g == row_g, o, 0.0)
        o_ref[0] = sum(o[:, g * A_HEAD_DIM:(g + 1) * A_HEAD_DIM] for g in range(A_KV_HEADS))


def nsa_sample(q_rot, z, cmp, pool, page_table, k_new, win_cache, w_new, past, t_real):
    dbs = q_rot.shape[0]
    n_pages = page_table.shape[1]
    pages = _page_chunk(n_pages)
    n_chunks = n_pages // pages
    bpc = pages * PAGE_SIZE // SLC_BLOCK
    ncmp = cmp.shape[2]
    n_slc = -(-(past + t_real) // SLC_BLOCK)
    assert n_slc == n_chunks * bpc + 1 and bpc <= 128
    nbp = -(-((n_chunks + 1) * bpc) // 128) * 128
    rows = A_HEADS * t_real
    q5 = q_rot[:, :t_real].reshape(dbs, t_real, A_KV_HEADS, A_GROUP, A_HEAD_DIM) * (A_HEAD_DIM ** -0.5)
    qbd = jnp.einsum("bqgrd,gh->bgrqhd", q5, jnp.eye(A_KV_HEADS, dtype=F32)).reshape(dbs, rows, KV_WIDTH).astype(BF16)
    gl = z[:, :t_real, Z_SMALL + SMALL_AG:Z_SMALL + SMALL_AG + 3 * A_HEADS].reshape(dbs, t_real, A_HEADS, 3)
    gl = jnp.swapaxes(gl, 1, 2).reshape(dbs, rows, 3)
    cov = np.zeros((ncmp, nbp), np.float32)
    cov[:, :n_slc] = _coverage(ncmp, n_slc)
    expand = jnp.asarray(_block_expand(128, pages * PAGE_SIZE), BF16)
    per_b = lambda shape: pl.BlockSpec((1,) + shape, lambda b_, c, pt: (b_,) + (0,) * len(shape))
    const = lambda shape: pl.BlockSpec(shape, lambda b_, c, pt: (0,) * len(shape))
    wb = win_cache.shape[1]
    out = pl.pallas_call(
        functools.partial(_nsa_sample_kernel, n_chunks=n_chunks, pages=pages, past=past, t_real=t_real, n_slc=n_slc),
        out_shape=jax.ShapeDtypeStruct((dbs, rows, A_HEAD_DIM), F32),
        grid_spec=pltpu.PrefetchScalarGridSpec(
            num_scalar_prefetch=1,
            grid=(dbs, n_chunks),
            in_specs=[per_b((rows, KV_WIDTH)), per_b((rows, 3)), per_b((2, ncmp, KV_WIDTH)),
                      pl.BlockSpec(memory_space=pl.ANY),
                      per_b((SAMPLE_ROWS, 2 * KV_WIDTH)), per_b((wb, 2 * KV_WIDTH)), per_b((SAMPLE_ROWS, 2 * KV_WIDTH)),
                      const((ncmp, nbp)), const((128, pages * PAGE_SIZE))],
            out_specs=per_b((rows, A_HEAD_DIM)),
            scratch_shapes=[pltpu.VMEM((2, 2 * KV_WIDTH, pages * PAGE_SIZE), F32),
                            pltpu.SemaphoreType.DMA((2,)),
                            pltpu.VMEM((n_chunks + 1, rows, 128), BF16),
                            pltpu.VMEM((rows, 1), F32), pltpu.VMEM((rows, 1), F32),
                            pltpu.VMEM((rows, KV_WIDTH), F32), pltpu.VMEM((rows, KV_WIDTH), F32)]),
        compiler_params=_cparams(("arbitrary", "arbitrary")),
        name="nsa_sample",
    )(page_table, qbd, gl, cmp, pool, k_new, win_cache, w_new, jnp.asarray(cov, BF16), expand)
    out = jnp.swapaxes(out.reshape(dbs, A_HEADS, t_real, A_HEAD_DIM), 1, 2).reshape(dbs, t_real, D_MODEL)
    return jnp.pad(out, ((0, 0), (0, SAMPLE_ROWS - t_real), (0, 0)))


def _kv_rows(a, bsz, t):
    return a.reshape(bsz, t, 2, A_KV_HEADS, A_HEAD_DIM)


def kernel(x_prompt, x_sample, cache_cmp_kv, cache_slc_kv, cache_win_kv, state_C, state_n, state_m, state_conv,
           page_table, c_prompt, c_sample, w_ada, b_ada, g_pre_mix, g_post_mix, g_pre_ffn, g_post_ffn, w_in, b_in,
           m_norm_w, cmp_w1, cmp_pe, cmp_w2, w_branch_m, w_branch_a, w_out, w_up, conv_w, conv_b, w_down):
    depth = w_ada.shape[0]
    bsz, t, d = x_prompt.shape
    dbs, ts, _ = x_sample.shape
    n_pages = page_table.shape[1]
    past = n_pages * PAGE_SIZE
    assert ts <= SAMPLE_ROWS and (past + ts) // CMP_STRIDE == past // CMP_STRIDE and past >= WINDOW
    srows = dbs * SAMPLE_ROWS
    xp = x_prompt.astype(F32)
    xs = jnp.pad(x_sample.astype(F32), ((0, 0), (0, SAMPLE_ROWS - ts), (0, 0))).reshape(1, srows, d)
    c_all = jnp.concatenate([c_prompt, c_sample], axis=0).astype(F32)
    c_all = jnp.pad(c_all, ((0, (-c_all.shape[0]) % 8), (0, 0)))
    tab_p = rope_tables(jnp.arange(t, dtype=jnp.int32))
    tab_s = rope_tables(jnp.tile(past + jnp.arange(SAMPLE_ROWS, dtype=jnp.int32), dbs))
    lchunk = math.gcd(t, 256)
    tm_p = math.gcd(t, 512)
    p_states, s_states = [], []
    for l in range(depth):
        mod = ada_modulation(c_all, w_ada[l], b_ada[l])
        mod_p = [m[:, None, :] for m in jnp.split(mod[:bsz], 6, axis=-1)]
        mod_s = [jnp.repeat(m, SAMPLE_ROWS, axis=0)[None] for m in jnp.split(mod[bsz:bsz + dbs], 6, axis=-1)]
        w_r, b_r = regroup_in_weights(w_in[l], b_in[l])
        cw = compress_weights(cmp_w1[l], cmp_pe[l], cmp_w2[l])
        wm, wa, wo = w_branch_m[l].astype(BF16), w_branch_a[l].astype(BF16), w_out[l].astype(BF16)
        wu, wd = w_up[l].astype(BF16), w_down[l].astype(BF16)

        sh_m, sc_m, gt_m, sh_f, sc_f, gt_f = mod_p
        z = in_projection(xp, g_pre_mix[l], sc_m, sh_m, w_r, b_r, tm=math.gcd(t, 1024))
        q_rot, kv_cmp, kv_slc, kv_win = rope_split(z, tab_p, tm=tm_p)
        hm, p_c, p_n, p_m = mlstm(z, m_norm_w[l], jnp.zeros((bsz, M_HEADS, M_HEAD_DIM, M_HEAD_DIM), F32),
                                  jnp.zeros((bsz, M_HEADS, M_HEAD_DIM), F32), jnp.zeros((bsz, M_HEADS), F32),
                                  lb=lchunk, lp=lchunk, t_real=lchunk)
        ha = nsa_prompt(q_rot, z, compress_prompt(kv_cmp, cw), kv_slc, kv_win, tq=256)
        xp = mix_out(xp, hm, ha, z, gt_m, g_post_mix[l], wm, wa, wo, tm=tm_p)
        xp, p_conv = conv_ffn_rows(xp, g_pre_ffn[l], sc_f, sh_f, gt_f, g_post_ffn[l], wu, conv_w[l], conv_b[l], wd,
                                   tm=tm_p)
        wkeep = min(WINDOW, t)
        p_states.append((_kv_rows(kv_cmp, bsz, t), _kv_rows(kv_slc, bsz, t), _kv_rows(kv_win[:, t - wkeep:], bsz, wkeep),
                         p_c, p_n, p_m, p_conv))

        sh_m, sc_m, gt_m, sh_f, sc_f, gt_f = mod_s
        z = in_projection(xs, g_pre_mix[l], sc_m, sh_m, w_r, b_r, tm=srows)
        q_rot, kv_cmp, kv_slc, kv_win = rope_split(z, tab_s, tm=srows)
        z3 = z.reshape(dbs, SAMPLE_ROWS, Z_WIDTH)
        hm, s_c, s_n, s_m = mlstm(z3, m_norm_w[l], state_C[l].astype(F32), state_n[l].astype(F32),
                                  state_m[l].astype(F32), lb=SAMPLE_ROWS, lp=128, t_real=ts)
        cmp_s = compress_paged(feature_major_pool(cache_cmp_kv[l].astype(F32)), page_table, cw)
        new3 = lambda a: a.reshape(dbs, SAMPLE_ROWS, 2 * KV_WIDTH)
        win_cache = cache_win_kv[l].astype(F32).reshape(dbs, -1, 2 * KV_WIDTH)
        ha = nsa_sample(q_rot.reshape(dbs, SAMPLE_ROWS, d), z3, cmp_s, feature_major_pool(cache_slc_kv[l].astype(F32)),
                        page_table, new3(kv_slc), win_cache, new3(kv_win), past, ts)
        xs = mix_out(xs, hm.reshape(1, srows, d), ha.reshape(1, srows, d), z, gt_m, g_post_mix[l], wm, wa, wo, tm=srows)
        st = state_conv[l].astype(F32)
        s2 = jnp.pad(st, ((0, 0), (0, SAMPLE_ROWS - (CONV_W - 1)), (0, 0))).reshape(1, srows, 2 * D_FF)
        s1 = jnp.pad(st[:, 1:], ((0, 0), (0, SAMPLE_ROWS - 1), (0, 0))).reshape(1, srows, 2 * D_FF)
        xs, u = conv_ffn(xs, g_pre_ffn[l], sc_f, sh_f, gt_f, g_post_ffn[l], wu, conv_w[l], conv_b[l], wd, tm=srows,
                         state_rows=(s1, s2))
        wb = win_cache.shape[1]
        s_win = jnp.concatenate([win_cache, new3(kv_win)[:, :ts]], axis=1)[:, ts:]
        s_conv = jnp.concatenate([st, u.reshape(dbs, SAMPLE_ROWS, 2 * D_FF)[:, :ts]], axis=1)[:, ts:]
        s_states.append((_kv_rows(new3(kv_cmp)[:, :ts], dbs, ts), _kv_rows(new3(kv_slc)[:, :ts], dbs, ts),
                         _kv_rows(s_win, dbs, wb), s_c, s_n, s_m, s_conv))

    stack = lambda states: [jnp.stack([s[i] for s in states]) for i in range(7)]
    y_sample = xs.reshape(dbs, SAMPLE_ROWS, d)[:, :ts]
    return (xp, y_sample, *stack(p_states), *stack(s_states))
```

```python
import functools
import math

import numpy as np
import jax
import jax.numpy as jnp
from jax import lax
from jax.experimental import pallas as pl
from jax.experimental.pallas import tpu as pltpu

F32 = jnp.float32
BF16 = jnp.bfloat16

D_MODEL = 1024
M_HEADS = 4
M_HEAD_DIM = 256
A_HEADS = 16
A_HEAD_DIM = 64
A_KV_HEADS = 4
A_GROUP = 4
KV_WIDTH = A_KV_HEADS * A_HEAD_DIM
CMP_STRIDE = 16
CMP_BLOCK = 32
SLC_BLOCK = 64
SLC_TOPK = 16
WINDOW = 512
ROPE_THETA = 500000.0
ROPE_DIM = 16
BIG = 1e6
D_FF = 2816
CONV_W = 3
NORM_EPS = 1e-6
PAGE_SIZE = 128
NEG_INF = float("-inf")
TINY = float(np.finfo(np.float32).tiny)

Z_MQ, Z_MK, Z_MV, Z_MO, Z_AQ, Z_GA, Z_GB, Z_AKV, Z_SMALL = 0, 1024, 2048, 3072, 4096, 5120, 6144, 7168, 8704
Z_WIDTH = 9216
SMALL_MI, SMALL_MF, SMALL_AG = 0, 4, 8

VMEM_LIMIT = 48 * 1024 * 1024


def _cparams(sem):
    return pltpu.CompilerParams(dimension_semantics=sem, vmem_limit_bytes=VMEM_LIMIT)


def _dot(a, b):
    return jnp.dot(a.astype(BF16), b.astype(BF16), preferred_element_type=F32)


def _dot_nt(a, b):
    return lax.dot_general(a.astype(BF16), b.astype(BF16), (((1,), (1,)), ((), ())), preferred_element_type=F32)


def _dot_tn(a, b):
    return lax.dot_general(a.astype(BF16), b.astype(BF16), (((0,), (0,)), ((), ())), preferred_element_type=F32)


def _split3(x):
    x1 = x.astype(BF16)
    r1 = x - x1.astype(F32)
    x2 = r1.astype(BF16)
    x3 = (r1 - x2.astype(F32)).astype(BF16)
    return x1, x2, x3


def _rms(x, g):
    return x * lax.rsqrt(jnp.mean(x * x, axis=-1, keepdims=True) + NORM_EPS) * g


def _ada_kernel(c_ref, w_ref, b_ref, o_ref):
    c = c_ref[...]
    o_ref[...] = _dot(c * jax.nn.sigmoid(c), w_ref[...]) + b_ref[...]


def ada_modulation(c, w_ada, b_ada):
    rows, d = c.shape
    n = w_ada.shape[1]
    tn = 512
    return pl.pallas_call(
        _ada_kernel,
        out_shape=jax.ShapeDtypeStruct((rows, n), F32),
        grid=(n // tn,),
        in_specs=[pl.BlockSpec((rows, d), lambda j: (0, 0)),
                  pl.BlockSpec((d, tn), lambda j: (0, j)),
                  pl.BlockSpec((1, tn), lambda j: (0, j))],
        out_specs=pl.BlockSpec((rows, tn), lambda j: (0, j)),
        compiler_params=_cparams(("arbitrary",)),
        name="ada",
    )(c, w_ada, b_ada.reshape(1, n))


def _inproj_kernel(x_ref, g_ref, sc_ref, sh_ref, w_ref, b_ref, o_ref, h_ref):
    @pl.when(pl.program_id(2) == 0)
    def _():
        h = _rms(x_ref[0], g_ref[...]) * (1.0 + sc_ref[0]) + sh_ref[0]
        h_ref[...] = h.astype(BF16)

    o_ref[0] = jnp.dot(h_ref[...], w_ref[...], preferred_element_type=F32) + b_ref[...]


def in_projection(x, g, sc, sh, w_bf16, b, tm):
    bsz, t, d = x.shape
    r = sc.shape[1]
    rb = 1 if r == 1 else tm
    tn = 1024
    mod_spec = pl.BlockSpec((1, rb, d), (lambda b_, i, j: (b_, 0, 0)) if r == 1 else (lambda b_, i, j: (b_, i, 0)))
    return pl.pallas_call(
        _inproj_kernel,
        out_shape=jax.ShapeDtypeStruct((bsz, t, Z_WIDTH), F32),
        grid=(bsz, t // tm, Z_WIDTH // tn),
        in_specs=[pl.BlockSpec((1, tm, d), lambda b_, i, j: (b_, i, 0)),
                  pl.BlockSpec((1, d), lambda b_, i, j: (0, 0)),
                  mod_spec, mod_spec,
                  pl.BlockSpec((d, tn), lambda b_, i, j: (0, j)),
                  pl.BlockSpec((1, tn), lambda b_, i, j: (0, j))],
        out_specs=pl.BlockSpec((1, tm, tn), lambda b_, i, j: (b_, i, j)),
        scratch_shapes=[pltpu.VMEM((tm, d), BF16)],
        compiler_params=_cparams(("arbitrary", "arbitrary", "arbitrary")),
        name="inproj",
    )(x, g.reshape(1, d), sc, sh, w_bf16, b.reshape(1, Z_WIDTH))


def regroup_in_weights(w_in, b_in):
    mw = M_HEADS * M_HEAD_DIM
    o_mi = 4 * mw
    o_aq = o_mi + 2 * M_HEADS
    o_akv = o_aq + A_HEADS * A_HEAD_DIM
    o_ag = o_akv + 6 * KV_WIDTH
    o_ga = o_ag + 3 * A_HEADS
    o_gb = o_ga + D_MODEL

    def regroup(a):
        lead = a.shape[:-1]
        parts = [a[..., :o_mi], a[..., o_aq:o_akv], a[..., o_ga:o_gb], a[..., o_gb:o_gb + D_MODEL],
                 a[..., o_akv:o_ag], a[..., o_mi:o_aq], a[..., o_ag:o_ga],
                 jnp.zeros(lead + (128 - 2 * M_HEADS - 3 * A_HEADS,), a.dtype),
                 jnp.zeros(lead + (Z_WIDTH - Z_SMALL - 128,), a.dtype)]
        return jnp.concatenate(parts, axis=-1)

    return regroup(w_in).astype(BF16), regroup(b_in)


def rope_tables(pos):
    half = ROPE_DIM // 2
    inv_freq = ROPE_THETA ** (-jnp.arange(half, dtype=F32) / half)
    ang = pos.astype(F32)[:, None] * inv_freq
    cos, sin = jnp.cos(ang), jnp.sin(ang)
    rows = pos.shape[0]
    zeros = jnp.zeros((rows, half), F32)
    rest1 = jnp.ones((rows, A_HEAD_DIM - ROPE_DIM), F32)
    rest0 = jnp.zeros((rows, A_HEAD_DIM - ROPE_DIM), F32)
    c = jnp.concatenate([cos, cos, rest1], axis=1)
    sa = jnp.concatenate([zeros, sin, rest0], axis=1)
    sb = jnp.concatenate([-sin, zeros, rest0], axis=1)
    return tuple(jnp.concatenate([a, a], axis=1) for a in (c, sa, sb))


def _rope_apply(x, c, sa, sb):
    w = x.shape[1]
    n = w // 128
    ct, sat, sbt = (jnp.concatenate([a] * n, axis=1) for a in (c, sa, sb))
    return x * ct + pltpu.roll(x, ROPE_DIM // 2, 1) * sat + pltpu.roll(x, w - ROPE_DIM // 2, 1) * sbt


def _rope_kernel(q_ref, c_ref, s_ref, w_ref, cos_ref, sa_ref, sb_ref, qo_ref, co_ref, so_ref, wo_ref):
    c, sa, sb = cos_ref[...], sa_ref[...], sb_ref[...]
    qo_ref[0] = _rope_apply(q_ref[0], c, sa, sb)
    for src, dst in ((c_ref, co_ref), (s_ref, so_ref), (w_ref, wo_ref)):
        kv = src[0]
        dst[0] = jnp.concatenate([_rope_apply(kv[:, :KV_WIDTH], c, sa, sb), kv[:, KV_WIDTH:]], axis=1)


def rope_split(z, tables, tm):
    bsz, t, _ = z.shape
    nt = t // tm
    kvw = 2 * KV_WIDTH
    tab_spec = pl.BlockSpec((tm, 128), lambda b_, i: (i, 0))
    return pl.pallas_call(
        _rope_kernel,
        out_shape=(jax.ShapeDtypeStruct((bsz, t, D_MODEL), F32),) + (jax.ShapeDtypeStruct((bsz, t, kvw), F32),) * 3,
        grid=(bsz, nt),
        in_specs=[pl.BlockSpec((1, tm, D_MODEL), lambda b_, i: (b_, i, Z_AQ // D_MODEL)),
                  pl.BlockSpec((1, tm, kvw), lambda b_, i: (b_, i, Z_AKV // kvw)),
                  pl.BlockSpec((1, tm, kvw), lambda b_, i: (b_, i, Z_AKV // kvw + 1)),
                  pl.BlockSpec((1, tm, kvw), lambda b_, i: (b_, i, Z_AKV // kvw + 2)),
                  tab_spec, tab_spec, tab_spec],
        out_specs=(pl.BlockSpec((1, tm, D_MODEL), lambda b_, i: (b_, i, 0)),) +
                  (pl.BlockSpec((1, tm, kvw), lambda b_, i: (b_, i, 0)),) * 3,
        compiler_params=_cparams(("arbitrary", "arbitrary")),
        name="rope",
    )(z, z, z, z, *tables)


def _mlstm_kernel(q_ref, k_ref, v_ref, o_ref, s_ref, nw_ref, c0_ref, n0_ref, m0_ref,
                  h_ref, c_ref, n_ref, m_ref, *, lb, lp, t_real):
    @pl.when(pl.program_id(1) == 0)
    def _():
        c_ref[...] = c0_ref[...]
        n_ref[...] = n0_ref[...]
        m_ref[...] = m0_ref[...]

    def pad(a):
        if lb == lp:
            return a
        return jnp.concatenate([a, jnp.zeros((lp - lb, a.shape[1]), a.dtype)], axis=0)

    small = pad(s_ref[0])
    small_t = small.T
    row_c = lax.broadcasted_iota(jnp.int32, (lp, 1), 0)
    row_r = lax.broadcasted_iota(jnp.int32, (1, lp), 1)
    li_col_all = jnp.where(row_c < t_real, small, NEG_INF)
    lf_col_all = jnp.where(row_c < t_real, jax.nn.log_sigmoid(small), 0.0)
    li_row_all = jnp.where(row_r < t_real, small_t[0:8], NEG_INF)
    lf_row_all = jnp.where(row_r < t_real, jax.nn.log_sigmoid(small_t[0:8]), 0.0)
    rr = lax.broadcasted_iota(jnp.int32, (lp, lp), 0)
    cc = lax.broadcasted_iota(jnp.int32, (lp, lp), 1)
    causal = cc <= rr
    tril = jnp.where(causal, 1.0, 0.0).astype(BF16)
    triu = jnp.where(rr <= cc, 1.0, 0.0).astype(BF16)
    b_col_all = sum(jnp.dot(tril, p, preferred_element_type=F32) for p in _split3(lf_col_all))
    b_row_all = sum(jnp.dot(p, triu, preferred_element_type=F32) for p in _split3(lf_row_all))

    q_all, k_all, v_all, o_all = pad(q_ref[0]), pad(k_ref[0]), pad(v_ref[0]), pad(o_ref[0])
    nw = nw_ref[...]
    for h in range(M_HEADS):
        hs = slice(h * M_HEAD_DIM, (h + 1) * M_HEAD_DIM)
        qf = q_all[:, hs]
        kf = k_all[:, hs] * (M_HEAD_DIM ** -0.5)
        vf = v_all[:, hs]
        li_row = li_row_all[SMALL_MI + h:SMALL_MI + h + 1, :]
        b_row = b_row_all[SMALL_MF + h:SMALL_MF + h + 1, :]
        li_col = li_col_all[:, SMALL_MI + h:SMALL_MI + h + 1]
        b_col = b_col_all[:, SMALL_MF + h:SMALL_MF + h + 1]
        m_prev = m_ref[0, h]
        c_prev = c_ref[0, h]
        n_prev = n_ref[0, h]

        dlog = jnp.where(causal, b_col - b_row + li_row, NEG_INF)
        inter = m_prev + b_col
        mt = jnp.maximum(inter, jnp.max(dlog, axis=1, keepdims=True))
        a = jnp.exp(inter - mt)
        s = _dot_nt(qf, kf) * jnp.exp(dlog - mt)
        num = a * _dot_nt(qf, c_prev) + _dot(s, vf)
        den = a * jnp.sum(qf * n_prev, axis=1, keepdims=True) + jnp.sum(s, axis=1, keepdims=True)
        hh = num / jnp.maximum(jnp.abs(den), jnp.exp(-mt))
        mu = jnp.mean(hh, axis=1, keepdims=True)
        var = jnp.mean(jnp.square(hh - mu), axis=1, keepdims=True)
        out = (hh - mu) * lax.rsqrt(var + NORM_EPS) * nw[:, hs] * jax.nn.sigmoid(o_all[:, hs])
        h_ref[0, :, hs] = out[:lb]

        bl = b_row[:, lp - 1:lp]
        wlog = bl - b_col + li_col
        m_new = jnp.maximum(m_prev + bl, jnp.max(wlog, axis=0, keepdims=True))
        w = jnp.exp(wlog - m_new)
        decay = jnp.exp(m_prev + bl - m_new)
        c_ref[0, h] = decay * c_prev + _dot_tn(vf * w, kf)
        n_ref[0, h] = decay * n_prev + jnp.sum(w * kf, axis=0, keepdims=True)
        m_ref[0, h] = m_new


def mlstm(z, norm_w, c0, n0, m0, lb, lp, t_real):
    bsz, t, _ = z.shape
    nc = t // lb
    mw = M_HEADS * M_HEAD_DIM
    zspec = lambda col: pl.BlockSpec((1, lb, mw), lambda b_, c: (b_, c, col // mw))
    cst = lambda shape: pl.BlockSpec((1,) + shape, lambda b_, c: (b_,) + (0,) * len(shape))
    h, c, n, m = pl.pallas_call(
        functools.partial(_mlstm_kernel, lb=lb, lp=lp, t_real=t_real),
        out_shape=(jax.ShapeDtypeStruct((bsz, t, mw), F32),
                   jax.ShapeDtypeStruct((bsz, M_HEADS, M_HEAD_DIM, M_HEAD_DIM), F32),
                   jax.ShapeDtypeStruct((bsz, M_HEADS, 1, M_HEAD_DIM), F32),
                   jax.ShapeDtypeStruct((bsz, M_HEADS, 1, 1), F32)),
        grid=(bsz, nc),
        in_specs=[zspec(Z_MQ), zspec(Z_MK), zspec(Z_MV), zspec(Z_MO),
                  pl.BlockSpec((1, lb, 128), lambda b_, c: (b_, c, Z_SMALL // 128)),
                  pl.BlockSpec((1, mw), lambda b_, c: (0, 0)),
                  cst((M_HEADS, M_HEAD_DIM, M_HEAD_DIM)), cst((M_HEADS, 1, M_HEAD_DIM)), cst((M_HEADS, 1, 1))],
        out_specs=(pl.BlockSpec((1, lb, mw), lambda b_, c: (b_, c, 0)),
                   cst((M_HEADS, M_HEAD_DIM, M_HEAD_DIM)), cst((M_HEADS, 1, M_HEAD_DIM)), cst((M_HEADS, 1, 1))),
        compiler_params=_cparams(("arbitrary", "arbitrary")),
        name="mlstm",
    )(z, z, z, z, z, norm_w.reshape(1, mw), c0, n0.reshape(bsz, M_HEADS, 1, M_HEAD_DIM),
      m0.reshape(bsz, M_HEADS, 1, 1))
    return h, c, n.reshape(bsz, M_HEADS, M_HEAD_DIM), m.reshape(bsz, M_HEADS)


def _mix_kernel(x_ref, hm_ref, ha_ref, ga_ref, gb_ref, gt_ref, g_ref, wm_ref, wa_ref, wo_ref, o_ref):
    mixed = (jax.nn.sigmoid(ga_ref[0]) * _dot(hm_ref[0], wm_ref[...]) +
             jax.nn.sigmoid(gb_ref[0]) * _dot(ha_ref[0], wa_ref[...]))
    o_ref[0] = x_ref[0] + gt_ref[0] * _rms(_dot(mixed, wo_ref[...]), g_ref[...])


def mix_out(x, hm, ha, z, gt, g_post, wm, wa, wo, tm):
    bsz, t, d = x.shape
    r = gt.shape[1]
    rb = 1 if r == 1 else tm
    row = lambda col=0: pl.BlockSpec((1, tm, d), lambda b_, i: (b_, i, col // d))
    mod = pl.BlockSpec((1, rb, d), (lambda b_, i: (b_, 0, 0)) if r == 1 else (lambda b_, i: (b_, i, 0)))
    wsp = pl.BlockSpec((d, d), lambda b_, i: (0, 0))
    return pl.pallas_call(
        _mix_kernel,
        out_shape=jax.ShapeDtypeStruct((bsz, t, d), F32),
        grid=(bsz, t // tm),
        in_specs=[row(), row(), row(), row(Z_GA), row(Z_GB), mod,
                  pl.BlockSpec((1, d), lambda b_, i: (0, 0)), wsp, wsp, wsp],
        out_specs=row(),
        compiler_params=_cparams(("arbitrary", "arbitrary")),
        name="mix",
    )(x, hm, ha, z, z, gt, g_post.reshape(1, d), wm, wa, wo)


FF_CHUNK = 256


def _ffn_kernel(x_ref, g_ref, sc_ref, sh_ref, gt_ref, gp_ref, wa_ref, wg_ref, cwa_ref, cwg_ref, cba_ref, cbg_ref,
                wd_ref, s1a_ref, s1g_ref, s2a_ref, s2g_ref, y_ref, ua_ref, ug_ref, h_ref, acc_ref):
    f = pl.program_id(2)
    tm = x_ref.shape[1]

    @pl.when(f == 0)
    def _():
        h = _rms(x_ref[0], g_ref[...]) * (1.0 + sc_ref[0]) + sh_ref[0]
        h_ref[...] = h.astype(BF16)
        acc_ref[...] = jnp.zeros_like(acc_ref)

    t = lax.broadcasted_iota(jnp.int32, (tm, 1), 0) % SAMPLE_ROWS

    def branch(w_ref, cw_ref, cb_ref, s1_ref, s2_ref):
        u = jnp.dot(h_ref[...], w_ref[...], preferred_element_type=F32)
        u1 = jnp.where(t < 1, s1_ref[0], pltpu.roll(u, 1, 0))
        u2 = jnp.where(t < 2, s2_ref[0], pltpu.roll(u, 2, 0))
        cw = cw_ref[...]
        return u, cb_ref[...] + cw[0:1] * u2 + cw[1:2] * u1 + cw[2:3] * u

    ua_ref[0], conv_a = branch(wa_ref, cwa_ref, cba_ref, s1a_ref, s2a_ref)
    ug_ref[0], conv_g = branch(wg_ref, cwg_ref, cbg_ref, s1g_ref, s2g_ref)
    acc_ref[...] += _dot(jax.nn.gelu(conv_g) * conv_a, wd_ref[...])

    @pl.when(f == pl.num_programs(2) - 1)
    def _():
        y_ref[0] = x_ref[0] + gt_ref[0] * _rms(acc_ref[...], gp_ref[...])


def conv_ffn(x, g_pre, sc, sh, gt, g_post, w_up, conv_w, conv_b, w_down, tm, state_rows):
    bsz, t, d = x.shape
    ck = FF_CHUNK
    nf = D_FF // ck
    xrow = pl.BlockSpec((1, tm, d), lambda b_, i, f: (b_, i, 0))
    vec = pl.BlockSpec((1, d), lambda b_, i, f: (0, 0))
    col_a = lambda rows: pl.BlockSpec((rows, ck), lambda b_, i, f: (0, f))
    col_g = lambda rows: pl.BlockSpec((rows, ck), lambda b_, i, f: (0, nf + f))
    st_a = pl.BlockSpec((1, tm, ck), lambda b_, i, f: (b_, i, f))
    st_g = pl.BlockSpec((1, tm, ck), lambda b_, i, f: (b_, i, nf + f))
    cb = conv_b.reshape(1, 2 * D_FF)
    y, ua, ug = pl.pallas_call(
        _ffn_kernel,
        out_shape=(jax.ShapeDtypeStruct((bsz, t, d), F32),) + (jax.ShapeDtypeStruct((bsz, t, D_FF), F32),) * 2,
        grid=(bsz, t // tm, nf),
        in_specs=[xrow, vec, xrow, xrow, xrow, vec, col_a(d), col_g(d), col_a(CONV_W), col_g(CONV_W), col_a(1), col_g(1),
                  pl.BlockSpec((ck, d), lambda b_, i, f: (f, 0)), st_a, st_g, st_a, st_g],
        out_specs=(xrow, st_a, st_a),
        scratch_shapes=[pltpu.VMEM((tm, d), BF16), pltpu.VMEM((tm, d), F32)],
        compiler_params=_cparams(("arbitrary", "arbitrary", "arbitrary")),
        name="ffn",
    )(x, g_pre.reshape(1, d), sc, sh, gt, g_post.reshape(1, d), w_up, w_up, conv_w, conv_w, cb, cb, w_down,
      state_rows[0], state_rows[0], state_rows[1], state_rows[1])
    return y, jnp.concatenate([ua, ug], axis=-1)


def _ffn_rows_kernel(x_ref, g_ref, sc_ref, sh_ref, gt_ref, gp_ref, wu_ref, cw_ref, cb_ref, wd_ref,
                     y_ref, tail_ref, carry_ref):
    tm = x_ref.shape[1]
    ck = FF_CHUNK

    @pl.when(pl.program_id(1) == 0)
    def _():
        carry_ref[...] = jnp.zeros_like(carry_ref)

    h = (_rms(x_ref[0], g_ref[...]) * (1.0 + sc_ref[0]) + sh_ref[0]).astype(BF16)
    top = lax.broadcasted_iota(jnp.int32, (8, 1), 0)
    acc = jnp.zeros((tm, x_ref.shape[2]), F32)

    def conv(cols):
        u = jnp.dot(h, wu_ref[:, cols], preferred_element_type=F32)
        r1 = pltpu.roll(u, 1, 0)
        r2 = pltpu.roll(u, 2, 0)
        prev = carry_ref[:, cols]
        u1 = jnp.concatenate([jnp.where(top < 1, prev[1:2], r1[0:8]), r1[8:]], axis=0)
        u2 = jnp.concatenate([jnp.where(top < 1, prev[0:1], jnp.where(top < 2, prev[1:2], r2[0:8])), r2[8:]], axis=0)
        carry_ref[0:2, cols] = u[tm - 2:tm]
        tail_ref[0, 0, :, cols] = u[tm - 2:tm]
        cw = cw_ref[:, cols]
        return cb_ref[:, cols] + cw[0:1] * u2 + cw[1:2] * u1 + cw[2:3] * u

    for c in range(D_FF // ck):
        conv_a = conv(slice(c * ck, (c + 1) * ck))
        conv_g = conv(slice(D_FF + c * ck, D_FF + (c + 1) * ck))
        acc = acc + _dot(jax.nn.gelu(conv_g) * conv_a, wd_ref[c * ck:(c + 1) * ck, :])
    y_ref[0] = x_ref[0] + gt_ref[0] * _rms(acc, gp_ref[...])


def conv_ffn_rows(x, g_pre, sc, sh, gt, g_post, w_up, conv_w, conv_b, w_down, tm):
    bsz, t, d = x.shape
    mod = pl.BlockSpec((1, 1, d), lambda b_, i: (b_, 0, 0))
    xrow = pl.BlockSpec((1, tm, d), lambda b_, i: (b_, i, 0))
    whole = lambda a: pl.BlockSpec(a.shape, lambda b_, i: (0,) * a.ndim, pipeline_mode=pl.Buffered(1))
    cb = conv_b.reshape(1, 2 * D_FF)
    g1, g2 = g_pre.reshape(1, d), g_post.reshape(1, d)
    y, tail = pl.pallas_call(
        _ffn_rows_kernel,
        out_shape=(jax.ShapeDtypeStruct((bsz, t, d), F32),
                   jax.ShapeDtypeStruct((bsz, t // tm, CONV_W - 1, 2 * D_FF), F32)),
        grid=(bsz, t // tm),
        in_specs=[xrow, whole(g1), mod, mod, mod, whole(g2), whole(w_up), whole(conv_w), whole(cb), whole(w_down)],
        out_specs=(xrow, pl.BlockSpec((1, 1, CONV_W - 1, 2 * D_FF), lambda b_, i: (b_, i, 0, 0))),
        scratch_shapes=[pltpu.VMEM((8, 2 * D_FF), F32)],
        compiler_params=_cparams(("arbitrary", "arbitrary")),
        name="ffn_rows",
    )(x, g1, sc, sh, gt, g2, w_up, conv_w, cb, w_down)
    return y, tail[:, -1]


LANE = 128
QUARTERS = 2 * KV_WIDTH // LANE
HEADS_PER_LANE_ROW = LANE // A_HEAD_DIM


def compress_weights(cmp_w1, cmp_pe, cmp_w2):
    eye = jnp.eye(HEADS_PER_LANE_ROW, dtype=F32)
    bd = lambda w: jnp.einsum("gh,...de->...gdhe", eye, w).reshape(w.shape[:-2] + (LANE, LANE))
    w1ab = jnp.concatenate([bd(cmp_w1[:, :CMP_STRIDE]), bd(cmp_w1[:, CMP_STRIDE:])], axis=-1).astype(BF16)
    w1ab = w1ab.reshape(2, CMP_STRIDE // 2, 2 * LANE, 2 * LANE)
    w2 = bd(cmp_w2).astype(BF16)
    w1r = cmp_w1.reshape(2, CMP_BLOCK * A_HEAD_DIM, A_HEAD_DIM)
    pe = cmp_pe.reshape(2, CMP_BLOCK * A_HEAD_DIM, 1)
    return w1ab, w2, w1r, pe


def _compress_chunk(get_x, nrows, w1_ref, w2_ref, w1r_ref, pe_ref, carry_ref, out_ref):
    row = lax.broadcasted_iota(jnp.int32, (nrows, 1), 0)
    for kind in range(2):
        peb = jnp.sum(pe_ref[kind] * w1r_ref[kind], axis=0, keepdims=True)
        peb = jnp.concatenate([peb] * HEADS_PER_LANE_ROW, axis=1)
        for half in range(2):
            acc = jnp.zeros((nrows, 2 * LANE), F32)
            for j in range(CMP_STRIDE // 2):
                x = get_x(2 * kind + half, j).astype(BF16)
                acc = acc + jnp.dot(x, w1_ref[kind, j], preferred_element_type=F32)
            acc_a, acc_b = acc[:, :LANE], acc[:, LANE:]
            a_shift = jnp.where(row == 0, carry_ref[kind, half], pltpu.roll(acc_a, 1, 0))
            carry_ref[kind, half] = acc_a[nrows - 1:nrows]
            hid = jax.nn.gelu(a_shift + acc_b + peb)
            out_ref[0, kind, :, half * LANE:(half + 1) * LANE] = _dot(hid, w2_ref[kind])


def _cmp_prompt_kernel(kv0_ref, kv1_ref, kv2_ref, kv3_ref, w1_ref, w2_ref, w1r_ref, pe_ref, out_ref, carry_ref):
    kv_refs = (kv0_ref, kv1_ref, kv2_ref, kv3_ref)
    nseg = kv0_ref.shape[1] // CMP_STRIDE
    carry_ref[...] = jnp.zeros_like(carry_ref)
    offset = lambda quarter, s: kv_refs[quarter][0, pl.ds(s, nseg, stride=CMP_STRIDE), :]
    get_x = lambda quarter, j: jnp.concatenate([offset(quarter, 2 * j), offset(quarter, 2 * j + 1)], axis=1)
    _compress_chunk(get_x, nseg, w1_ref, w2_ref, w1r_ref, pe_ref, carry_ref, out_ref)


def _cmp_weight_specs():
    zero = lambda n: (lambda *_: (0,) * n)
    return [pl.BlockSpec((2, CMP_STRIDE // 2, 2 * LANE, 2 * LANE), zero(4)),
            pl.BlockSpec((2, LANE, LANE), zero(3)),
            pl.BlockSpec((2, CMP_BLOCK * A_HEAD_DIM, A_HEAD_DIM), zero(3)),
            pl.BlockSpec((2, CMP_BLOCK * A_HEAD_DIM, 1), zero(3))]


def compress_prompt(kv_cmp, cw):
    bsz, t, w = kv_cmp.shape
    nseg = t // CMP_STRIDE
    return pl.pallas_call(
        _cmp_prompt_kernel,
        out_shape=jax.ShapeDtypeStruct((bsz, 2, nseg, KV_WIDTH), F32),
        grid=(bsz,),
        in_specs=[pl.BlockSpec((1, t, LANE), functools.partial(lambda q, b_: (b_, 0, q), q)) for q in range(QUARTERS)]
                 + _cmp_weight_specs(),
        out_specs=pl.BlockSpec((1, 2, nseg, KV_WIDTH), lambda b_: (b_, 0, 0, 0)),
        scratch_shapes=[pltpu.VMEM((2, 2, 1, LANE), F32)],
        compiler_params=_cparams(("arbitrary",)),
        name="cmp_prompt",
    )(*([kv_cmp] * QUARTERS), *cw)


def feature_major_pool(cache):
    n_pool = cache.shape[0]
    return jnp.transpose(cache, (0, 2, 3, 4, 1)).reshape(n_pool, 2 * KV_WIDTH, PAGE_SIZE)


def _page_copy(pool_ref, dst, sem_ref, pt_ref, step, slot, k, n_chunks, pages):
    b_ = step // n_chunks
    c = step % n_chunks
    pid = pt_ref[b_, c * pages + k]
    return pltpu.make_async_copy(pool_ref.at[pid], dst(slot, k), sem_ref.at[slot])


def _page_pipeline(pool_ref, dst, sem_ref, pt_ref, n_chunks, pages):
    step = pl.program_id(0) * n_chunks + pl.program_id(1)
    total = pl.num_programs(0) * n_chunks
    slot = step % 2

    def start(st, sl):
        for k in range(pages):
            _page_copy(pool_ref, dst, sem_ref, pt_ref, st, sl, k, n_chunks, pages).start()

    @pl.when(step == 0)
    def _():
        start(step, slot)

    @pl.when(step + 1 < total)
    def _():
        start(step + 1, 1 - slot)

    for k in range(pages):
        _page_copy(pool_ref, dst, sem_ref, pt_ref, step, slot, k, n_chunks, pages).wait()
    return slot


def _cmp_sample_kernel(pt_ref, pool_ref, w1_ref, w2_ref, w1r_ref, pe_ref, out_ref,
                       buf_ref, sem_ref, x_ref, carry_ref, *, n_chunks, pages):
    slot = _page_pipeline(pool_ref, lambda sl, k: buf_ref.at[sl, k], sem_ref, pt_ref, n_chunks, pages)

    @pl.when(pl.program_id(1) == 0)
    def _():
        carry_ref[...] = jnp.zeros_like(carry_ref)

    segs = PAGE_SIZE // CMP_STRIDE
    nrows = pages * segs
    half_rows = PAGE_SIZE // 2
    dst = lax.broadcasted_iota(jnp.int32, (PAGE_SIZE, PAGE_SIZE), 0)
    src = lax.broadcasted_iota(jnp.int32, (PAGE_SIZE, PAGE_SIZE), 1)
    wanted = (dst % segs) * CMP_STRIDE + 2 * ((dst % half_rows) // segs) + dst // half_rows
    pick = jnp.where(src == wanted, 1.0, 0.0).astype(BF16)

    def relayout(p, carry):
        for quarter in range(QUARTERS):
            t = buf_ref[slot, p, quarter * LANE:(quarter + 1) * LANE, :]
            y = _dot_nt(pick, t)
            x_ref[quarter, p] = jnp.concatenate([y[:half_rows], y[half_rows:]], axis=1)
        return carry

    lax.fori_loop(0, pages, relayout, 0, unroll=8)

    def get_x(quarter, j):
        return x_ref[quarter, :, j * segs:(j + 1) * segs, :].reshape(nrows, 2 * LANE)

    _compress_chunk(get_x, nrows, w1_ref, w2_ref, w1r_ref, pe_ref, carry_ref, out_ref)


def _page_chunk(n_pages):
    return math.gcd(n_pages, 32)


def compress_paged(pool, page_table, cw):
    dbs, n_pages = page_table.shape
    pages = _page_chunk(n_pages)
    n_chunks = n_pages // pages
    rows = pages * PAGE_SIZE // CMP_STRIDE
    return pl.pallas_call(
        functools.partial(_cmp_sample_kernel, n_chunks=n_chunks, pages=pages),
        out_shape=jax.ShapeDtypeStruct((dbs, 2, n_chunks * rows, KV_WIDTH), F32),
        grid_spec=pltpu.PrefetchScalarGridSpec(
            num_scalar_prefetch=1,
            grid=(dbs, n_chunks),
            in_specs=[pl.BlockSpec(memory_space=pl.ANY)] + _cmp_weight_specs(),
            out_specs=pl.BlockSpec((1, 2, rows, KV_WIDTH), lambda b_, c, pt: (b_, 0, c, 0)),
            scratch_shapes=[pltpu.VMEM((2, pages, 2 * KV_WIDTH, PAGE_SIZE), F32),
                            pltpu.SemaphoreType.DMA((2,)),
                            pltpu.VMEM((QUARTERS, pages, PAGE_SIZE // 2, 2 * LANE), F32),
                            pltpu.VMEM((2, 2, 1, LANE), F32)]),
        compiler_params=_cparams(("arbitrary", "arbitrary")),
        name="cmp_paged",
    )(page_table, pool, *cw)


def _masked_softmax_rows(s, mask):
    s = jnp.where(mask, s, NEG_INF)
    m = jnp.max(s, axis=-1, keepdims=True)
    m = jnp.where(m == NEG_INF, 0.0, m)
    e = jnp.exp(s - m)
    return e / jnp.maximum(jnp.sum(e, axis=-1, keepdims=True), TINY)


def _topk_mask(score, k, axis):
    n = score.shape[axis]
    idx = lax.broadcasted_iota(jnp.int32, score.shape, axis)
    sel = jnp.zeros(score.shape, F32)
    for _ in range(k):
        mx = jnp.max(score, axis=axis, keepdims=True)
        first = jnp.min(jnp.where(score == mx, idx, n), axis=axis, keepdims=True)
        pick = idx == first
        sel = jnp.where(pick, 1.0, sel)
        score = jnp.where(pick, NEG_INF, score)
    return sel


def _flash_tile(carry, s, v):
    m_old, l_old, acc = carry
    m_new = jnp.maximum(m_old, jnp.max(s, axis=-1, keepdims=True))
    m_safe = jnp.where(m_new == NEG_INF, 0.0, m_new)
    p = jnp.exp(s - m_safe)
    alpha = jnp.exp(m_old - m_safe)
    pv = v(p.astype(BF16)) if callable(v) else _dot(p, v)
    return m_new, alpha * l_old + jnp.sum(p, axis=-1, keepdims=True), alpha * acc + pv


MASKED = -(2.0 ** 100)
LOG2E = math.log2(math.e)


ONES_ROWS = 16
V_ROWS = A_HEAD_DIM + ONES_ROWS


def _flash_cols(carry, s, v_t):
    m_old, acc = carry
    m_new = jnp.maximum(m_old, jnp.max(s, axis=0, keepdims=True))
    return m_new, jnp.exp2(m_old - m_new) * acc + _dot(v_t, jnp.exp2(s - m_new))


def _flash_cols_init(cols):
    return jnp.full((1, cols), MASKED, F32), jnp.zeros((V_ROWS, cols), F32)


def _flash_cols_out(carry):
    acc = carry[1]
    return acc[:A_HEAD_DIM] / jnp.maximum(acc[A_HEAD_DIM:A_HEAD_DIM + 1], TINY)


def _flash_init(rows, dv):
    return jnp.full((rows, 1), NEG_INF, F32), jnp.zeros((rows, 1), F32), jnp.zeros((rows, dv), F32)


def _flash_out(carry):
    _, l, acc = carry
    return acc / jnp.maximum(l, TINY)


KEY_TILE = 512


def _nsa_prompt_kernel(q_ref, small_ref, cmp_ref, ks_ref, vs_ref, kw_ref, vw_ref, cov_ref, exp_ref, o_ref,
                       *, tq, n_slc):
    i = pl.program_id(1)
    q0 = i * tq
    tk = math.gcd(ks_ref.shape[1], KEY_TILE)
    cols = A_GROUP * tq
    q_t = (q_ref[0] * (A_HEAD_DIM ** -0.5 * LOG2E)).T.astype(BF16)
    gate_t = jax.nn.sigmoid(small_ref[0]).T
    ncmp = cmp_ref.shape[2]
    qpos = q0 + lax.broadcasted_iota(jnp.int32, (1, tq), 1)
    jcol = lax.broadcasted_iota(jnp.int32, (ncmp, 1), 0)
    cmp_mask = (jcol >= 1) & (jcol * CMP_STRIDE + (CMP_BLOCK - CMP_STRIDE - 1) <= qpos)
    blk = lax.broadcasted_iota(jnp.int32, (n_slc, 1), 0)
    cur = qpos // SLC_BLOCK
    forced = (blk == 0) | (blk == cur) | (blk == cur - 1)
    future = blk * SLC_BLOCK > qpos
    krow = lax.broadcasted_iota(jnp.int32, (tk, 1), 0)
    lanes = lambda a, r: a[:, r * tq:(r + 1) * tq]
    wk = min(WINDOW + tq, kw_ref.shape[1])
    w_off = pl.multiple_of(jnp.maximum(q0 - WINDOW, 0), tq)
    wpos = w_off + lax.broadcasted_iota(jnp.int32, (wk, 1), 0)
    band = (wpos <= qpos) & (wpos > qpos - WINDOW)
    pieces = []

    for g in range(A_KV_HEADS):
        gs = slice(g * A_HEAD_DIM, (g + 1) * A_HEAD_DIM)
        gv = slice(g * V_ROWS, (g + 1) * V_ROWS)
        head = lambda r: slice((g * A_GROUP + r) * A_HEAD_DIM, (g * A_GROUP + r + 1) * A_HEAD_DIM)
        qg = jnp.concatenate([q_t[head(r)] for r in range(A_GROUP)], axis=1)

        s = jnp.dot(kw_ref[0, pl.ds(w_off, wk), gs], qg, preferred_element_type=F32)
        s = jnp.concatenate([jnp.where(band, lanes(s, r), MASKED) for r in range(A_GROUP)], axis=1)
        o_w = _flash_cols_out(_flash_cols(_flash_cols_init(cols), s, vw_ref[0, gv, pl.ds(w_off, wk)]))

        s_c = _dot(cmp_ref[0, 0, :, gs], qg)
        p_r = []
        for r in range(A_GROUP):
            s_r = jnp.where(cmp_mask, lanes(s_c, r), NEG_INF)
            m = jnp.max(s_r, axis=0, keepdims=True)
            e = jnp.exp2(s_r - jnp.where(m == NEG_INF, 0.0, m))
            p_r.append(e / jnp.maximum(jnp.sum(e, axis=0, keepdims=True), TINY))
        o_c = _dot_tn(cmp_ref[0, 1, :, gs], jnp.concatenate(p_r, axis=1))
        imp_t = _dot(cov_ref[...], p_r[0] + p_r[1] + p_r[2] + p_r[3])
        score = jnp.where(future, -BIG, imp_t + jnp.where(forced, BIG, 0.0))
        sel_t = _topk_mask(score, min(SLC_TOPK, n_slc), 0)
        sel_bias = jnp.concatenate([jnp.where(sel_t > 0.5, 0.0, MASKED),
                                    jnp.zeros((exp_ref.shape[1] - n_slc, tq), F32)], axis=0)
        sel_bias = jnp.concatenate([sel_bias] * A_GROUP, axis=1).astype(BF16)

        def slc_scores(off):
            return (jnp.dot(ks_ref[0, pl.ds(off, tk), gs], qg, preferred_element_type=F32) +
                    jnp.dot(exp_ref[pl.ds(off, tk), :], sel_bias, preferred_element_type=F32))

        def slc_step(j, carry):
            off = pl.multiple_of(j * tk, tk)
            return _flash_cols(carry, slc_scores(off), vs_ref[0, gv, pl.ds(off, tk)])

        j_last = (q0 + tq - 1) // tk
        carry = lax.fori_loop(0, j_last, slc_step, _flash_cols_init(cols))
        off = pl.multiple_of(j_last * tk, tk)
        causal = off + krow <= qpos
        s = slc_scores(off)
        s = jnp.concatenate([jnp.where(causal, lanes(s, r), MASKED) for r in range(A_GROUP)], axis=1)
        o_s = _flash_cols_out(_flash_cols(carry, s, vs_ref[0, gv, pl.ds(off, tk)]))

        for r in range(A_GROUP):
            row = SMALL_AG + 3 * (g * A_GROUP + r)
            pieces.append(gate_t[row:row + 1] * lanes(o_c, r) + gate_t[row + 1:row + 2] * lanes(o_s, r) +
                          gate_t[row + 2:row + 3] * lanes(o_w, r))
    o_ref[0] = jnp.concatenate(pieces, axis=0).T


def _coverage(n_cmp_rows, n_slc):
    cs = (np.arange(n_cmp_rows) - 1) * CMP_STRIDE
    ss = np.arange(n_slc) * SLC_BLOCK
    lo = np.maximum(cs[:, None], ss[None, :])
    hi = np.minimum(cs[:, None] + CMP_BLOCK, ss[None, :] + SLC_BLOCK)
    cov = np.clip(hi - lo, 0, None) / CMP_BLOCK
    cov[0] = 0.0
    return cov.astype(np.float32)


def _block_expand(n_rows, n_keys):
    return (np.arange(n_rows)[:, None] == (np.arange(n_keys) // SLC_BLOCK)[None, :]).astype(np.float32)


def nsa_prompt(q_rot, z, cmp, kv_slc, kv_win, tq):
    bsz, t, _ = q_rot.shape
    n_slc = t // SLC_BLOCK
    ncmp = cmp.shape[2]
    kk = lambda kv: kv[..., :KV_WIDTH].astype(BF16)
    def vt(kv):
        v = jnp.swapaxes(kv[..., KV_WIDTH:], 1, 2).astype(BF16).reshape(bsz, A_KV_HEADS, A_HEAD_DIM, t)
        ones = jnp.ones((bsz, A_KV_HEADS, ONES_ROWS, t), BF16)
        return jnp.concatenate([v, ones], axis=2).reshape(bsz, A_KV_HEADS * V_ROWS, t)

    cov_t = jnp.asarray(_coverage(ncmp, n_slc).T, BF16)
    n_exp = -(-n_slc // 128) * 128
    expand = jnp.asarray(_block_expand(n_exp, t).T, BF16)
    per_b = lambda shape: pl.BlockSpec((1,) + shape, lambda b_, i: (b_,) + (0,) * len(shape))
    return pl.pallas_call(
        functools.partial(_nsa_prompt_kernel, tq=tq, n_slc=n_slc),
        out_shape=jax.ShapeDtypeStruct((bsz, t, D_MODEL), F32),
        grid=(bsz, t // tq),
        in_specs=[pl.BlockSpec((1, tq, D_MODEL), lambda b_, i: (b_, i, 0)),
                  pl.BlockSpec((1, tq, 128), lambda b_, i: (b_, i, Z_SMALL // 128)),
                  per_b((2, ncmp, KV_WIDTH)),
                  per_b((t, KV_WIDTH)), per_b((A_KV_HEADS * V_ROWS, t)),
                  per_b((t, KV_WIDTH)), per_b((A_KV_HEADS * V_ROWS, t)),
                  pl.BlockSpec((n_slc, ncmp), lambda b_, i: (0, 0)),
                  pl.BlockSpec((t, n_exp), lambda b_, i: (0, 0))],
        out_specs=pl.BlockSpec((1, tq, D_MODEL), lambda b_, i: (b_, i, 0)),
        compiler_params=_cparams(("arbitrary", "arbitrary")),
        name="nsa_prompt",
    )(q_rot, z, cmp, kk(kv_slc), vt(kv_slc), kk(kv_win), vt(kv_win), cov_t, expand)


SAMPLE_ROWS = 8
NEW_KEYS = 128


def _nsa_sample_kernel(pt_ref, qbd_ref, gl_ref, cmp_ref, pool_ref, knew_ref, wcache_ref, wnew_ref, cov_ref, exp_ref,
                       o_ref, buf_ref, sem_ref, sel_ref, m_ref, l_ref, acc_ref, oc_ref,
                       *, n_chunks, pages, past, t_real, n_slc):
    c = pl.program_id(1)
    slot = _page_pipeline(pool_ref, lambda sl, k: buf_ref.at[sl, k], sem_ref, pt_ref, n_chunks, pages)
    qbd = qbd_ref[0]
    rows = qbd.shape[0]
    bpc = pages * PAGE_SIZE // SLC_BLOCK
    rq = lax.broadcasted_iota(jnp.int32, (rows, 1), 0) % t_real
    qpos = past + rq

    @pl.when(c == 0)
    def _():
        ncmp = cmp_ref.shape[2]
        nbp = cov_ref.shape[1]
        jrow = lax.broadcasted_iota(jnp.int32, (1, ncmp), 1)
        cmp_mask = (jrow >= 1) & (jrow * CMP_STRIDE + (CMP_BLOCK - CMP_STRIDE - 1) <= qpos)
        p = _masked_softmax_rows(_dot_nt(qbd, cmp_ref[0, 0]), cmp_mask)
        oc_ref[...] = _dot(p, cmp_ref[0, 1])
        ri = lax.broadcasted_iota(jnp.int32, (rows, rows), 0)
        ci = lax.broadcasted_iota(jnp.int32, (rows, rows), 1)
        group_rows = A_GROUP * t_real
        same = (ri // group_rows == ci // group_rows) & (ri % t_real == ci % t_real)
        p_group = _dot(jnp.where(same, 1.0, 0.0), p)
        imp = _dot(p_group, cov_ref[...])
        blk = lax.broadcasted_iota(jnp.int32, (1, nbp), 1)
        cur = qpos // SLC_BLOCK
        forced = (blk == 0) | (blk == cur) | (blk == cur - 1)
        future = blk * SLC_BLOCK > qpos
        score = jnp.where(future, -BIG, imp + jnp.where(forced, BIG, 0.0))
        score = jnp.where(blk < n_slc, score, NEG_INF)
        sel = _topk_mask(score, min(SLC_TOPK, n_slc), 1)
        pad = jnp.zeros((rows, 128 - bpc), F32)
        for cc in range(n_chunks + 1):
            sel_ref[cc] = jnp.concatenate([sel[:, cc * bpc:(cc + 1) * bpc], pad], axis=1).astype(BF16)
        m_ref[...] = jnp.full(m_ref.shape, NEG_INF, F32)
        l_ref[...] = jnp.zeros_like(l_ref)
        acc_ref[...] = jnp.zeros_like(acc_ref)

    scores = jnp.concatenate([_dot(qbd, buf_ref[slot, k, 0:KV_WIDTH, :]) for k in range(pages)], axis=1)
    picked = jnp.dot(sel_ref[c], exp_ref[...], preferred_element_type=F32) > 0.5
    s = jnp.where(picked, scores, NEG_INF)

    def values(p):
        return sum(_dot_nt(p[:, k * PAGE_SIZE:(k + 1) * PAGE_SIZE], buf_ref[slot, k, KV_WIDTH:2 * KV_WIDTH, :])
                   for k in range(pages))

    carry = _flash_tile((m_ref[...], l_ref[...], acc_ref[...]), s, values)
    m_ref[...], l_ref[...], acc_ref[...] = carry

    @pl.when(c == n_chunks - 1)
    def _():
        zpad = jnp.zeros((NEW_KEYS - SAMPLE_ROWS, 2 * KV_WIDTH), F32)
        kcol = lax.broadcasted_iota(jnp.int32, (1, NEW_KEYS), 1)
        new_mask = (kcol <= rq) & (kcol < t_real)
        knew = jnp.concatenate([knew_ref[0], zpad], axis=0)
        last_picked = sel_ref[n_chunks][:, 0:1].astype(F32) > 0.5
        s_new = jnp.where(new_mask & last_picked, _dot_nt(qbd, knew[:, :KV_WIDTH]), NEG_INF)
        o_s = _flash_out(_flash_tile((m_ref[...], l_ref[...], acc_ref[...]), s_new, knew[:, KV_WIDTH:]))

        wc = wcache_ref[0]
        wb = wc.shape[0]
        wcol = lax.broadcasted_iota(jnp.int32, (1, wb), 1)
        s_w = jnp.where(wcol > rq + (wb - WINDOW), _dot_nt(qbd, wc[:, :KV_WIDTH]), NEG_INF)
        cw = _flash_tile(_flash_init(rows, KV_WIDTH), s_w, wc[:, KV_WIDTH:])
        wnew = jnp.concatenate([wnew_ref[0], zpad], axis=0)
        s_wn = jnp.where(new_mask, _dot_nt(qbd, wnew[:, :KV_WIDTH]), NEG_INF)
        o_w = _flash_out(_flash_tile(cw, s_wn, wnew[:, KV_WIDTH:]))

        gate = jax.nn.sigmoid(gl_ref[0])
        o = gate[:, 0:1] * oc_ref[...] + gate[:, 1:2] * o_s + gate[:, 2:3] * o_w
        lane_g = lax.broadcasted_iota(jnp.int32, (1, KV_WIDTH), 1) // A_HEAD_DIM
        row_g = lax.broadcasted_iota(jnp.int32, (rows, 1), 0) // (A_GROUP * t_real)
        o = jnp.where(lane_g == row_g, o, 0.0)
        o_ref[0] = sum(o[:, g * A_HEAD_DIM:(g + 1) * A_HEAD_DIM] for g in range(A_KV_HEADS))


def nsa_sample(q_rot, z, cmp, pool, page_table, k_new, win_cache, w_new, past, t_real):
    dbs = q_rot.shape[0]
    n_pages = page_table.shape[1]
    pages = _page_chunk(n_pages)
    n_chunks = n_pages // pages
    bpc = pages * PAGE_SIZE // SLC_BLOCK
    ncmp = cmp.shape[2]
    n_slc = -(-(past + t_real) // SLC_BLOCK)
    assert n_slc == n_chunks * bpc + 1 and bpc <= 128
    nbp = -(-((n_chunks + 1) * bpc) // 128) * 128
    rows = A_HEADS * t_real
    q5 = q_rot[:, :t_real].reshape(dbs, t_real, A_KV_HEADS, A_GROUP, A_HEAD_DIM) * (A_HEAD_DIM ** -0.5)
    qbd = jnp.einsum("bqgrd,gh->bgrqhd", q5, jnp.eye(A_KV_HEADS, dtype=F32)).reshape(dbs, rows, KV_WIDTH).astype(BF16)
    gl = z[:, :t_real, Z_SMALL + SMALL_AG:Z_SMALL + SMALL_AG + 3 * A_HEADS].reshape(dbs, t_real, A_HEADS, 3)
    gl = jnp.swapaxes(gl, 1, 2).reshape(dbs, rows, 3)
    cov = np.zeros((ncmp, nbp), np.float32)
    cov[:, :n_slc] = _coverage(ncmp, n_slc)
    expand = jnp.asarray(_block_expand(128, pages * PAGE_SIZE), BF16)
    per_b = lambda shape: pl.BlockSpec((1,) + shape, lambda b_, c, pt: (b_,) + (0,) * len(shape))
    const = lambda shape: pl.BlockSpec(shape, lambda b_, c, pt: (0,) * len(shape))
    wb = win_cache.shape[1]
    out = pl.pallas_call(
        functools.partial(_nsa_sample_kernel, n_chunks=n_chunks, pages=pages, past=past, t_real=t_real, n_slc=n_slc),
        out_shape=jax.ShapeDtypeStruct((dbs, rows, A_HEAD_DIM), F32),
        grid_spec=pltpu.PrefetchScalarGridSpec(
            num_scalar_prefetch=1,
            grid=(dbs, n_chunks),
            in_specs=[per_b((rows, KV_WIDTH)), per_b((rows, 3)), per_b((2, ncmp, KV_WIDTH)),
                      pl.BlockSpec(memory_space=pl.ANY),
                      per_b((SAMPLE_ROWS, 2 * KV_WIDTH)), per_b((wb, 2 * KV_WIDTH)), per_b((SAMPLE_ROWS, 2 * KV_WIDTH)),
                      const((ncmp, nbp)), const((128, pages * PAGE_SIZE))],
            out_specs=per_b((rows, A_HEAD_DIM)),
            scratch_shapes=[pltpu.VMEM((2, pages, 2 * KV_WIDTH, PAGE_SIZE), F32),
                            pltpu.SemaphoreType.DMA((2,)),
                            pltpu.VMEM((n_chunks + 1, rows, 128), BF16),
                            pltpu.VMEM((rows, 1), F32), pltpu.VMEM((rows, 1), F32),
                            pltpu.VMEM((rows, KV_WIDTH), F32), pltpu.VMEM((rows, KV_WIDTH), F32)]),
        compiler_params=_cparams(("arbitrary", "arbitrary")),
        name="nsa_sample",
    )(page_table, qbd, gl, cmp, pool, k_new, win_cache, w_new, jnp.asarray(cov, BF16), expand)
    out = jnp.swapaxes(out.reshape(dbs, A_HEADS, t_real, A_HEAD_DIM), 1, 2).reshape(dbs, t_real, D_MODEL)
    return jnp.pad(out, ((0, 0), (0, SAMPLE_ROWS - t_real), (0, 0)))


def _kv_rows(a, bsz, t):
    return a.reshape(bsz, t, 2, A_KV_HEADS, A_HEAD_DIM)


def kernel(x_prompt, x_sample, cache_cmp_kv, cache_slc_kv, cache_win_kv, state_C, state_n, state_m, state_conv,
           page_table, c_prompt, c_sample, w_ada, b_ada, g_pre_mix, g_post_mix, g_pre_ffn, g_post_ffn, w_in, b_in,
           m_norm_w, cmp_w1, cmp_pe, cmp_w2, w_branch_m, w_branch_a, w_out, w_up, conv_w, conv_b, w_down):
    depth = w_ada.shape[0]
    bsz, t, d = x_prompt.shape
    dbs, ts, _ = x_sample.shape
    n_pages = page_table.shape[1]
    past = n_pages * PAGE_SIZE
    assert ts <= SAMPLE_ROWS and (past + ts) // CMP_STRIDE == past // CMP_STRIDE and past >= WINDOW
    srows = dbs * SAMPLE_ROWS
    xp = x_prompt.astype(F32)
    xs = jnp.pad(x_sample.astype(F32), ((0, 0), (0, SAMPLE_ROWS - ts), (0, 0))).reshape(1, srows, d)
    c_all = jnp.concatenate([c_prompt, c_sample], axis=0).astype(F32)
    c_all = jnp.pad(c_all, ((0, (-c_all.shape[0]) % 8), (0, 0)))
    tab_p = rope_tables(jnp.arange(t, dtype=jnp.int32))
    tab_s = rope_tables(jnp.tile(past + jnp.arange(SAMPLE_ROWS, dtype=jnp.int32), dbs))
    lchunk = math.gcd(t, 256)
    tm_p = math.gcd(t, 512)
    p_states, s_states = [], []
    for l in range(depth):
        mod = ada_modulation(c_all, w_ada[l], b_ada[l])
        mod_p = [m[:, None, :] for m in jnp.split(mod[:bsz], 6, axis=-1)]
        mod_s = [jnp.repeat(m, SAMPLE_ROWS, axis=0)[None] for m in jnp.split(mod[bsz:bsz + dbs], 6, axis=-1)]
        w_r, b_r = regroup_in_weights(w_in[l], b_in[l])
        cw = compress_weights(cmp_w1[l], cmp_pe[l], cmp_w2[l])
        wm, wa, wo = w_branch_m[l].astype(BF16), w_branch_a[l].astype(BF16), w_out[l].astype(BF16)
        wu, wd = w_up[l].astype(BF16), w_down[l].astype(BF16)

        sh_m, sc_m, gt_m, sh_f, sc_f, gt_f = mod_p
        z = in_projection(xp, g_pre_mix[l], sc_m, sh_m, w_r, b_r, tm=math.gcd(t, 1024))
        q_rot, kv_cmp, kv_slc, kv_win = rope_split(z, tab_p, tm=tm_p)
        hm, p_c, p_n, p_m = mlstm(z, m_norm_w[l], jnp.zeros((bsz, M_HEADS, M_HEAD_DIM, M_HEAD_DIM), F32),
                                  jnp.zeros((bsz, M_HEADS, M_HEAD_DIM), F32), jnp.zeros((bsz, M_HEADS), F32),
                                  lb=lchunk, lp=lchunk, t_real=lchunk)
        ha = nsa_prompt(q_rot, z, compress_prompt(kv_cmp, cw), kv_slc, kv_win, tq=256)
        xp = mix_out(xp, hm, ha, z, gt_m, g_post_mix[l], wm, wa, wo, tm=tm_p)
        xp, p_conv = conv_ffn_rows(xp, g_pre_ffn[l], sc_f, sh_f, gt_f, g_post_ffn[l], wu, conv_w[l], conv_b[l], wd,
                                   tm=tm_p)
        wkeep = min(WINDOW, t)
        p_states.append((_kv_rows(kv_cmp, bsz, t), _kv_rows(kv_slc, bsz, t), _kv_rows(kv_win[:, t - wkeep:], bsz, wkeep),
                         p_c, p_n, p_m, p_conv))

        sh_m, sc_m, gt_m, sh_f, sc_f, gt_f = mod_s
        z = in_projection(xs, g_pre_mix[l], sc_m, sh_m, w_r, b_r, tm=srows)
        q_rot, kv_cmp, kv_slc, kv_win = rope_split(z, tab_s, tm=srows)
        z3 = z.reshape(dbs, SAMPLE_ROWS, Z_WIDTH)
        hm, s_c, s_n, s_m = mlstm(z3, m_norm_w[l], state_C[l].astype(F32), state_n[l].astype(F32),
                                  state_m[l].astype(F32), lb=SAMPLE_ROWS, lp=128, t_real=ts)
        cmp_s = compress_paged(feature_major_pool(cache_cmp_kv[l].astype(F32)), page_table, cw)
        new3 = lambda a: a.reshape(dbs, SAMPLE_ROWS, 2 * KV_WIDTH)
        win_cache = cache_win_kv[l].astype(F32).reshape(dbs, -1, 2 * KV_WIDTH)
        ha = nsa_sample(q_rot.reshape(dbs, SAMPLE_ROWS, d), z3, cmp_s, feature_major_pool(cache_slc_kv[l].astype(F32)),
                        page_table, new3(kv_slc), win_cache, new3(kv_win), past, ts)
        xs = mix_out(xs, hm.reshape(1, srows, d), ha.reshape(1, srows, d), z, gt_m, g_post_mix[l], wm, wa, wo, tm=srows)
        st = state_conv[l].astype(F32)
        s2 = jnp.pad(st, ((0, 0), (0, SAMPLE_ROWS - (CONV_W - 1)), (0, 0))).reshape(1, srows, 2 * D_FF)
        s1 = jnp.pad(st[:, 1:], ((0, 0), (0, SAMPLE_ROWS - 1), (0, 0))).reshape(1, srows, 2 * D_FF)
        xs, u = conv_ffn(xs, g_pre_ffn[l], sc_f, sh_f, gt_f, g_post_ffn[l], wu, conv_w[l], conv_b[l], wd, tm=srows,
                         state_rows=(s1, s2))
        wb = win_cache.shape[1]
        s_win = jnp.concatenate([win_cache, new3(kv_win)[:, :ts]], axis=1)[:, ts:]
        s_conv = jnp.concatenate([st, u.reshape(dbs, SAMPLE_ROWS, 2 * D_FF)[:, :ts]], axis=1)[:, ts:]
        s_states.append((_kv_rows(new3(kv_cmp)[:, :ts], dbs, ts), _kv_rows(new3(kv_slc)[:, :ts], dbs, ts),
                         _kv_rows(s_win, dbs, wb), s_c, s_n, s_m, s_conv))

    stack = lambda states: [jnp.stack([s[i] for s in states]) for i in range(7)]
    y_sample = xs.reshape(dbs, SAMPLE_ROWS, d)[:, :ts]
    return (xp, y_sample, *stack(p_states), *stack(s_states))
```

```python
import functools
import math

import numpy as np
import jax
import jax.numpy as jnp
from jax import lax
from jax.experimental import pallas as pl
from jax.experimental.pallas import tpu as pltpu

F32 = jnp.float32
BF16 = jnp.bfloat16

D_MODEL = 1024
M_HEADS = 4
M_HEAD_DIM = 256
A_HEADS = 16
A_HEAD_DIM = 64
A_KV_HEADS = 4
A_GROUP = 4
KV_WIDTH = A_KV_HEADS * A_HEAD_DIM
CMP_STRIDE = 16
CMP_BLOCK = 32
SLC_BLOCK = 64
SLC_TOPK = 16
WINDOW = 512
ROPE_THETA = 500000.0
ROPE_DIM = 16
BIG = 1e6
D_FF = 2816
CONV_W = 3
NORM_EPS = 1e-6
PAGE_SIZE = 128
NEG_INF = float("-inf")
TINY = float(np.finfo(np.float32).tiny)

Z_MQ, Z_MK, Z_MV, Z_MO, Z_AQ, Z_GA, Z_GB, Z_AKV, Z_SMALL = 0, 1024, 2048, 3072, 4096, 5120, 6144, 7168, 8704
Z_WIDTH = 9216
SMALL_MI, SMALL_MF, SMALL_AG = 0, 4, 8

VMEM_LIMIT = 48 * 1024 * 1024


def _cparams(sem):
    return pltpu.CompilerParams(dimension_semantics=sem, vmem_limit_bytes=VMEM_LIMIT)


def _dot(a, b):
    return jnp.dot(a.astype(BF16), b.astype(BF16), preferred_element_type=F32)


def _dot_nt(a, b):
    return lax.dot_general(a.astype(BF16), b.astype(BF16), (((1,), (1,)), ((), ())), preferred_element_type=F32)


def _dot_tn(a, b):
    return lax.dot_general(a.astype(BF16), b.astype(BF16), (((0,), (0,)), ((), ())), preferred_element_type=F32)


def _split3(x):
    x1 = x.astype(BF16)
    r1 = x - x1.astype(F32)
    x2 = r1.astype(BF16)
    x3 = (r1 - x2.astype(F32)).astype(BF16)
    return x1, x2, x3


def _rms(x, g):
    return x * lax.rsqrt(jnp.mean(x * x, axis=-1, keepdims=True) + NORM_EPS) * g


def _ada_kernel(c_ref, w_ref, b_ref, o_ref):
    c = c_ref[...]
    o_ref[...] = _dot(c * jax.nn.sigmoid(c), w_ref[...]) + b_ref[...]


def ada_modulation(c, w_ada, b_ada):
    rows, d = c.shape
    n = w_ada.shape[1]
    tn = 512
    return pl.pallas_call(
        _ada_kernel,
        out_shape=jax.ShapeDtypeStruct((rows, n), F32),
        grid=(n // tn,),
        in_specs=[pl.BlockSpec((rows, d), lambda j: (0, 0)),
                  pl.BlockSpec((d, tn), lambda j: (0, j)),
                  pl.BlockSpec((1, tn), lambda j: (0, j))],
        out_specs=pl.BlockSpec((rows, tn), lambda j: (0, j)),
        compiler_params=_cparams(("arbitrary",)),
        name="ada",
    )(c, w_ada, b_ada.reshape(1, n))


def _inproj_kernel(x_ref, g_ref, sc_ref, sh_ref, w_ref, b_ref, o_ref, h_ref):
    @pl.when(pl.program_id(2) == 0)
    def _():
        h = _rms(x_ref[0], g_ref[...]) * (1.0 + sc_ref[0]) + sh_ref[0]
        h_ref[...] = h.astype(BF16)

    o_ref[0] = jnp.dot(h_ref[...], w_ref[...], preferred_element_type=F32) + b_ref[...]


def in_projection(x, g, sc, sh, w_bf16, b, tm):
    bsz, t, d = x.shape
    r = sc.shape[1]
    rb = 1 if r == 1 else tm
    tn = 1024
    mod_spec = pl.BlockSpec((1, rb, d), (lambda b_, i, j: (b_, 0, 0)) if r == 1 else (lambda b_, i, j: (b_, i, 0)))
    return pl.pallas_call(
        _inproj_kernel,
        out_shape=jax.ShapeDtypeStruct((bsz, t, Z_WIDTH), F32),
        grid=(bsz, t // tm, Z_WIDTH // tn),
        in_specs=[pl.BlockSpec((1, tm, d), lambda b_, i, j: (b_, i, 0)),
                  pl.BlockSpec((1, d), lambda b_, i, j: (0, 0)),
                  mod_spec, mod_spec,
                  pl.BlockSpec((d, tn), lambda b_, i, j: (0, j)),
                  pl.BlockSpec((1, tn), lambda b_, i, j: (0, j))],
        out_specs=pl.BlockSpec((1, tm, tn), lambda b_, i, j: (b_, i, j)),
        scratch_shapes=[pltpu.VMEM((tm, d), BF16)],
        compiler_params=_cparams(("arbitrary", "arbitrary", "arbitrary")),
        name="inproj",
    )(x, g.reshape(1, d), sc, sh, w_bf16, b.reshape(1, Z_WIDTH))


def regroup_in_weights(w_in, b_in):
    mw = M_HEADS * M_HEAD_DIM
    o_mi = 4 * mw
    o_aq = o_mi + 2 * M_HEADS
    o_akv = o_aq + A_HEADS * A_HEAD_DIM
    o_ag = o_akv + 6 * KV_WIDTH
    o_ga = o_ag + 3 * A_HEADS
    o_gb = o_ga + D_MODEL

    def regroup(a):
        lead = a.shape[:-1]
        parts = [a[..., :o_mi], a[..., o_aq:o_akv], a[..., o_ga:o_gb], a[..., o_gb:o_gb + D_MODEL],
                 a[..., o_akv:o_ag], a[..., o_mi:o_aq], a[..., o_ag:o_ga],
                 jnp.zeros(lead + (128 - 2 * M_HEADS - 3 * A_HEADS,), a.dtype),
                 jnp.zeros(lead + (Z_WIDTH - Z_SMALL - 128,), a.dtype)]
        return jnp.concatenate(parts, axis=-1)

    return regroup(w_in).astype(BF16), regroup(b_in)


def rope_tables(pos):
    half = ROPE_DIM // 2
    inv_freq = ROPE_THETA ** (-jnp.arange(half, dtype=F32) / half)
    ang = pos.astype(F32)[:, None] * inv_freq
    cos, sin = jnp.cos(ang), jnp.sin(ang)
    rows = pos.shape[0]
    zeros = jnp.zeros((rows, half), F32)
    rest1 = jnp.ones((rows, A_HEAD_DIM - ROPE_DIM), F32)
    rest0 = jnp.zeros((rows, A_HEAD_DIM - ROPE_DIM), F32)
    c = jnp.concatenate([cos, cos, rest1], axis=1)
    sa = jnp.concatenate([zeros, sin, rest0], axis=1)
    sb = jnp.concatenate([-sin, zeros, rest0], axis=1)
    return tuple(jnp.concatenate([a, a], axis=1) for a in (c, sa, sb))


def _rope_apply(x, c, sa, sb):
    w = x.shape[1]
    n = w // 128
    ct, sat, sbt = (jnp.concatenate([a] * n, axis=1) for a in (c, sa, sb))
    return x * ct + pltpu.roll(x, ROPE_DIM // 2, 1) * sat + pltpu.roll(x, w - ROPE_DIM // 2, 1) * sbt


def _rope_kernel(q_ref, c_ref, s_ref, w_ref, cos_ref, sa_ref, sb_ref, qo_ref, co_ref, so_ref, wo_ref):
    c, sa, sb = cos_ref[...], sa_ref[...], sb_ref[...]
    qo_ref[0] = _rope_apply(q_ref[0], c, sa, sb)
    for src, dst in ((c_ref, co_ref), (s_ref, so_ref), (w_ref, wo_ref)):
        kv = src[0]
        dst[0] = jnp.concatenate([_rope_apply(kv[:, :KV_WIDTH], c, sa, sb), kv[:, KV_WIDTH:]], axis=1)


def rope_split(z, tables, tm):
    bsz, t, _ = z.shape
    nt = t // tm
    kvw = 2 * KV_WIDTH
    tab_spec = pl.BlockSpec((tm, 128), lambda b_, i: (i, 0))
    return pl.pallas_call(
        _rope_kernel,
        out_shape=(jax.ShapeDtypeStruct((bsz, t, D_MODEL), F32),) + (jax.ShapeDtypeStruct((bsz, t, kvw), F32),) * 3,
        grid=(bsz, nt),
        in_specs=[pl.BlockSpec((1, tm, D_MODEL), lambda b_, i: (b_, i, Z_AQ // D_MODEL)),
                  pl.BlockSpec((1, tm, kvw), lambda b_, i: (b_, i, Z_AKV // kvw)),
                  pl.BlockSpec((1, tm, kvw), lambda b_, i: (b_, i, Z_AKV // kvw + 1)),
                  pl.BlockSpec((1, tm, kvw), lambda b_, i: (b_, i, Z_AKV // kvw + 2)),
                  tab_spec, tab_spec, tab_spec],
        out_specs=(pl.BlockSpec((1, tm, D_MODEL), lambda b_, i: (b_, i, 0)),) +
                  (pl.BlockSpec((1, tm, kvw), lambda b_, i: (b_, i, 0)),) * 3,
        compiler_params=_cparams(("arbitrary", "arbitrary")),
        name="rope",
    )(z, z, z, z, *tables)


def _mlstm_kernel(q_ref, k_ref, v_ref, o_ref, s_ref, nw_ref, c0_ref, n0_ref, m0_ref,
                  h_ref, c_ref, n_ref, m_ref, *, lb, lp, t_real):
    @pl.when(pl.program_id(1) == 0)
    def _():
        c_ref[...] = c0_ref[...]
        n_ref[...] = n0_ref[...]
        m_ref[...] = m0_ref[...]

    def pad(a):
        if lb == lp:
            return a
        return jnp.concatenate([a, jnp.zeros((lp - lb, a.shape[1]), a.dtype)], axis=0)

    small = pad(s_ref[0])
    small_t = small.T
    row_c = lax.broadcasted_iota(jnp.int32, (lp, 1), 0)
    row_r = lax.broadcasted_iota(jnp.int32, (1, lp), 1)
    li_col_all = jnp.where(row_c < t_real, small, NEG_INF)
    lf_col_all = jnp.where(row_c < t_real, jax.nn.log_sigmoid(small), 0.0)
    li_row_all = jnp.where(row_r < t_real, small_t[0:8], NEG_INF)
    lf_row_all = jnp.where(row_r < t_real, jax.nn.log_sigmoid(small_t[0:8]), 0.0)
    rr = lax.broadcasted_iota(jnp.int32, (lp, lp), 0)
    cc = lax.broadcasted_iota(jnp.int32, (lp, lp), 1)
    causal = cc <= rr
    tril = jnp.where(causal, 1.0, 0.0).astype(BF16)
    triu = jnp.where(rr <= cc, 1.0, 0.0).astype(BF16)
    b_col_all = sum(jnp.dot(tril, p, preferred_element_type=F32) for p in _split3(lf_col_all))
    b_row_all = sum(jnp.dot(p, triu, preferred_element_type=F32) for p in _split3(lf_row_all))

    q_all, k_all, v_all, o_all = pad(q_ref[0]), pad(k_ref[0]), pad(v_ref[0]), pad(o_ref[0])
    nw = nw_ref[...]
    for h in range(M_HEADS):
        hs = slice(h * M_HEAD_DIM, (h + 1) * M_HEAD_DIM)
        qf = q_all[:, hs]
        kf = k_all[:, hs] * (M_HEAD_DIM ** -0.5)
        vf = v_all[:, hs]
        li_row = li_row_all[SMALL_MI + h:SMALL_MI + h + 1, :]
        b_row = b_row_all[SMALL_MF + h:SMALL_MF + h + 1, :]
        li_col = li_col_all[:, SMALL_MI + h:SMALL_MI + h + 1]
        b_col = b_col_all[:, SMALL_MF + h:SMALL_MF + h + 1]
        m_prev = m_ref[0, h]
        c_prev = c_ref[0, h]
        n_prev = n_ref[0, h]

        dlog = jnp.where(causal, b_col - b_row + li_row, NEG_INF)
        inter = m_prev + b_col
        mt = jnp.maximum(inter, jnp.max(dlog, axis=1, keepdims=True))
        a = jnp.exp(inter - mt)
        s = _dot_nt(qf, kf) * jnp.exp(dlog - mt)
        num = a * _dot_nt(qf, c_prev) + _dot(s, vf)
        den = a * jnp.sum(qf * n_prev, axis=1, keepdims=True) + jnp.sum(s, axis=1, keepdims=True)
        hh = num / jnp.maximum(jnp.abs(den), jnp.exp(-mt))
        mu = jnp.mean(hh, axis=1, keepdims=True)
        var = jnp.mean(jnp.square(hh - mu), axis=1, keepdims=True)
        out = (hh - mu) * lax.rsqrt(var + NORM_EPS) * nw[:, hs] * jax.nn.sigmoid(o_all[:, hs])
        h_ref[0, :, hs] = out[:lb]

        bl = b_row[:, lp - 1:lp]
        wlog = bl - b_col + li_col
        m_new = jnp.maximum(m_prev + bl, jnp.max(wlog, axis=0, keepdims=True))
        w = jnp.exp(wlog - m_new)
        decay = jnp.exp(m_prev + bl - m_new)
        c_ref[0, h] = decay * c_prev + _dot_tn(vf * w, kf)
        n_ref[0, h] = decay * n_prev + jnp.sum(w * kf, axis=0, keepdims=True)
        m_ref[0, h] = m_new


def mlstm(z, norm_w, c0, n0, m0, lb, lp, t_real):
    bsz, t, _ = z.shape
    nc = t // lb
    mw = M_HEADS * M_HEAD_DIM
    zspec = lambda col: pl.BlockSpec((1, lb, mw), lambda b_, c: (b_, c, col // mw))
    cst = lambda shape: pl.BlockSpec((1,) + shape, lambda b_, c: (b_,) + (0,) * len(shape))
    h, c, n, m = pl.pallas_call(
        functools.partial(_mlstm_kernel, lb=lb, lp=lp, t_real=t_real),
        out_shape=(jax.ShapeDtypeStruct((bsz, t, mw), F32),
                   jax.ShapeDtypeStruct((bsz, M_HEADS, M_HEAD_DIM, M_HEAD_DIM), F32),
                   jax.ShapeDtypeStruct((bsz, M_HEADS, 1, M_HEAD_DIM), F32),
                   jax.ShapeDtypeStruct((bsz, M_HEADS, 1, 1), F32)),
        grid=(bsz, nc),
        in_specs=[zspec(Z_MQ), zspec(Z_MK), zspec(Z_MV), zspec(Z_MO),
                  pl.BlockSpec((1, lb, 128), lambda b_, c: (b_, c, Z_SMALL // 128)),
                  pl.BlockSpec((1, mw), lambda b_, c: (0, 0)),
                  cst((M_HEADS, M_HEAD_DIM, M_HEAD_DIM)), cst((M_HEADS, 1, M_HEAD_DIM)), cst((M_HEADS, 1, 1))],
        out_specs=(pl.BlockSpec((1, lb, mw), lambda b_, c: (b_, c, 0)),
                   cst((M_HEADS, M_HEAD_DIM, M_HEAD_DIM)), cst((M_HEADS, 1, M_HEAD_DIM)), cst((M_HEADS, 1, 1))),
        compiler_params=_cparams(("arbitrary", "arbitrary")),
        name="mlstm",
    )(z, z, z, z, z, norm_w.reshape(1, mw), c0, n0.reshape(bsz, M_HEADS, 1, M_HEAD_DIM),
      m0.reshape(bsz, M_HEADS, 1, 1))
    return h, c, n.reshape(bsz, M_HEADS, M_HEAD_DIM), m.reshape(bsz, M_HEADS)


def _mix_kernel(x_ref, hm_ref, ha_ref, ga_ref, gb_ref, gt_ref, g_ref, wm_ref, wa_ref, wo_ref, o_ref):
    mixed = (jax.nn.sigmoid(ga_ref[0]) * _dot(hm_ref[0], wm_ref[...]) +
             jax.nn.sigmoid(gb_ref[0]) * _dot(ha_ref[0], wa_ref[...]))
    o_ref[0] = x_ref[0] + gt_ref[0] * _rms(_dot(mixed, wo_ref[...]), g_ref[...])


def mix_out(x, hm, ha, z, gt, g_post, wm, wa, wo, tm):
    bsz, t, d = x.shape
    r = gt.shape[1]
    rb = 1 if r == 1 else tm
    row = lambda col=0: pl.BlockSpec((1, tm, d), lambda b_, i: (b_, i, col // d))
    mod = pl.BlockSpec((1, rb, d), (lambda b_, i: (b_, 0, 0)) if r == 1 else (lambda b_, i: (b_, i, 0)))
    wsp = pl.BlockSpec((d, d), lambda b_, i: (0, 0))
    return pl.pallas_call(
        _mix_kernel,
        out_shape=jax.ShapeDtypeStruct((bsz, t, d), F32),
        grid=(bsz, t // tm),
        in_specs=[row(), row(), row(), row(Z_GA), row(Z_GB), mod,
                  pl.BlockSpec((1, d), lambda b_, i: (0, 0)), wsp, wsp, wsp],
        out_specs=row(),
        compiler_params=_cparams(("arbitrary", "arbitrary")),
        name="mix",
    )(x, hm, ha, z, z, gt, g_post.reshape(1, d), wm, wa, wo)


FF_CHUNK = 256


def _ffn_kernel(x_ref, g_ref, sc_ref, sh_ref, gt_ref, gp_ref, wa_ref, wg_ref, cwa_ref, cwg_ref, cba_ref, cbg_ref,
                wd_ref, s1a_ref, s1g_ref, s2a_ref, s2g_ref, y_ref, ua_ref, ug_ref, h_ref, acc_ref):
    f = pl.program_id(2)
    tm = x_ref.shape[1]

    @pl.when(f == 0)
    def _():
        h = _rms(x_ref[0], g_ref[...]) * (1.0 + sc_ref[0]) + sh_ref[0]
        h_ref[...] = h.astype(BF16)
        acc_ref[...] = jnp.zeros_like(acc_ref)

    t = lax.broadcasted_iota(jnp.int32, (tm, 1), 0) % SAMPLE_ROWS

    def branch(w_ref, cw_ref, cb_ref, s1_ref, s2_ref):
        u = jnp.dot(h_ref[...], w_ref[...], preferred_element_type=F32)
        u1 = jnp.where(t < 1, s1_ref[0], pltpu.roll(u, 1, 0))
        u2 = jnp.where(t < 2, s2_ref[0], pltpu.roll(u, 2, 0))
        cw = cw_ref[...]
        return u, cb_ref[...] + cw[0:1] * u2 + cw[1:2] * u1 + cw[2:3] * u

    ua_ref[0], conv_a = branch(wa_ref, cwa_ref, cba_ref, s1a_ref, s2a_ref)
    ug_ref[0], conv_g = branch(wg_ref, cwg_ref, cbg_ref, s1g_ref, s2g_ref)
    acc_ref[...] += _dot(jax.nn.gelu(conv_g) * conv_a, wd_ref[...])

    @pl.when(f == pl.num_programs(2) - 1)
    def _():
        y_ref[0] = x_ref[0] + gt_ref[0] * _rms(acc_ref[...], gp_ref[...])


def conv_ffn(x, g_pre, sc, sh, gt, g_post, w_up, conv_w, conv_b, w_down, tm, state_rows):
    bsz, t, d = x.shape
    ck = FF_CHUNK
    nf = D_FF // ck
    xrow = pl.BlockSpec((1, tm, d), lambda b_, i, f: (b_, i, 0))
    vec = pl.BlockSpec((1, d), lambda b_, i, f: (0, 0))
    col_a = lambda rows: pl.BlockSpec((rows, ck), lambda b_, i, f: (0, f))
    col_g = lambda rows: pl.BlockSpec((rows, ck), lambda b_, i, f: (0, nf + f))
    st_a = pl.BlockSpec((1, tm, ck), lambda b_, i, f: (b_, i, f))
    st_g = pl.BlockSpec((1, tm, ck), lambda b_, i, f: (b_, i, nf + f))
    cb = conv_b.reshape(1, 2 * D_FF)
    y, ua, ug = pl.pallas_call(
        _ffn_kernel,
        out_shape=(jax.ShapeDtypeStruct((bsz, t, d), F32),) + (jax.ShapeDtypeStruct((bsz, t, D_FF), F32),) * 2,
        grid=(bsz, t // tm, nf),
        in_specs=[xrow, vec, xrow, xrow, xrow, vec, col_a(d), col_g(d), col_a(CONV_W), col_g(CONV_W), col_a(1), col_g(1),
                  pl.BlockSpec((ck, d), lambda b_, i, f: (f, 0)), st_a, st_g, st_a, st_g],
        out_specs=(xrow, st_a, st_a),
        scratch_shapes=[pltpu.VMEM((tm, d), BF16), pltpu.VMEM((tm, d), F32)],
        compiler_params=_cparams(("arbitrary", "arbitrary", "arbitrary")),
        name="ffn",
    )(x, g_pre.reshape(1, d), sc, sh, gt, g_post.reshape(1, d), w_up, w_up, conv_w, conv_w, cb, cb, w_down,
      state_rows[0], state_rows[0], state_rows[1], state_rows[1])
    return y, jnp.concatenate([ua, ug], axis=-1)


def _ffn_rows_kernel(x_ref, g_ref, sc_ref, sh_ref, gt_ref, gp_ref, wu_ref, cw_ref, cb_ref, wd_ref,
                     y_ref, tail_ref, carry_ref):
    tm = x_ref.shape[1]
    ck = FF_CHUNK

    @pl.when(pl.program_id(1) == 0)
    def _():
        carry_ref[...] = jnp.zeros_like(carry_ref)

    h = (_rms(x_ref[0], g_ref[...]) * (1.0 + sc_ref[0]) + sh_ref[0]).astype(BF16)
    top = lax.broadcasted_iota(jnp.int32, (8, 1), 0)
    acc = jnp.zeros((tm, x_ref.shape[2]), F32)

    def conv(cols):
        u = jnp.dot(h, wu_ref[:, cols], preferred_element_type=F32)
        r1 = pltpu.roll(u, 1, 0)
        r2 = pltpu.roll(u, 2, 0)
        prev = carry_ref[:, cols]
        u1 = jnp.concatenate([jnp.where(top < 1, prev[1:2], r1[0:8]), r1[8:]], axis=0)
        u2 = jnp.concatenate([jnp.where(top < 1, prev[0:1], jnp.where(top < 2, prev[1:2], r2[0:8])), r2[8:]], axis=0)
        carry_ref[0:2, cols] = u[tm - 2:tm]
        tail_ref[0, 0, :, cols] = u[tm - 2:tm]
        cw = cw_ref[:, cols]
        return cb_ref[:, cols] + cw[0:1] * u2 + cw[1:2] * u1 + cw[2:3] * u

    for c in range(D_FF // ck):
        conv_a = conv(slice(c * ck, (c + 1) * ck))
        conv_g = conv(slice(D_FF + c * ck, D_FF + (c + 1) * ck))
        acc = acc + _dot(jax.nn.gelu(conv_g) * conv_a, wd_ref[c * ck:(c + 1) * ck, :])
    y_ref[0] = x_ref[0] + gt_ref[0] * _rms(acc, gp_ref[...])


def conv_ffn_rows(x, g_pre, sc, sh, gt, g_post, w_up, conv_w, conv_b, w_down, tm):
    bsz, t, d = x.shape
    mod = pl.BlockSpec((1, 1, d), lambda b_, i: (b_, 0, 0))
    xrow = pl.BlockSpec((1, tm, d), lambda b_, i: (b_, i, 0))
    whole = lambda a: pl.BlockSpec(a.shape, lambda b_, i: (0,) * a.ndim, pipeline_mode=pl.Buffered(1))
    cb = conv_b.reshape(1, 2 * D_FF)
    g1, g2 = g_pre.reshape(1, d), g_post.reshape(1, d)
    y, tail = pl.pallas_call(
        _ffn_rows_kernel,
        out_shape=(jax.ShapeDtypeStruct((bsz, t, d), F32),
                   jax.ShapeDtypeStruct((bsz, t // tm, CONV_W - 1, 2 * D_FF), F32)),
        grid=(bsz, t // tm),
        in_specs=[xrow, whole(g1), mod, mod, mod, whole(g2), whole(w_up), whole(conv_w), whole(cb), whole(w_down)],
        out_specs=(xrow, pl.BlockSpec((1, 1, CONV_W - 1, 2 * D_FF), lambda b_, i: (b_, i, 0, 0))),
        scratch_shapes=[pltpu.VMEM((8, 2 * D_FF), F32)],
        compiler_params=_cparams(("arbitrary", "arbitrary")),
        name="ffn_rows",
    )(x, g1, sc, sh, gt, g2, w_up, conv_w, cb, w_down)
    return y, tail[:, -1]


LANE = 128
QUARTERS = 2 * KV_WIDTH // LANE
HEADS_PER_LANE_ROW = LANE // A_HEAD_DIM


def compress_weights(cmp_w1, cmp_pe, cmp_w2):
    eye = jnp.eye(HEADS_PER_LANE_ROW, dtype=F32)
    bd = lambda w: jnp.einsum("gh,...de->...gdhe", eye, w).reshape(w.shape[:-2] + (LANE, LANE))
    w1ab = jnp.concatenate([bd(cmp_w1[:, :CMP_STRIDE]), bd(cmp_w1[:, CMP_STRIDE:])], axis=-1).astype(BF16)
    w1ab = w1ab.reshape(2, CMP_STRIDE // 2, 2 * LANE, 2 * LANE)
    w2 = bd(cmp_w2).astype(BF16)
    w1r = cmp_w1.reshape(2, CMP_BLOCK * A_HEAD_DIM, A_HEAD_DIM)
    pe = cmp_pe.reshape(2, CMP_BLOCK * A_HEAD_DIM, 1)
    return w1ab, w2, w1r, pe


def _compress_chunk(get_x, nrows, w1_ref, w2_ref, w1r_ref, pe_ref, carry_ref, out_ref):
    row = lax.broadcasted_iota(jnp.int32, (nrows, 1), 0)
    for kind in range(2):
        peb = jnp.sum(pe_ref[kind] * w1r_ref[kind], axis=0, keepdims=True)
        peb = jnp.concatenate([peb] * HEADS_PER_LANE_ROW, axis=1)
        acc2 = jnp.zeros((2 * nrows, 2 * LANE), F32)
        for j in range(CMP_STRIDE // 2):
            x = jnp.concatenate([get_x(2 * kind, j), get_x(2 * kind + 1, j)], axis=0).astype(BF16)
            acc2 = acc2 + jnp.dot(x, w1_ref[kind, j], preferred_element_type=F32)
        for half in range(2):
            acc = acc2[half * nrows:(half + 1) * nrows]
            acc_a, acc_b = acc[:, :LANE], acc[:, LANE:]
            a_shift = jnp.where(row == 0, carry_ref[kind, half], pltpu.roll(acc_a, 1, 0))
            carry_ref[kind, half] = acc_a[nrows - 1:nrows]
            hid = jax.nn.gelu(a_shift + acc_b + peb)
            out_ref[0, kind, :, half * LANE:(half + 1) * LANE] = _dot(hid, w2_ref[kind])


def _cmp_prompt_kernel(kv0_ref, kv1_ref, kv2_ref, kv3_ref, w1_ref, w2_ref, w1r_ref, pe_ref, out_ref, carry_ref):
    kv_refs = (kv0_ref, kv1_ref, kv2_ref, kv3_ref)
    nseg = kv0_ref.shape[1] // CMP_STRIDE
    carry_ref[...] = jnp.zeros_like(carry_ref)
    offset = lambda quarter, s: kv_refs[quarter][0, pl.ds(s, nseg, stride=CMP_STRIDE), :]
    get_x = lambda quarter, j: jnp.concatenate([offset(quarter, 2 * j), offset(quarter, 2 * j + 1)], axis=1)
    _compress_chunk(get_x, nseg, w1_ref, w2_ref, w1r_ref, pe_ref, carry_ref, out_ref)


def _cmp_weight_specs():
    zero = lambda n: (lambda *_: (0,) * n)
    return [pl.BlockSpec((2, CMP_STRIDE // 2, 2 * LANE, 2 * LANE), zero(4)),
            pl.BlockSpec((2, LANE, LANE), zero(3)),
            pl.BlockSpec((2, CMP_BLOCK * A_HEAD_DIM, A_HEAD_DIM), zero(3)),
            pl.BlockSpec((2, CMP_BLOCK * A_HEAD_DIM, 1), zero(3))]


def compress_prompt(kv_cmp, cw):
    bsz, t, w = kv_cmp.shape
    nseg = t // CMP_STRIDE
    return pl.pallas_call(
        _cmp_prompt_kernel,
        out_shape=jax.ShapeDtypeStruct((bsz, 2, nseg, KV_WIDTH), F32),
        grid=(bsz,),
        in_specs=[pl.BlockSpec((1, t, LANE), functools.partial(lambda q, b_: (b_, 0, q), q)) for q in range(QUARTERS)]
                 + _cmp_weight_specs(),
        out_specs=pl.BlockSpec((1, 2, nseg, KV_WIDTH), lambda b_: (b_, 0, 0, 0)),
        scratch_shapes=[pltpu.VMEM((2, 2, 1, LANE), F32)],
        compiler_params=_cparams(("arbitrary",)),
        name="cmp_prompt",
    )(*([kv_cmp] * QUARTERS), *cw)


def feature_major_pool(cache):
    n_pool = cache.shape[0]
    return jnp.transpose(cache, (0, 2, 3, 4, 1)).reshape(n_pool, 2 * KV_WIDTH, PAGE_SIZE)


def _page_copy(pool_ref, dst, sem_ref, pt_ref, step, slot, k, n_chunks, pages):
    b_ = step // n_chunks
    c = step % n_chunks
    pid = pt_ref[b_, c * pages + k]
    return pltpu.make_async_copy(pool_ref.at[pid], dst(slot, k), sem_ref.at[slot])


def _page_pipeline(pool_ref, dst, sem_ref, pt_ref, n_chunks, pages):
    step = pl.program_id(0) * n_chunks + pl.program_id(1)
    total = pl.num_programs(0) * n_chunks
    slot = step % 2

    def start(st, sl):
        for k in range(pages):
            _page_copy(pool_ref, dst, sem_ref, pt_ref, st, sl, k, n_chunks, pages).start()

    @pl.when(step == 0)
    def _():
        start(step, slot)

    @pl.when(step + 1 < total)
    def _():
        start(step + 1, 1 - slot)

    for k in range(pages):
        _page_copy(pool_ref, dst, sem_ref, pt_ref, step, slot, k, n_chunks, pages).wait()
    return slot


def _cmp_sample_kernel(pt_ref, pool_ref, w1_ref, w2_ref, w1r_ref, pe_ref, out_ref,
                       buf_ref, sem_ref, x_ref, carry_ref, *, n_chunks, pages):
    slot = _page_pipeline(pool_ref, lambda sl, k: buf_ref.at[sl, k], sem_ref, pt_ref, n_chunks, pages)

    @pl.when(pl.program_id(1) == 0)
    def _():
        carry_ref[...] = jnp.zeros_like(carry_ref)

    segs = PAGE_SIZE // CMP_STRIDE
    nrows = pages * segs
    half_rows = PAGE_SIZE // 2
    dst = lax.broadcasted_iota(jnp.int32, (PAGE_SIZE, PAGE_SIZE), 0)
    src = lax.broadcasted_iota(jnp.int32, (PAGE_SIZE, PAGE_SIZE), 1)
    wanted = (dst % segs) * CMP_STRIDE + 2 * ((dst % half_rows) // segs) + dst // half_rows
    pick = jnp.where(src == wanted, 1.0, 0.0).astype(BF16)

    def relayout(p, carry):
        for quarter in range(QUARTERS):
            t = buf_ref[slot, p, quarter * LANE:(quarter + 1) * LANE, :]
            y = _dot_nt(pick, t)
            x_ref[quarter, p] = jnp.concatenate([y[:half_rows], y[half_rows:]], axis=1)
        return carry

    lax.fori_loop(0, pages, relayout, 0, unroll=8)

    def get_x(quarter, j):
        return x_ref[quarter, :, j * segs:(j + 1) * segs, :].reshape(nrows, 2 * LANE)

    _compress_chunk(get_x, nrows, w1_ref, w2_ref, w1r_ref, pe_ref, carry_ref, out_ref)


CMP_PAGES = 32
SLC_PAGES = 64


def compress_paged(pool, page_table, cw):
    dbs, n_pages = page_table.shape
    pages = math.gcd(n_pages, CMP_PAGES)
    n_chunks = n_pages // pages
    rows = pages * PAGE_SIZE // CMP_STRIDE
    return pl.pallas_call(
        functools.partial(_cmp_sample_kernel, n_chunks=n_chunks, pages=pages),
        out_shape=jax.ShapeDtypeStruct((dbs, 2, n_chunks * rows, KV_WIDTH), F32),
        grid_spec=pltpu.PrefetchScalarGridSpec(
            num_scalar_prefetch=1,
            grid=(dbs, n_chunks),
            in_specs=[pl.BlockSpec(memory_space=pl.ANY)] + _cmp_weight_specs(),
            out_specs=pl.BlockSpec((1, 2, rows, KV_WIDTH), lambda b_, c, pt: (b_, 0, c, 0)),
            scratch_shapes=[pltpu.VMEM((2, pages, 2 * KV_WIDTH, PAGE_SIZE), F32),
                            pltpu.SemaphoreType.DMA((2,)),
                            pltpu.VMEM((QUARTERS, pages, PAGE_SIZE // 2, 2 * LANE), F32),
                            pltpu.VMEM((2, 2, 1, LANE), F32)]),
        compiler_params=_cparams(("arbitrary", "arbitrary")),
        name="cmp_paged",
    )(page_table, pool, *cw)


def _masked_softmax_rows(s, mask):
    s = jnp.where(mask, s, NEG_INF)
    m = jnp.max(s, axis=-1, keepdims=True)
    m = jnp.where(m == NEG_INF, 0.0, m)
    e = jnp.exp(s - m)
    return e / jnp.maximum(jnp.sum(e, axis=-1, keepdims=True), TINY)


def _topk_mask(score, k, axis):
    n = score.shape[axis]
    idx = lax.broadcasted_iota(jnp.int32, score.shape, axis)
    sel = jnp.zeros(score.shape, F32)
    for _ in range(k):
        mx = jnp.max(score, axis=axis, keepdims=True)
        first = jnp.min(jnp.where(score == mx, idx, n), axis=axis, keepdims=True)
        pick = idx == first
        sel = jnp.where(pick, 1.0, sel)
        score = jnp.where(pick, NEG_INF, score)
    return sel


def _flash_tile(carry, s, v):
    m_old, l_old, acc = carry
    m_new = jnp.maximum(m_old, jnp.max(s, axis=-1, keepdims=True))
    m_safe = jnp.where(m_new == NEG_INF, 0.0, m_new)
    p = jnp.exp(s - m_safe)
    alpha = jnp.exp(m_old - m_safe)
    pv = v(p.astype(BF16)) if callable(v) else _dot(p, v)
    return m_new, alpha * l_old + jnp.sum(p, axis=-1, keepdims=True), alpha * acc + pv


MASKED = -(2.0 ** 100)
LOG2E = math.log2(math.e)


ONES_ROWS = 16
V_ROWS = A_HEAD_DIM + ONES_ROWS


def _flash_cols(carry, s, v_t):
    m_old, acc = carry
    m_new = jnp.maximum(m_old, jnp.max(s, axis=0, keepdims=True))
    return m_new, jnp.exp2(m_old - m_new) * acc + _dot(v_t, jnp.exp2(s - m_new))


def _flash_cols_init(cols):
    return jnp.full((1, cols), MASKED, F32), jnp.zeros((V_ROWS, cols), F32)


def _flash_cols_out(carry):
    acc = carry[1]
    return acc[:A_HEAD_DIM] / jnp.maximum(acc[A_HEAD_DIM:A_HEAD_DIM + 1], TINY)


def _flash_init(rows, dv):
    return jnp.full((rows, 1), NEG_INF, F32), jnp.zeros((rows, 1), F32), jnp.zeros((rows, dv), F32)


def _flash_out(carry):
    _, l, acc = carry
    return acc / jnp.maximum(l, TINY)


KEY_TILE = 512


def _nsa_prompt_kernel(q_ref, small_ref, cmp_ref, ks_ref, vs_ref, kw_ref, vw_ref, cov_ref, exp_ref, o_ref,
                       *, tq, n_slc):
    i = pl.program_id(1)
    q0 = i * tq
    tk = math.gcd(ks_ref.shape[1], KEY_TILE)
    cols = A_GROUP * tq
    q_t = (q_ref[0] * (A_HEAD_DIM ** -0.5 * LOG2E)).T.astype(BF16)
    gate_t = jax.nn.sigmoid(small_ref[0]).T
    ncmp = cmp_ref.shape[2]
    qpos = q0 + lax.broadcasted_iota(jnp.int32, (1, tq), 1)
    jcol = lax.broadcasted_iota(jnp.int32, (ncmp, 1), 0)
    cmp_mask = (jcol >= 1) & (jcol * CMP_STRIDE + (CMP_BLOCK - CMP_STRIDE - 1) <= qpos)
    blk = lax.broadcasted_iota(jnp.int32, (n_slc, 1), 0)
    cur = qpos // SLC_BLOCK
    forced = (blk == 0) | (blk == cur) | (blk == cur - 1)
    future = blk * SLC_BLOCK > qpos
    krow = lax.broadcasted_iota(jnp.int32, (tk, 1), 0)
    lanes = lambda a, r: a[:, r * tq:(r + 1) * tq]
    wk = min(WINDOW + tq, kw_ref.shape[1])
    w_off = pl.multiple_of(jnp.maximum(q0 - WINDOW, 0), tq)
    wpos = w_off + lax.broadcasted_iota(jnp.int32, (wk, 1), 0)
    band = (wpos <= qpos) & (wpos > qpos - WINDOW)
    pieces = []

    for g in range(A_KV_HEADS):
        gs = slice(g * A_HEAD_DIM, (g + 1) * A_HEAD_DIM)
        gv = slice(g * V_ROWS, (g + 1) * V_ROWS)
        head = lambda r: slice((g * A_GROUP + r) * A_HEAD_DIM, (g * A_GROUP + r + 1) * A_HEAD_DIM)
        qg = jnp.concatenate([q_t[head(r)] for r in range(A_GROUP)], axis=1)

        s = jnp.dot(kw_ref[0, pl.ds(w_off, wk), gs], qg, preferred_element_type=F32)
        s = jnp.concatenate([jnp.where(band, lanes(s, r), MASKED) for r in range(A_GROUP)], axis=1)
        o_w = _flash_cols_out(_flash_cols(_flash_cols_init(cols), s, vw_ref[0, gv, pl.ds(w_off, wk)]))

        s_c = _dot(cmp_ref[0, 0, :, gs], qg)
        p_r = []
        for r in range(A_GROUP):
            s_r = jnp.where(cmp_mask, lanes(s_c, r), NEG_INF)
            m = jnp.max(s_r, axis=0, keepdims=True)
            e = jnp.exp2(s_r - jnp.where(m == NEG_INF, 0.0, m))
            p_r.append(e * (1.0 / jnp.maximum(jnp.sum(e, axis=0, keepdims=True), TINY)))
        o_c = _dot_tn(cmp_ref[0, 1, :, gs], jnp.concatenate(p_r, axis=1))
        imp_t = _dot(cov_ref[...], p_r[0] + p_r[1] + p_r[2] + p_r[3])
        score = jnp.where(future, -BIG, imp_t + jnp.where(forced, BIG, 0.0))
        sel_t = _topk_mask(score, min(SLC_TOPK, n_slc), 0)
        sel_bias = jnp.concatenate([jnp.where(sel_t > 0.5, 0.0, MASKED),
                                    jnp.zeros((exp_ref.shape[1] - n_slc, tq), F32)], axis=0)
        sel_bias = jnp.concatenate([sel_bias] * A_GROUP, axis=1).astype(BF16)

        def slc_scores(off):
            return (jnp.dot(ks_ref[0, pl.ds(off, tk), gs], qg, preferred_element_type=F32) +
                    jnp.dot(exp_ref[pl.ds(off, tk), :], sel_bias, preferred_element_type=F32))

        def slc_step(j, carry):
            off = pl.multiple_of(j * tk, tk)
            return _flash_cols(carry, slc_scores(off), vs_ref[0, gv, pl.ds(off, tk)])

        j_last = (q0 + tq - 1) // tk
        carry = lax.fori_loop(0, j_last, slc_step, _flash_cols_init(cols))
        off = pl.multiple_of(j_last * tk, tk)
        causal = off + krow <= qpos
        s = slc_scores(off)
        s = jnp.concatenate([jnp.where(causal, lanes(s, r), MASKED) for r in range(A_GROUP)], axis=1)
        o_s = _flash_cols_out(_flash_cols(carry, s, vs_ref[0, gv, pl.ds(off, tk)]))

        for r in range(A_GROUP):
            row = SMALL_AG + 3 * (g * A_GROUP + r)
            pieces.append(gate_t[row:row + 1] * lanes(o_c, r) + gate_t[row + 1:row + 2] * lanes(o_s, r) +
                          gate_t[row + 2:row + 3] * lanes(o_w, r))
    o_ref[0] = jnp.concatenate(pieces, axis=0).T


def _coverage(n_cmp_rows, n_slc):
    cs = (np.arange(n_cmp_rows) - 1) * CMP_STRIDE
    ss = np.arange(n_slc) * SLC_BLOCK
    lo = np.maximum(cs[:, None], ss[None, :])
    hi = np.minimum(cs[:, None] + CMP_BLOCK, ss[None, :] + SLC_BLOCK)
    cov = np.clip(hi - lo, 0, None) / CMP_BLOCK
    cov[0] = 0.0
    return cov.astype(np.float32)


def _block_expand(n_rows, n_keys):
    return (np.arange(n_rows)[:, None] == (np.arange(n_keys) // SLC_BLOCK)[None, :]).astype(np.float32)


def nsa_prompt(q_rot, z, cmp, kv_slc, kv_win, tq):
    bsz, t, _ = q_rot.shape
    n_slc = t // SLC_BLOCK
    ncmp = cmp.shape[2]
    kk = lambda kv: kv[..., :KV_WIDTH].astype(BF16)
    def vt(kv):
        v = jnp.swapaxes(kv[..., KV_WIDTH:], 1, 2).astype(BF16).reshape(bsz, A_KV_HEADS, A_HEAD_DIM, t)
        ones = jnp.ones((bsz, A_KV_HEADS, ONES_ROWS, t), BF16)
        return jnp.concatenate([v, ones], axis=2).reshape(bsz, A_KV_HEADS * V_ROWS, t)

    cov_t = jnp.asarray(_coverage(ncmp, n_slc).T, BF16)
    n_exp = -(-n_slc // 128) * 128
    expand = jnp.asarray(_block_expand(n_exp, t).T, BF16)
    per_b = lambda shape: pl.BlockSpec((1,) + shape, lambda b_, i: (b_,) + (0,) * len(shape))
    return pl.pallas_call(
        functools.partial(_nsa_prompt_kernel, tq=tq, n_slc=n_slc),
        out_shape=jax.ShapeDtypeStruct((bsz, t, D_MODEL), F32),
        grid=(bsz, t // tq),
        in_specs=[pl.BlockSpec((1, tq, D_MODEL), lambda b_, i: (b_, i, 0)),
                  pl.BlockSpec((1, tq, 128), lambda b_, i: (b_, i, Z_SMALL // 128)),
                  per_b((2, ncmp, KV_WIDTH)),
                  per_b((t, KV_WIDTH)), per_b((A_KV_HEADS * V_ROWS, t)),
                  per_b((t, KV_WIDTH)), per_b((A_KV_HEADS * V_ROWS, t)),
                  pl.BlockSpec((n_slc, ncmp), lambda b_, i: (0, 0)),
                  pl.BlockSpec((t, n_exp), lambda b_, i: (0, 0))],
        out_specs=pl.BlockSpec((1, tq, D_MODEL), lambda b_, i: (b_, i, 0)),
        compiler_params=_cparams(("arbitrary", "arbitrary")),
        name="nsa_prompt",
    )(q_rot, z, cmp, kk(kv_slc), vt(kv_slc), kk(kv_win), vt(kv_win), cov_t, expand)


SAMPLE_ROWS = 8
NEW_KEYS = 128


def _nsa_sample_kernel(pt_ref, qbd_ref, gl_ref, cmp_ref, pool_ref, knew_ref, wcache_ref, wnew_ref, cov_ref, exp_ref,
                       o_ref, buf_ref, sem_ref, sel_ref, m_ref, l_ref, acc_ref, oc_ref,
                       *, n_chunks, pages, past, t_real, n_slc):
    c = pl.program_id(1)
    slot = _page_pipeline(pool_ref, lambda sl, k: buf_ref.at[sl, k], sem_ref, pt_ref, n_chunks, pages)
    qbd = qbd_ref[0]
    rows = qbd.shape[0]
    bpc = pages * PAGE_SIZE // SLC_BLOCK
    rq = lax.broadcasted_iota(jnp.int32, (rows, 1), 0) % t_real
    qpos = past + rq

    @pl.when(c == 0)
    def _():
        ncmp = cmp_ref.shape[2]
        nbp = cov_ref.shape[1]
        jrow = lax.broadcasted_iota(jnp.int32, (1, ncmp), 1)
        cmp_mask = (jrow >= 1) & (jrow * CMP_STRIDE + (CMP_BLOCK - CMP_STRIDE - 1) <= qpos)
        p = _masked_softmax_rows(_dot_nt(qbd, cmp_ref[0, 0]), cmp_mask)
        oc_ref[...] = _dot(p, cmp_ref[0, 1])
        ri = lax.broadcasted_iota(jnp.int32, (rows, rows), 0)
        ci = lax.broadcasted_iota(jnp.int32, (rows, rows), 1)
        group_rows = A_GROUP * t_real
        same = (ri // group_rows == ci // group_rows) & (ri % t_real == ci % t_real)
        p_group = _dot(jnp.where(same, 1.0, 0.0), p)
        imp = _dot(p_group, cov_ref[...])
        blk = lax.broadcasted_iota(jnp.int32, (1, nbp), 1)
        cur = qpos // SLC_BLOCK
        forced = (blk == 0) | (blk == cur) | (blk == cur - 1)
        future = blk * SLC_BLOCK > qpos
        score = jnp.where(future, -BIG, imp + jnp.where(forced, BIG, 0.0))
        score = jnp.where(blk < n_slc, score, NEG_INF)
        sel = _topk_mask(score, min(SLC_TOPK, n_slc), 1)
        for cc in range(n_chunks + 1):
            chunk_sel = sel[:, cc * bpc:(cc + 1) * bpc]
            if bpc < 128:
                chunk_sel = jnp.concatenate([chunk_sel, jnp.zeros((rows, 128 - bpc), F32)], axis=1)
            sel_ref[cc] = chunk_sel.astype(BF16)
        m_ref[...] = jnp.full(m_ref.shape, NEG_INF, F32)
        l_ref[...] = jnp.zeros_like(l_ref)
        acc_ref[...] = jnp.zeros_like(acc_ref)

    scores = jnp.concatenate([_dot(qbd, buf_ref[slot, k, 0:KV_WIDTH, :]) for k in range(pages)], axis=1)
    picked = jnp.dot(sel_ref[c], exp_ref[...], preferred_element_type=F32) > 0.5
    s = jnp.where(picked, scores, NEG_INF)

    def values(p):
        return sum(_dot_nt(p[:, k * PAGE_SIZE:(k + 1) * PAGE_SIZE], buf_ref[slot, k, KV_WIDTH:2 * KV_WIDTH, :])
                   for k in range(pages))

    carry = _flash_tile((m_ref[...], l_ref[...], acc_ref[...]), s, values)
    m_ref[...], l_ref[...], acc_ref[...] = carry

    @pl.when(c == n_chunks - 1)
    def _():
        zpad = jnp.zeros((NEW_KEYS - SAMPLE_ROWS, 2 * KV_WIDTH), F32)
        kcol = lax.broadcasted_iota(jnp.int32, (1, NEW_KEYS), 1)
        new_mask = (kcol <= rq) & (kcol < t_real)
        knew = jnp.concatenate([knew_ref[0], zpad], axis=0)
        last_picked = sel_ref[n_chunks][:, 0:1].astype(F32) > 0.5
        s_new = jnp.where(new_mask & last_picked, _dot_nt(qbd, knew[:, :KV_WIDTH]), NEG_INF)
        o_s = _flash_out(_flash_tile((m_ref[...], l_ref[...], acc_ref[...]), s_new, knew[:, KV_WIDTH:]))

        wc = wcache_ref[0]
        wb = wc.shape[0]
        wcol = lax.broadcasted_iota(jnp.int32, (1, wb), 1)
        s_w = jnp.where(wcol > rq + (wb - WINDOW), _dot_nt(qbd, wc[:, :KV_WIDTH]), NEG_INF)
        cw = _flash_tile(_flash_init(rows, KV_WIDTH), s_w, wc[:, KV_WIDTH:])
        wnew = jnp.concatenate([wnew_ref[0], zpad], axis=0)
        s_wn = jnp.where(new_mask, _dot_nt(qbd, wnew[:, :KV_WIDTH]), NEG_INF)
        o_w = _flash_out(_flash_tile(cw, s_wn, wnew[:, KV_WIDTH:]))

        gate = jax.nn.sigmoid(gl_ref[0])
        o = gate[:, 0:1] * oc_ref[...] + gate[:, 1:2] * o_s + gate[:, 2:3] * o_w
        lane_g = lax.broadcasted_iota(jnp.int32, (1, KV_WIDTH), 1) // A_HEAD_DIM
        row_g = lax.broadcasted_iota(jnp.int32, (rows, 1), 0) // (A_GROUP * t_real)
        o = jnp.where(lane_g == row_g, o, 0.0)
        o_ref[0] = sum(o[:, g * A_HEAD_DIM:(g + 1) * A_HEAD_DIM] for g in range(A_KV_HEADS))


def nsa_sample(q_rot, z, cmp, pool, page_table, k_new, win_cache, w_new, past, t_real):
    dbs = q_rot.shape[0]
    n_pages = page_table.shape[1]
    pages = math.gcd(n_pages, SLC_PAGES)
    n_chunks = n_pages // pages
    bpc = pages * PAGE_SIZE // SLC_BLOCK
    ncmp = cmp.shape[2]
    n_slc = -(-(past + t_real) // SLC_BLOCK)
    assert n_slc == n_chunks * bpc + 1 and bpc <= 128
    nbp = -(-((n_chunks + 1) * bpc) // 128) * 128
    rows = A_HEADS * t_real
    q5 = q_rot[:, :t_real].reshape(dbs, t_real, A_KV_HEADS, A_GROUP, A_HEAD_DIM) * (A_HEAD_DIM ** -0.5)
    qbd = jnp.einsum("bqgrd,gh->bgrqhd", q5, jnp.eye(A_KV_HEADS, dtype=F32)).reshape(dbs, rows, KV_WIDTH).astype(BF16)
    gl = z[:, :t_real, Z_SMALL + SMALL_AG:Z_SMALL + SMALL_AG + 3 * A_HEADS].reshape(dbs, t_real, A_HEADS, 3)
    gl = jnp.swapaxes(gl, 1, 2).reshape(dbs, rows, 3)
    cov = np.zeros((ncmp, nbp), np.float32)
    cov[:, :n_slc] = _coverage(ncmp, n_slc)
    expand = jnp.asarray(_block_expand(128, pages * PAGE_SIZE), BF16)
    per_b = lambda shape: pl.BlockSpec((1,) + shape, lambda b_, c, pt: (b_,) + (0,) * len(shape))
    const = lambda shape: pl.BlockSpec(shape, lambda b_, c, pt: (0,) * len(shape))
    wb = win_cache.shape[1]
    out = pl.pallas_call(
        functools.partial(_nsa_sample_kernel, n_chunks=n_chunks, pages=pages, past=past, t_real=t_real, n_slc=n_slc),
        out_shape=jax.ShapeDtypeStruct((dbs, rows, A_HEAD_DIM), F32),
        grid_spec=pltpu.PrefetchScalarGridSpec(
            num_scalar_prefetch=1,
            grid=(dbs, n_chunks),
            in_specs=[per_b((rows, KV_WIDTH)), per_b((rows, 3)), per_b((2, ncmp, KV_WIDTH)),
                      pl.BlockSpec(memory_space=pl.ANY),
                      per_b((SAMPLE_ROWS, 2 * KV_WIDTH)), per_b((wb, 2 * KV_WIDTH)), per_b((SAMPLE_ROWS, 2 * KV_WIDTH)),
                      const((ncmp, nbp)), const((128, pages * PAGE_SIZE))],
            out_specs=per_b((rows, A_HEAD_DIM)),
            scratch_shapes=[pltpu.VMEM((2, pages, 2 * KV_WIDTH, PAGE_SIZE), F32),
                            pltpu.SemaphoreType.DMA((2,)),
                            pltpu.VMEM((n_chunks + 1, rows, 128), BF16),
                            pltpu.VMEM((rows, 1), F32), pltpu.VMEM((rows, 1), F32),
                            pltpu.VMEM((rows, KV_WIDTH), F32), pltpu.VMEM((rows, KV_WIDTH), F32)]),
        compiler_params=_cparams(("arbitrary", "arbitrary")),
        name="nsa_sample",
    )(page_table, qbd, gl, cmp, pool, k_new, win_cache, w_new, jnp.asarray(cov, BF16), expand)
    out = jnp.swapaxes(out.reshape(dbs, A_HEADS, t_real, A_HEAD_DIM), 1, 2).reshape(dbs, t_real, D_MODEL)
    return jnp.pad(out, ((0, 0), (0, SAMPLE_ROWS - t_real), (0, 0)))


def _kv_rows(a, bsz, t):
    return a.reshape(bsz, t, 2, A_KV_HEADS, A_HEAD_DIM)


def kernel(x_prompt, x_sample, cache_cmp_kv, cache_slc_kv, cache_win_kv, state_C, state_n, state_m, state_conv,
           page_table, c_prompt, c_sample, w_ada, b_ada, g_pre_mix, g_post_mix, g_pre_ffn, g_post_ffn, w_in, b_in,
           m_norm_w, cmp_w1, cmp_pe, cmp_w2, w_branch_m, w_branch_a, w_out, w_up, conv_w, conv_b, w_down):
    depth = w_ada.shape[0]
    bsz, t, d = x_prompt.shape
    dbs, ts, _ = x_sample.shape
    n_pages = page_table.shape[1]
    past = n_pages * PAGE_SIZE
    assert ts <= SAMPLE_ROWS and (past + ts) // CMP_STRIDE == past // CMP_STRIDE and past >= WINDOW
    srows = dbs * SAMPLE_ROWS
    xp = x_prompt.astype(F32)
    xs = jnp.pad(x_sample.astype(F32), ((0, 0), (0, SAMPLE_ROWS - ts), (0, 0))).reshape(1, srows, d)
    c_all = jnp.concatenate([c_prompt, c_sample], axis=0).astype(F32)
    c_all = jnp.pad(c_all, ((0, (-c_all.shape[0]) % 8), (0, 0)))
    tab_p = rope_tables(jnp.arange(t, dtype=jnp.int32))
    tab_s = rope_tables(jnp.tile(past + jnp.arange(SAMPLE_ROWS, dtype=jnp.int32), dbs))
    lchunk = math.gcd(t, 256)
    tm_p = math.gcd(t, 512)
    p_states, s_states = [], []
    for l in range(depth):
        mod = ada_modulation(c_all, w_ada[l], b_ada[l])
        mod_p = [m[:, None, :] for m in jnp.split(mod[:bsz], 6, axis=-1)]
        mod_s = [jnp.repeat(m, SAMPLE_ROWS, axis=0)[None] for m in jnp.split(mod[bsz:bsz + dbs], 6, axis=-1)]
        w_r, b_r = regroup_in_weights(w_in[l], b_in[l])
        cw = compress_weights(cmp_w1[l], cmp_pe[l], cmp_w2[l])
        wm, wa, wo = w_branch_m[l].astype(BF16), w_branch_a[l].astype(BF16), w_out[l].astype(BF16)
        wu, wd = w_up[l].astype(BF16), w_down[l].astype(BF16)

        sh_m, sc_m, gt_m, sh_f, sc_f, gt_f = mod_p
        z = in_projection(xp, g_pre_mix[l], sc_m, sh_m, w_r, b_r, tm=math.gcd(t, 1024))
        q_rot, kv_cmp, kv_slc, kv_win = rope_split(z, tab_p, tm=tm_p)
        hm, p_c, p_n, p_m = mlstm(z, m_norm_w[l], jnp.zeros((bsz, M_HEADS, M_HEAD_DIM, M_HEAD_DIM), F32),
                                  jnp.zeros((bsz, M_HEADS, M_HEAD_DIM), F32), jnp.zeros((bsz, M_HEADS), F32),
                                  lb=lchunk, lp=lchunk, t_real=lchunk)
        ha = nsa_prompt(q_rot, z, compress_prompt(kv_cmp, cw), kv_slc, kv_win, tq=256)
        xp = mix_out(xp, hm, ha, z, gt_m, g_post_mix[l], wm, wa, wo, tm=tm_p)
        xp, p_conv = conv_ffn_rows(xp, g_pre_ffn[l], sc_f, sh_f, gt_f, g_post_ffn[l], wu, conv_w[l], conv_b[l], wd,
                                   tm=tm_p)
        wkeep = min(WINDOW, t)
        p_states.append((_kv_rows(kv_cmp, bsz, t), _kv_rows(kv_slc, bsz, t), _kv_rows(kv_win[:, t - wkeep:], bsz, wkeep),
                         p_c, p_n, p_m, p_conv))

        sh_m, sc_m, gt_m, sh_f, sc_f, gt_f = mod_s
        z = in_projection(xs, g_pre_mix[l], sc_m, sh_m, w_r, b_r, tm=srows)
        q_rot, kv_cmp, kv_slc, kv_win = rope_split(z, tab_s, tm=srows)
        z3 = z.reshape(dbs, SAMPLE_ROWS, Z_WIDTH)
        hm, s_c, s_n, s_m = mlstm(z3, m_norm_w[l], state_C[l].astype(F32), state_n[l].astype(F32),
                                  state_m[l].astype(F32), lb=SAMPLE_ROWS, lp=128, t_real=ts)
        cmp_s = compress_paged(feature_major_pool(cache_cmp_kv[l].astype(F32)), page_table, cw)
        new3 = lambda a: a.reshape(dbs, SAMPLE_ROWS, 2 * KV_WIDTH)
        win_cache = cache_win_kv[l].astype(F32).reshape(dbs, -1, 2 * KV_WIDTH)
        ha = nsa_sample(q_rot.reshape(dbs, SAMPLE_ROWS, d), z3, cmp_s, feature_major_pool(cache_slc_kv[l].astype(F32)),
                        page_table, new3(kv_slc), win_cache, new3(kv_win), past, ts)
        xs = mix_out(xs, hm.reshape(1, srows, d), ha.reshape(1, srows, d), z, gt_m, g_post_mix[l], wm, wa, wo, tm=srows)
        st = state_conv[l].astype(F32)
        s2 = jnp.pad(st, ((0, 0), (0, SAMPLE_ROWS - (CONV_W - 1)), (0, 0))).reshape(1, srows, 2 * D_FF)
        s1 = jnp.pad(st[:, 1:], ((0, 0), (0, SAMPLE_ROWS - 1), (0, 0))).reshape(1, srows, 2 * D_FF)
        xs, u = conv_ffn(xs, g_pre_ffn[l], sc_f, sh_f, gt_f, g_post_ffn[l], wu, conv_w[l], conv_b[l], wd, tm=srows,
                         state_rows=(s1, s2))
        wb = win_cache.shape[1]
        s_win = jnp.concatenate([win_cache, new3(kv_win)[:, :ts]], axis=1)[:, ts:]
        s_conv = jnp.concatenate([st, u.reshape(dbs, SAMPLE_ROWS, 2 * D_FF)[:, :ts]], axis=1)[:, ts:]
        s_states.append((_kv_rows(new3(kv_cmp)[:, :ts], dbs, ts), _kv_rows(new3(kv_slc)[:, :ts], dbs, ts),
                         _kv_rows(s_win, dbs, wb), s_c, s_n, s_m, s_conv))

    stack = lambda states: [jnp.stack([s[i] for s in states]) for i in range(7)]
    y_sample = xs.reshape(dbs, SAMPLE_ROWS, d)[:, :ts]
    return (xp, y_sample, *stack(p_states), *stack(s_states))
```

```python
import functools
import math

import numpy as np
import jax
import jax.numpy as jnp
from jax import lax
from jax.experimental import pallas as pl
from jax.experimental.pallas import tpu as pltpu

F32 = jnp.float32
BF16 = jnp.bfloat16

D_MODEL = 1024
M_HEADS = 4
M_HEAD_DIM = 256
A_HEADS = 16
A_HEAD_DIM = 64
A_KV_HEADS = 4
A_GROUP = 4
KV_WIDTH = A_KV_HEADS * A_HEAD_DIM
CMP_STRIDE = 16
CMP_BLOCK = 32
SLC_BLOCK = 64
SLC_TOPK = 16
WINDOW = 512
ROPE_THETA = 500000.0
ROPE_DIM = 16
BIG = 1e6
D_FF = 2816
CONV_W = 3
NORM_EPS = 1e-6
PAGE_SIZE = 128
NEG_INF = float("-inf")
TINY = float(np.finfo(np.float32).tiny)

Z_MQ, Z_MK, Z_MV, Z_MO, Z_AQ, Z_GA, Z_GB, Z_AKV, Z_SMALL = 0, 1024, 2048, 3072, 4096, 5120, 6144, 7168, 8704
Z_WIDTH = 9216
SMALL_MI, SMALL_MF, SMALL_AG = 0, 4, 8

VMEM_LIMIT = 48 * 1024 * 1024


def _cparams(sem):
    return pltpu.CompilerParams(dimension_semantics=sem, vmem_limit_bytes=VMEM_LIMIT)


def _dot(a, b):
    return jnp.dot(a.astype(BF16), b.astype(BF16), preferred_element_type=F32)


def _dot_nt(a, b):
    return lax.dot_general(a.astype(BF16), b.astype(BF16), (((1,), (1,)), ((), ())), preferred_element_type=F32)


def _dot_tn(a, b):
    return lax.dot_general(a.astype(BF16), b.astype(BF16), (((0,), (0,)), ((), ())), preferred_element_type=F32)


def _split3(x):
    x1 = x.astype(BF16)
    r1 = x - x1.astype(F32)
    x2 = r1.astype(BF16)
    x3 = (r1 - x2.astype(F32)).astype(BF16)
    return x1, x2, x3


def _rms(x, g):
    return x * lax.rsqrt(jnp.mean(x * x, axis=-1, keepdims=True) + NORM_EPS) * g


def _ada_kernel(c_ref, w_ref, b_ref, o_ref):
    c = c_ref[...]
    o_ref[...] = _dot(c * jax.nn.sigmoid(c), w_ref[...]) + b_ref[...]


def ada_modulation(c, w_ada, b_ada):
    rows, d = c.shape
    n = w_ada.shape[1]
    tn = 512
    return pl.pallas_call(
        _ada_kernel,
        out_shape=jax.ShapeDtypeStruct((rows, n), F32),
        grid=(n // tn,),
        in_specs=[pl.BlockSpec((rows, d), lambda j: (0, 0)),
                  pl.BlockSpec((d, tn), lambda j: (0, j)),
                  pl.BlockSpec((1, tn), lambda j: (0, j))],
        out_specs=pl.BlockSpec((rows, tn), lambda j: (0, j)),
        compiler_params=_cparams(("arbitrary",)),
        name="ada",
    )(c, w_ada, b_ada.reshape(1, n))


def _inproj_kernel(x_ref, g_ref, sc_ref, sh_ref, w_ref, b_ref, o_ref, h_ref):
    @pl.when(pl.program_id(2) == 0)
    def _():
        h = _rms(x_ref[0], g_ref[...]) * (1.0 + sc_ref[0]) + sh_ref[0]
        h_ref[...] = h.astype(BF16)

    o_ref[0] = jnp.dot(h_ref[...], w_ref[...], preferred_element_type=F32) + b_ref[...]


def in_projection(x, g, sc, sh, w_bf16, b, tm):
    bsz, t, d = x.shape
    r = sc.shape[1]
    rb = 1 if r == 1 else tm
    tn = 1024
    mod_spec = pl.BlockSpec((1, rb, d), (lambda b_, i, j: (b_, 0, 0)) if r == 1 else (lambda b_, i, j: (b_, i, 0)))
    return pl.pallas_call(
        _inproj_kernel,
        out_shape=jax.ShapeDtypeStruct((bsz, t, Z_WIDTH), F32),
        grid=(bsz, t // tm, Z_WIDTH // tn),
        in_specs=[pl.BlockSpec((1, tm, d), lambda b_, i, j: (b_, i, 0)),
                  pl.BlockSpec((1, d), lambda b_, i, j: (0, 0)),
                  mod_spec, mod_spec,
                  pl.BlockSpec((d, tn), lambda b_, i, j: (0, j)),
                  pl.BlockSpec((1, tn), lambda b_, i, j: (0, j))],
        out_specs=pl.BlockSpec((1, tm, tn), lambda b_, i, j: (b_, i, j)),
        scratch_shapes=[pltpu.VMEM((tm, d), BF16)],
        compiler_params=_cparams(("arbitrary", "arbitrary", "arbitrary")),
        name="inproj",
    )(x, g.reshape(1, d), sc, sh, w_bf16, b.reshape(1, Z_WIDTH))


def regroup_in_weights(w_in, b_in):
    mw = M_HEADS * M_HEAD_DIM
    o_mi = 4 * mw
    o_aq = o_mi + 2 * M_HEADS
    o_akv = o_aq + A_HEADS * A_HEAD_DIM
    o_ag = o_akv + 6 * KV_WIDTH
    o_ga = o_ag + 3 * A_HEADS
    o_gb = o_ga + D_MODEL

    def regroup(a):
        lead = a.shape[:-1]
        parts = [a[..., :o_mi], a[..., o_aq:o_akv], a[..., o_ga:o_gb], a[..., o_gb:o_gb + D_MODEL],
                 a[..., o_akv:o_ag], a[..., o_mi:o_aq], a[..., o_ag:o_ga],
                 jnp.zeros(lead + (128 - 2 * M_HEADS - 3 * A_HEADS,), a.dtype),
                 jnp.zeros(lead + (Z_WIDTH - Z_SMALL - 128,), a.dtype)]
        return jnp.concatenate(parts, axis=-1)

    return regroup(w_in).astype(BF16), regroup(b_in)


def rope_tables(pos):
    half = ROPE_DIM // 2
    inv_freq = ROPE_THETA ** (-jnp.arange(half, dtype=F32) / half)
    ang = pos.astype(F32)[:, None] * inv_freq
    cos, sin = jnp.cos(ang), jnp.sin(ang)
    rows = pos.shape[0]
    zeros = jnp.zeros((rows, half), F32)
    rest1 = jnp.ones((rows, A_HEAD_DIM - ROPE_DIM), F32)
    rest0 = jnp.zeros((rows, A_HEAD_DIM - ROPE_DIM), F32)
    c = jnp.concatenate([cos, cos, rest1], axis=1)
    sa = jnp.concatenate([zeros, sin, rest0], axis=1)
    sb = jnp.concatenate([-sin, zeros, rest0], axis=1)
    return tuple(jnp.concatenate([a, a], axis=1) for a in (c, sa, sb))


def _rope_apply(x, c, sa, sb):
    w = x.shape[1]
    n = w // 128
    ct, sat, sbt = (jnp.concatenate([a] * n, axis=1) for a in (c, sa, sb))
    return x * ct + pltpu.roll(x, ROPE_DIM // 2, 1) * sat + pltpu.roll(x, w - ROPE_DIM // 2, 1) * sbt


def _rope_kernel(q_ref, c_ref, s_ref, w_ref, cos_ref, sa_ref, sb_ref, qo_ref, co_ref, so_ref, wo_ref):
    c, sa, sb = cos_ref[...], sa_ref[...], sb_ref[...]
    qo_ref[0] = _rope_apply(q_ref[0], c, sa, sb)
    for src, dst in ((c_ref, co_ref), (s_ref, so_ref), (w_ref, wo_ref)):
        kv = src[0]
        dst[0] = jnp.concatenate([_rope_apply(kv[:, :KV_WIDTH], c, sa, sb), kv[:, KV_WIDTH:]], axis=1)


def rope_split(z, tables, tm):
    bsz, t, _ = z.shape
    nt = t // tm
    kvw = 2 * KV_WIDTH
    tab_spec = pl.BlockSpec((tm, 128), lambda b_, i: (i, 0))
    return pl.pallas_call(
        _rope_kernel,
        out_shape=(jax.ShapeDtypeStruct((bsz, t, D_MODEL), F32),) + (jax.ShapeDtypeStruct((bsz, t, kvw), F32),) * 3,
        grid=(bsz, nt),
        in_specs=[pl.BlockSpec((1, tm, D_MODEL), lambda b_, i: (b_, i, Z_AQ // D_MODEL)),
                  pl.BlockSpec((1, tm, kvw), lambda b_, i: (b_, i, Z_AKV // kvw)),
                  pl.BlockSpec((1, tm, kvw), lambda b_, i: (b_, i, Z_AKV // kvw + 1)),
                  pl.BlockSpec((1, tm, kvw), lambda b_, i: (b_, i, Z_AKV // kvw + 2)),
                  tab_spec, tab_spec, tab_spec],
        out_specs=(pl.BlockSpec((1, tm, D_MODEL), lambda b_, i: (b_, i, 0)),) +
                  (pl.BlockSpec((1, tm, kvw), lambda b_, i: (b_, i, 0)),) * 3,
        compiler_params=_cparams(("arbitrary", "arbitrary")),
        name="rope",
    )(z, z, z, z, *tables)


def _mlstm_kernel(q_ref, k_ref, v_ref, o_ref, s_ref, nw_ref, c0_ref, n0_ref, m0_ref,
                  h_ref, c_ref, n_ref, m_ref, *, lb, lp, t_real):
    @pl.when(pl.program_id(1) == 0)
    def _():
        c_ref[...] = c0_ref[...]
        n_ref[...] = n0_ref[...]
        m_ref[...] = m0_ref[...]

    def pad(a):
        if lb == lp:
            return a
        return jnp.concatenate([a, jnp.zeros((lp - lb, a.shape[1]), a.dtype)], axis=0)

    small = pad(s_ref[0])
    small_t = small.T
    row_c = lax.broadcasted_iota(jnp.int32, (lp, 1), 0)
    row_r = lax.broadcasted_iota(jnp.int32, (1, lp), 1)
    li_col_all = jnp.where(row_c < t_real, small, NEG_INF)
    lf_col_all = jnp.where(row_c < t_real, jax.nn.log_sigmoid(small), 0.0)
    li_row_all = jnp.where(row_r < t_real, small_t[0:8], NEG_INF)
    lf_row_all = jnp.where(row_r < t_real, jax.nn.log_sigmoid(small_t[0:8]), 0.0)
    rr = lax.broadcasted_iota(jnp.int32, (lp, lp), 0)
    cc = lax.broadcasted_iota(jnp.int32, (lp, lp), 1)
    causal = cc <= rr
    tril = jnp.where(causal, 1.0, 0.0).astype(BF16)
    triu = jnp.where(rr <= cc, 1.0, 0.0).astype(BF16)
    b_col_all = sum(jnp.dot(tril, p, preferred_element_type=F32) for p in _split3(lf_col_all))
    b_row_all = sum(jnp.dot(p, triu, preferred_element_type=F32) for p in _split3(lf_row_all))

    q_all, k_all, v_all, o_all = pad(q_ref[0]), pad(k_ref[0]), pad(v_ref[0]), pad(o_ref[0])
    nw = nw_ref[...]
    for h in range(M_HEADS):
        hs = slice(h * M_HEAD_DIM, (h + 1) * M_HEAD_DIM)
        qf = q_all[:, hs]
        kf = k_all[:, hs] * (M_HEAD_DIM ** -0.5)
        vf = v_all[:, hs]
        li_row = li_row_all[SMALL_MI + h:SMALL_MI + h + 1, :]
        b_row = b_row_all[SMALL_MF + h:SMALL_MF + h + 1, :]
        li_col = li_col_all[:, SMALL_MI + h:SMALL_MI + h + 1]
        b_col = b_col_all[:, SMALL_MF + h:SMALL_MF + h + 1]
        m_prev = m_ref[0, h]
        c_prev = c_ref[0, h]
        n_prev = n_ref[0, h]

        dlog = jnp.where(causal, b_col - b_row + li_row, NEG_INF)
        inter = m_prev + b_col
        mt = jnp.maximum(inter, jnp.max(dlog, axis=1, keepdims=True))
        a = jnp.exp(inter - mt)
        s = _dot_nt(qf, kf) * jnp.exp(dlog - mt)
        num = a * _dot_nt(qf, c_prev) + _dot(s, vf)
        den = a * jnp.sum(qf * n_prev, axis=1, keepdims=True) + jnp.sum(s, axis=1, keepdims=True)
        hh = num / jnp.maximum(jnp.abs(den), jnp.exp(-mt))
        mu = jnp.mean(hh, axis=1, keepdims=True)
        var = jnp.mean(jnp.square(hh - mu), axis=1, keepdims=True)
        out = (hh - mu) * lax.rsqrt(var + NORM_EPS) * nw[:, hs] * jax.nn.sigmoid(o_all[:, hs])
        h_ref[0, :, hs] = out[:lb]

        bl = b_row[:, lp - 1:lp]
        wlog = bl - b_col + li_col
        m_new = jnp.maximum(m_prev + bl, jnp.max(wlog, axis=0, keepdims=True))
        w = jnp.exp(wlog - m_new)
        decay = jnp.exp(m_prev + bl - m_new)
        c_ref[0, h] = decay * c_prev + _dot_tn(vf * w, kf)
        n_ref[0, h] = decay * n_prev + jnp.sum(w * kf, axis=0, keepdims=True)
        m_ref[0, h] = m_new


def mlstm(z, norm_w, c0, n0, m0, lb, lp, t_real):
    bsz, t, _ = z.shape
    nc = t // lb
    mw = M_HEADS * M_HEAD_DIM
    zspec = lambda col: pl.BlockSpec((1, lb, mw), lambda b_, c: (b_, c, col // mw))
    cst = lambda shape: pl.BlockSpec((1,) + shape, lambda b_, c: (b_,) + (0,) * len(shape))
    h, c, n, m = pl.pallas_call(
        functools.partial(_mlstm_kernel, lb=lb, lp=lp, t_real=t_real),
        out_shape=(jax.ShapeDtypeStruct((bsz, t, mw), F32),
                   jax.ShapeDtypeStruct((bsz, M_HEADS, M_HEAD_DIM, M_HEAD_DIM), F32),
                   jax.ShapeDtypeStruct((bsz, M_HEADS, 1, M_HEAD_DIM), F32),
                   jax.ShapeDtypeStruct((bsz, M_HEADS, 1, 1), F32)),
        grid=(bsz, nc),
        in_specs=[zspec(Z_MQ), zspec(Z_MK), zspec(Z_MV), zspec(Z_MO),
                  pl.BlockSpec((1, lb, 128), lambda b_, c: (b_, c, Z_SMALL // 128)),
                  pl.BlockSpec((1, mw), lambda b_, c: (0, 0)),
                  cst((M_HEADS, M_HEAD_DIM, M_HEAD_DIM)), cst((M_HEADS, 1, M_HEAD_DIM)), cst((M_HEADS, 1, 1))],
        out_specs=(pl.BlockSpec((1, lb, mw), lambda b_, c: (b_, c, 0)),
                   cst((M_HEADS, M_HEAD_DIM, M_HEAD_DIM)), cst((M_HEADS, 1, M_HEAD_DIM)), cst((M_HEADS, 1, 1))),
        compiler_params=_cparams(("arbitrary", "arbitrary")),
        name="mlstm",
    )(z, z, z, z, z, norm_w.reshape(1, mw), c0, n0.reshape(bsz, M_HEADS, 1, M_HEAD_DIM),
      m0.reshape(bsz, M_HEADS, 1, 1))
    return h, c, n.reshape(bsz, M_HEADS, M_HEAD_DIM), m.reshape(bsz, M_HEADS)


def _mix_kernel(x_ref, hm_ref, ha_ref, ga_ref, gb_ref, gt_ref, g_ref, wm_ref, wa_ref, wo_ref, o_ref):
    mixed = (jax.nn.sigmoid(ga_ref[0]) * _dot(hm_ref[0], wm_ref[...]) +
             jax.nn.sigmoid(gb_ref[0]) * _dot(ha_ref[0], wa_ref[...]))
    o_ref[0] = x_ref[0] + gt_ref[0] * _rms(_dot(mixed, wo_ref[...]), g_ref[...])


def mix_out(x, hm, ha, z, gt, g_post, wm, wa, wo, tm):
    bsz, t, d = x.shape
    r = gt.shape[1]
    rb = 1 if r == 1 else tm
    row = lambda col=0: pl.BlockSpec((1, tm, d), lambda b_, i: (b_, i, col // d))
    mod = pl.BlockSpec((1, rb, d), (lambda b_, i: (b_, 0, 0)) if r == 1 else (lambda b_, i: (b_, i, 0)))
    wsp = pl.BlockSpec((d, d), lambda b_, i: (0, 0))
    return pl.pallas_call(
        _mix_kernel,
        out_shape=jax.ShapeDtypeStruct((bsz, t, d), F32),
        grid=(bsz, t // tm),
        in_specs=[row(), row(), row(), row(Z_GA), row(Z_GB), mod,
                  pl.BlockSpec((1, d), lambda b_, i: (0, 0)), wsp, wsp, wsp],
        out_specs=row(),
        compiler_params=_cparams(("arbitrary", "arbitrary")),
        name="mix",
    )(x, hm, ha, z, z, gt, g_post.reshape(1, d), wm, wa, wo)


FF_CHUNK = 256


def _ffn_kernel(x_ref, g_ref, sc_ref, sh_ref, gt_ref, gp_ref, wa_ref, wg_ref, cwa_ref, cwg_ref, cba_ref, cbg_ref,
                wd_ref, s1a_ref, s1g_ref, s2a_ref, s2g_ref, y_ref, ua_ref, ug_ref, h_ref, acc_ref):
    f = pl.program_id(2)
    tm = x_ref.shape[1]

    @pl.when(f == 0)
    def _():
        h = _rms(x_ref[0], g_ref[...]) * (1.0 + sc_ref[0]) + sh_ref[0]
        h_ref[...] = h.astype(BF16)
        acc_ref[...] = jnp.zeros_like(acc_ref)

    t = lax.broadcasted_iota(jnp.int32, (tm, 1), 0) % SAMPLE_ROWS

    def branch(w_ref, cw_ref, cb_ref, s1_ref, s2_ref):
        u = jnp.dot(h_ref[...], w_ref[...], preferred_element_type=F32)
        u1 = jnp.where(t < 1, s1_ref[0], pltpu.roll(u, 1, 0))
        u2 = jnp.where(t < 2, s2_ref[0], pltpu.roll(u, 2, 0))
        cw = cw_ref[...]
        return u, cb_ref[...] + cw[0:1] * u2 + cw[1:2] * u1 + cw[2:3] * u

    ua_ref[0], conv_a = branch(wa_ref, cwa_ref, cba_ref, s1a_ref, s2a_ref)
    ug_ref[0], conv_g = branch(wg_ref, cwg_ref, cbg_ref, s1g_ref, s2g_ref)
    acc_ref[...] += _dot(jax.nn.gelu(conv_g) * conv_a, wd_ref[...])

    @pl.when(f == pl.num_programs(2) - 1)
    def _():
        y_ref[0] = x_ref[0] + gt_ref[0] * _rms(acc_ref[...], gp_ref[...])


def conv_ffn(x, g_pre, sc, sh, gt, g_post, w_up, conv_w, conv_b, w_down, tm, state_rows):
    bsz, t, d = x.shape
    ck = FF_CHUNK
    nf = D_FF // ck
    xrow = pl.BlockSpec((1, tm, d), lambda b_, i, f: (b_, i, 0))
    vec = pl.BlockSpec((1, d), lambda b_, i, f: (0, 0))
    col_a = lambda rows: pl.BlockSpec((rows, ck), lambda b_, i, f: (0, f))
    col_g = lambda rows: pl.BlockSpec((rows, ck), lambda b_, i, f: (0, nf + f))
    st_a = pl.BlockSpec((1, tm, ck), lambda b_, i, f: (b_, i, f))
    st_g = pl.BlockSpec((1, tm, ck), lambda b_, i, f: (b_, i, nf + f))
    cb = conv_b.reshape(1, 2 * D_FF)
    y, ua, ug = pl.pallas_call(
        _ffn_kernel,
        out_shape=(jax.ShapeDtypeStruct((bsz, t, d), F32),) + (jax.ShapeDtypeStruct((bsz, t, D_FF), F32),) * 2,
        grid=(bsz, t // tm, nf),
        in_specs=[xrow, vec, xrow, xrow, xrow, vec, col_a(d), col_g(d), col_a(CONV_W), col_g(CONV_W), col_a(1), col_g(1),
                  pl.BlockSpec((ck, d), lambda b_, i, f: (f, 0)), st_a, st_g, st_a, st_g],
        out_specs=(xrow, st_a, st_a),
        scratch_shapes=[pltpu.VMEM((tm, d), BF16), pltpu.VMEM((tm, d), F32)],
        compiler_params=_cparams(("arbitrary", "arbitrary", "arbitrary")),
        name="ffn",
    )(x, g_pre.reshape(1, d), sc, sh, gt, g_post.reshape(1, d), w_up, w_up, conv_w, conv_w, cb, cb, w_down,
      state_rows[0], state_rows[0], state_rows[1], state_rows[1])
    return y, jnp.concatenate([ua, ug], axis=-1)


def _ffn_rows_kernel(x_ref, g_ref, sc_ref, sh_ref, gt_ref, gp_ref, wu_ref, cw_ref, cb_ref, wd_ref,
                     y_ref, tail_ref, carry_ref):
    tm = x_ref.shape[1]
    ck = FF_CHUNK

    @pl.when(pl.program_id(1) == 0)
    def _():
        carry_ref[...] = jnp.zeros_like(carry_ref)

    h = (_rms(x_ref[0], g_ref[...]) * (1.0 + sc_ref[0]) + sh_ref[0]).astype(BF16)
    top = lax.broadcasted_iota(jnp.int32, (8, 1), 0)
    acc = jnp.zeros((tm, x_ref.shape[2]), F32)

    def conv(cols):
        u = jnp.dot(h, wu_ref[:, cols], preferred_element_type=F32)
        r1 = pltpu.roll(u, 1, 0)
        r2 = pltpu.roll(u, 2, 0)
        prev = carry_ref[:, cols]
        u1 = jnp.concatenate([jnp.where(top < 1, prev[1:2], r1[0:8]), r1[8:]], axis=0)
        u2 = jnp.concatenate([jnp.where(top < 1, prev[0:1], jnp.where(top < 2, prev[1:2], r2[0:8])), r2[8:]], axis=0)
        carry_ref[0:2, cols] = u[tm - 2:tm]
        tail_ref[0, 0, :, cols] = u[tm - 2:tm]
        cw = cw_ref[:, cols]
        return cb_ref[:, cols] + cw[0:1] * u2 + cw[1:2] * u1 + cw[2:3] * u

    for c in range(D_FF // ck):
        conv_a = conv(slice(c * ck, (c + 1) * ck))
        conv_g = conv(slice(D_FF + c * ck, D_FF + (c + 1) * ck))
        acc = acc + _dot(jax.nn.gelu(conv_g) * conv_a, wd_ref[c * ck:(c + 1) * ck, :])
    y_ref[0] = x_ref[0] + gt_ref[0] * _rms(acc, gp_ref[...])


def conv_ffn_rows(x, g_pre, sc, sh, gt, g_post, w_up, conv_w, conv_b, w_down, tm):
    bsz, t, d = x.shape
    mod = pl.BlockSpec((1, 1, d), lambda b_, i: (b_, 0, 0))
    xrow = pl.BlockSpec((1, tm, d), lambda b_, i: (b_, i, 0))
    whole = lambda a: pl.BlockSpec(a.shape, lambda b_, i: (0,) * a.ndim, pipeline_mode=pl.Buffered(1))
    cb = conv_b.reshape(1, 2 * D_FF)
    g1, g2 = g_pre.reshape(1, d), g_post.reshape(1, d)
    y, tail = pl.pallas_call(
        _ffn_rows_kernel,
        out_shape=(jax.ShapeDtypeStruct((bsz, t, d), F32),
                   jax.ShapeDtypeStruct((bsz, t // tm, CONV_W - 1, 2 * D_FF), F32)),
        grid=(bsz, t // tm),
        in_specs=[xrow, whole(g1), mod, mod, mod, whole(g2), whole(w_up), whole(conv_w), whole(cb), whole(w_down)],
        out_specs=(xrow, pl.BlockSpec((1, 1, CONV_W - 1, 2 * D_FF), lambda b_, i: (b_, i, 0, 0))),
        scratch_shapes=[pltpu.VMEM((8, 2 * D_FF), F32)],
        compiler_params=_cparams(("arbitrary", "arbitrary")),
        name="ffn_rows",
    )(x, g1, sc, sh, gt, g2, w_up, conv_w, cb, w_down)
    return y, tail[:, -1]


LANE = 128
QUARTERS = 2 * KV_WIDTH // LANE
HEADS_PER_LANE_ROW = LANE // A_HEAD_DIM


def compress_weights(cmp_w1, cmp_pe, cmp_w2):
    eye = jnp.eye(HEADS_PER_LANE_ROW, dtype=F32)
    bd = lambda w: jnp.einsum("gh,...de->...gdhe", eye, w).reshape(w.shape[:-2] + (LANE, LANE))
    w1ab = jnp.concatenate([bd(cmp_w1[:, :CMP_STRIDE]), bd(cmp_w1[:, CMP_STRIDE:])], axis=-1).astype(BF16)
    w1ab = w1ab.reshape(2, CMP_STRIDE // 2, 2 * LANE, 2 * LANE)
    w2 = bd(cmp_w2).astype(BF16)
    w1r = cmp_w1.reshape(2, CMP_BLOCK * A_HEAD_DIM, A_HEAD_DIM)
    pe = cmp_pe.reshape(2, CMP_BLOCK * A_HEAD_DIM, 1)
    return w1ab, w2, w1r, pe


def _compress_chunk(get_x, nrows, w1_ref, w2_ref, w1r_ref, pe_ref, carry_ref, out_ref):
    row = lax.broadcasted_iota(jnp.int32, (nrows, 1), 0)
    for kind in range(2):
        peb = jnp.sum(pe_ref[kind] * w1r_ref[kind], axis=0, keepdims=True)
        peb = jnp.concatenate([peb] * HEADS_PER_LANE_ROW, axis=1)
        acc2 = jnp.zeros((2 * nrows, 2 * LANE), F32)
        for j in range(CMP_STRIDE // 2):
            x = jnp.concatenate([get_x(2 * kind, j), get_x(2 * kind + 1, j)], axis=0).astype(BF16)
            acc2 = acc2 + jnp.dot(x, w1_ref[kind, j], preferred_element_type=F32)
        for half in range(2):
            acc = acc2[half * nrows:(half + 1) * nrows]
            acc_a, acc_b = acc[:, :LANE], acc[:, LANE:]
            a_shift = jnp.where(row == 0, carry_ref[kind, half], pltpu.roll(acc_a, 1, 0))
            carry_ref[kind, half] = acc_a[nrows - 1:nrows]
            hid = jax.nn.gelu(a_shift + acc_b + peb)
            out_ref[0, kind, :, half * LANE:(half + 1) * LANE] = _dot(hid, w2_ref[kind])


def _cmp_prompt_kernel(kv0_ref, kv1_ref, kv2_ref, kv3_ref, w1_ref, w2_ref, w1r_ref, pe_ref, out_ref, carry_ref):
    kv_refs = (kv0_ref, kv1_ref, kv2_ref, kv3_ref)
    nseg = kv0_ref.shape[1] // CMP_STRIDE
    carry_ref[...] = jnp.zeros_like(carry_ref)
    offset = lambda quarter, s: kv_refs[quarter][0, pl.ds(s, nseg, stride=CMP_STRIDE), :]
    get_x = lambda quarter, j: jnp.concatenate([offset(quarter, 2 * j), offset(quarter, 2 * j + 1)], axis=1)
    _compress_chunk(get_x, nseg, w1_ref, w2_ref, w1r_ref, pe_ref, carry_ref, out_ref)


def _cmp_weight_specs():
    zero = lambda n: (lambda *_: (0,) * n)
    return [pl.BlockSpec((2, CMP_STRIDE // 2, 2 * LANE, 2 * LANE), zero(4)),
            pl.BlockSpec((2, LANE, LANE), zero(3)),
            pl.BlockSpec((2, CMP_BLOCK * A_HEAD_DIM, A_HEAD_DIM), zero(3)),
            pl.BlockSpec((2, CMP_BLOCK * A_HEAD_DIM, 1), zero(3))]


def compress_prompt(kv_cmp, cw):
    bsz, t, w = kv_cmp.shape
    nseg = t // CMP_STRIDE
    return pl.pallas_call(
        _cmp_prompt_kernel,
        out_shape=jax.ShapeDtypeStruct((bsz, 2, nseg, KV_WIDTH), F32),
        grid=(bsz,),
        in_specs=[pl.BlockSpec((1, t, LANE), functools.partial(lambda q, b_: (b_, 0, q), q)) for q in range(QUARTERS)]
                 + _cmp_weight_specs(),
        out_specs=pl.BlockSpec((1, 2, nseg, KV_WIDTH), lambda b_: (b_, 0, 0, 0)),
        scratch_shapes=[pltpu.VMEM((2, 2, 1, LANE), F32)],
        compiler_params=_cparams(("arbitrary",)),
        name="cmp_prompt",
    )(*([kv_cmp] * QUARTERS), *cw)


def feature_major_pool(cache):
    n_pool = cache.shape[0]
    return jnp.transpose(cache, (0, 2, 3, 4, 1)).reshape(n_pool, 2 * KV_WIDTH, PAGE_SIZE)


def _page_copy(pool_ref, dst, sem_ref, pt_ref, step, slot, k, n_chunks, pages):
    b_ = step // n_chunks
    c = step % n_chunks
    pid = pt_ref[b_, c * pages + k]
    return pltpu.make_async_copy(pool_ref.at[pid], dst(slot, k), sem_ref.at[slot])


def _page_pipeline(pool_ref, dst, sem_ref, pt_ref, n_chunks, pages):
    step = pl.program_id(0) * n_chunks + pl.program_id(1)
    total = pl.num_programs(0) * n_chunks
    slot = step % 2

    def start(st, sl):
        for k in range(pages):
            _page_copy(pool_ref, dst, sem_ref, pt_ref, st, sl, k, n_chunks, pages).start()

    @pl.when(step == 0)
    def _():
        start(step, slot)

    @pl.when(step + 1 < total)
    def _():
        start(step + 1, 1 - slot)

    for k in range(pages):
        _page_copy(pool_ref, dst, sem_ref, pt_ref, step, slot, k, n_chunks, pages).wait()
    return slot


def _cmp_sample_kernel(pt_ref, pool_ref, w1_ref, w2_ref, w1r_ref, pe_ref, out_ref,
                       buf_ref, sem_ref, x_ref, carry_ref, *, n_chunks, pages):
    slot = _page_pipeline(pool_ref, lambda sl, k: buf_ref.at[sl, k], sem_ref, pt_ref, n_chunks, pages)

    @pl.when(pl.program_id(1) == 0)
    def _():
        carry_ref[...] = jnp.zeros_like(carry_ref)

    segs = PAGE_SIZE // CMP_STRIDE
    nrows = pages * segs
    half_rows = PAGE_SIZE // 2
    dst = lax.broadcasted_iota(jnp.int32, (PAGE_SIZE, PAGE_SIZE), 0)
    src = lax.broadcasted_iota(jnp.int32, (PAGE_SIZE, PAGE_SIZE), 1)
    wanted = (dst % segs) * CMP_STRIDE + 2 * ((dst % half_rows) // segs) + dst // half_rows
    pick = jnp.where(src == wanted, 1.0, 0.0).astype(BF16)

    def relayout(p, carry):
        for quarter in range(QUARTERS):
            t = buf_ref[slot, p, quarter * LANE:(quarter + 1) * LANE, :]
            y = _dot_nt(pick, t)
            x_ref[quarter, p] = jnp.concatenate([y[:half_rows], y[half_rows:]], axis=1)
        return carry

    lax.fori_loop(0, pages, relayout, 0, unroll=16)

    def get_x(quarter, j):
        return x_ref[quarter, :, j * segs:(j + 1) * segs, :].reshape(nrows, 2 * LANE)

    _compress_chunk(get_x, nrows, w1_ref, w2_ref, w1r_ref, pe_ref, carry_ref, out_ref)


CMP_PAGES = 32
SLC_PAGES = 64


def compress_paged(pool, page_table, cw):
    dbs, n_pages = page_table.shape
    pages = math.gcd(n_pages, CMP_PAGES)
    n_chunks = n_pages // pages
    rows = pages * PAGE_SIZE // CMP_STRIDE
    return pl.pallas_call(
        functools.partial(_cmp_sample_kernel, n_chunks=n_chunks, pages=pages),
        out_shape=jax.ShapeDtypeStruct((dbs, 2, n_chunks * rows, KV_WIDTH), F32),
        grid_spec=pltpu.PrefetchScalarGridSpec(
            num_scalar_prefetch=1,
            grid=(dbs, n_chunks),
            in_specs=[pl.BlockSpec(memory_space=pl.ANY)] + _cmp_weight_specs(),
            out_specs=pl.BlockSpec((1, 2, rows, KV_WIDTH), lambda b_, c, pt: (b_, 0, c, 0)),
            scratch_shapes=[pltpu.VMEM((2, pages, 2 * KV_WIDTH, PAGE_SIZE), F32),
                            pltpu.SemaphoreType.DMA((2,)),
                            pltpu.VMEM((QUARTERS, pages, PAGE_SIZE // 2, 2 * LANE), F32),
                            pltpu.VMEM((2, 2, 1, LANE), F32)]),
        compiler_params=_cparams(("arbitrary", "arbitrary")),
        name="cmp_paged",
    )(page_table, pool, *cw)


def _masked_softmax_rows(s, mask):
    s = jnp.where(mask, s, NEG_INF)
    m = jnp.max(s, axis=-1, keepdims=True)
    m = jnp.where(m == NEG_INF, 0.0, m)
    e = jnp.exp(s - m)
    return e / jnp.maximum(jnp.sum(e, axis=-1, keepdims=True), TINY)


def _topk_mask(score, k, axis):
    n = score.shape[axis]
    idx = lax.broadcasted_iota(jnp.int32, score.shape, axis)
    sel = jnp.zeros(score.shape, F32)
    for _ in range(k):
        mx = jnp.max(score, axis=axis, keepdims=True)
        first = jnp.min(jnp.where(score == mx, idx, n), axis=axis, keepdims=True)
        pick = idx == first
        sel = jnp.where(pick, 1.0, sel)
        score = jnp.where(pick, NEG_INF, score)
    return sel


def _flash_tile(carry, s, v):
    m_old, l_old, acc = carry
    m_new = jnp.maximum(m_old, jnp.max(s, axis=-1, keepdims=True))
    m_safe = jnp.where(m_new == NEG_INF, 0.0, m_new)
    p = jnp.exp(s - m_safe)
    alpha = jnp.exp(m_old - m_safe)
    pv = v(p.astype(BF16)) if callable(v) else _dot(p, v)
    return m_new, alpha * l_old + jnp.sum(p, axis=-1, keepdims=True), alpha * acc + pv


MASKED = -(2.0 ** 100)
LOG2E = math.log2(math.e)


ONES_ROWS = 16
V_ROWS = A_HEAD_DIM + ONES_ROWS


def _flash_cols(carry, s, v_t):
    m_old, acc = carry
    m_new = jnp.maximum(m_old, jnp.max(s, axis=0, keepdims=True))
    return m_new, jnp.exp2(m_old - m_new) * acc + _dot(v_t, jnp.exp2(s - m_new))


def _flash_cols_init(cols):
    return jnp.full((1, cols), MASKED, F32), jnp.zeros((V_ROWS, cols), F32)


def _flash_cols_out(carry):
    acc = carry[1]
    return acc[:A_HEAD_DIM] / jnp.maximum(acc[A_HEAD_DIM:A_HEAD_DIM + 1], TINY)


def _flash_init(rows, dv):
    return jnp.full((rows, 1), NEG_INF, F32), jnp.zeros((rows, 1), F32), jnp.zeros((rows, dv), F32)


def _flash_out(carry):
    _, l, acc = carry
    return acc / jnp.maximum(l, TINY)


KEY_TILE = 512


def _nsa_prompt_kernel(q_ref, small_ref, cmp_ref, ks_ref, vs_ref, kw_ref, vw_ref, cov_ref, exp_ref, o_ref,
                       *, tq, n_slc):
    i = pl.program_id(1)
    q0 = i * tq
    tk = math.gcd(ks_ref.shape[1], KEY_TILE)
    cols = A_GROUP * tq
    q_t = (q_ref[0] * (A_HEAD_DIM ** -0.5 * LOG2E)).T.astype(BF16)
    gate_t = jax.nn.sigmoid(small_ref[0]).T
    ncmp = cmp_ref.shape[2]
    qpos = q0 + lax.broadcasted_iota(jnp.int32, (1, tq), 1)
    jcol = lax.broadcasted_iota(jnp.int32, (ncmp, 1), 0)
    cmp_mask = (jcol >= 1) & (jcol * CMP_STRIDE + (CMP_BLOCK - CMP_STRIDE - 1) <= qpos)
    blk = lax.broadcasted_iota(jnp.int32, (n_slc, 1), 0)
    cur = qpos // SLC_BLOCK
    forced = (blk == 0) | (blk == cur) | (blk == cur - 1)
    future = blk * SLC_BLOCK > qpos
    krow = lax.broadcasted_iota(jnp.int32, (tk, 1), 0)
    lanes = lambda a, r: a[:, r * tq:(r + 1) * tq]
    wk = min(WINDOW + tq, kw_ref.shape[1])
    w_off = pl.multiple_of(jnp.maximum(q0 - WINDOW, 0), tq)
    wpos = w_off + lax.broadcasted_iota(jnp.int32, (wk, 1), 0)
    band = (wpos <= qpos) & (wpos > qpos - WINDOW)
    pieces = []

    for g in range(A_KV_HEADS):
        gs = slice(g * A_HEAD_DIM, (g + 1) * A_HEAD_DIM)
        gv = slice(g * V_ROWS, (g + 1) * V_ROWS)
        head = lambda r: slice((g * A_GROUP + r) * A_HEAD_DIM, (g * A_GROUP + r + 1) * A_HEAD_DIM)
        qg = jnp.concatenate([q_t[head(r)] for r in range(A_GROUP)], axis=1)

        s = jnp.dot(kw_ref[0, pl.ds(w_off, wk), gs], qg, preferred_element_type=F32)
        s = jnp.concatenate([jnp.where(band, lanes(s, r), MASKED) for r in range(A_GROUP)], axis=1)
        o_w = _flash_cols_out(_flash_cols(_flash_cols_init(cols), s, vw_ref[0, gv, pl.ds(w_off, wk)]))

        s_c = _dot(cmp_ref[0, 0, :, gs], qg)
        p_r = []
        for r in range(A_GROUP):
            s_r = jnp.where(cmp_mask, lanes(s_c, r), NEG_INF)
            m = jnp.max(s_r, axis=0, keepdims=True)
            e = jnp.exp2(s_r - jnp.where(m == NEG_INF, 0.0, m))
            p_r.append(e * (1.0 / jnp.maximum(jnp.sum(e, axis=0, keepdims=True), TINY)))
        o_c = _dot_tn(cmp_ref[0, 1, :, gs], jnp.concatenate(p_r, axis=1))
        imp_t = _dot(cov_ref[...], p_r[0] + p_r[1] + p_r[2] + p_r[3])
        score = jnp.where(future, -BIG, imp_t + jnp.where(forced, BIG, 0.0))
        sel_t = _topk_mask(score, min(SLC_TOPK, n_slc), 0)
        sel_bias = jnp.concatenate([jnp.where(sel_t > 0.5, 0.0, MASKED),
                                    jnp.zeros((exp_ref.shape[1] - n_slc, tq), F32)], axis=0)
        sel_bias = jnp.concatenate([sel_bias] * A_GROUP, axis=1).astype(BF16)

        def slc_scores(off):
            return (jnp.dot(ks_ref[0, pl.ds(off, tk), gs], qg, preferred_element_type=F32) +
                    jnp.dot(exp_ref[pl.ds(off, tk), :], sel_bias, preferred_element_type=F32))

        def slc_step(j, carry):
            off = pl.multiple_of(j * tk, tk)
            return _flash_cols(carry, slc_scores(off), vs_ref[0, gv, pl.ds(off, tk)])

        j_last = (q0 + tq - 1) // tk
        carry = lax.fori_loop(0, j_last, slc_step, _flash_cols_init(cols))
        off = pl.multiple_of(j_last * tk, tk)
        causal = off + krow <= qpos
        s = slc_scores(off)
        s = jnp.concatenate([jnp.where(causal, lanes(s, r), MASKED) for r in range(A_GROUP)], axis=1)
        o_s = _flash_cols_out(_flash_cols(carry, s, vs_ref[0, gv, pl.ds(off, tk)]))

        for r in range(A_GROUP):
            row = SMALL_AG + 3 * (g * A_GROUP + r)
            pieces.append(gate_t[row:row + 1] * lanes(o_c, r) + gate_t[row + 1:row + 2] * lanes(o_s, r) +
                          gate_t[row + 2:row + 3] * lanes(o_w, r))
    o_ref[0] = jnp.concatenate(pieces, axis=0).T


def _coverage(n_cmp_rows, n_slc):
    cs = (np.arange(n_cmp_rows) - 1) * CMP_STRIDE
    ss = np.arange(n_slc) * SLC_BLOCK
    lo = np.maximum(cs[:, None], ss[None, :])
    hi = np.minimum(cs[:, None] + CMP_BLOCK, ss[None, :] + SLC_BLOCK)
    cov = np.clip(hi - lo, 0, None) / CMP_BLOCK
    cov[0] = 0.0
    return cov.astype(np.float32)


def _block_expand(n_rows, n_keys):
    return (np.arange(n_rows)[:, None] == (np.arange(n_keys) // SLC_BLOCK)[None, :]).astype(np.float32)


def nsa_prompt(q_rot, z, cmp, kv_slc, kv_win, tq):
    bsz, t, _ = q_rot.shape
    n_slc = t // SLC_BLOCK
    ncmp = cmp.shape[2]
    kk = lambda kv: kv[..., :KV_WIDTH].astype(BF16)
    def vt(kv):
        v = jnp.swapaxes(kv[..., KV_WIDTH:], 1, 2).astype(BF16).reshape(bsz, A_KV_HEADS, A_HEAD_DIM, t)
        ones = jnp.ones((bsz, A_KV_HEADS, ONES_ROWS, t), BF16)
        return jnp.concatenate([v, ones], axis=2).reshape(bsz, A_KV_HEADS * V_ROWS, t)

    cov_t = jnp.asarray(_coverage(ncmp, n_slc).T, BF16)
    n_exp = -(-n_slc // 128) * 128
    expand = jnp.asarray(_block_expand(n_exp, t).T, BF16)
    per_b = lambda shape: pl.BlockSpec((1,) + shape, lambda b_, i: (b_,) + (0,) * len(shape))
    return pl.pallas_call(
        functools.partial(_nsa_prompt_kernel, tq=tq, n_slc=n_slc),
        out_shape=jax.ShapeDtypeStruct((bsz, t, D_MODEL), F32),
        grid=(bsz, t // tq),
        in_specs=[pl.BlockSpec((1, tq, D_MODEL), lambda b_, i: (b_, i, 0)),
                  pl.BlockSpec((1, tq, 128), lambda b_, i: (b_, i, Z_SMALL // 128)),
                  per_b((2, ncmp, KV_WIDTH)),
                  per_b((t, KV_WIDTH)), per_b((A_KV_HEADS * V_ROWS, t)),
                  per_b((t, KV_WIDTH)), per_b((A_KV_HEADS * V_ROWS, t)),
                  pl.BlockSpec((n_slc, ncmp), lambda b_, i: (0, 0)),
                  pl.BlockSpec((t, n_exp), lambda b_, i: (0, 0))],
        out_specs=pl.BlockSpec((1, tq, D_MODEL), lambda b_, i: (b_, i, 0)),
        compiler_params=_cparams(("arbitrary", "arbitrary")),
        name="nsa_prompt",
    )(q_rot, z, cmp, kk(kv_slc), vt(kv_slc), kk(kv_win), vt(kv_win), cov_t, expand)


SAMPLE_ROWS = 8
NEW_KEYS = 128


def _nsa_sample_kernel(pt_ref, qbd_ref, gl_ref, cmp_ref, pool_ref, knew_ref, wcache_ref, wnew_ref, cov_ref, exp_ref,
                       o_ref, buf_ref, sem_ref, sel_ref, m_ref, l_ref, acc_ref, oc_ref,
                       *, n_chunks, pages, past, t_real, n_slc):
    c = pl.program_id(1)
    slot = _page_pipeline(pool_ref, lambda sl, k: buf_ref.at[sl, k], sem_ref, pt_ref, n_chunks, pages)
    qbd = qbd_ref[0]
    rows = qbd.shape[0]
    bpc = pages * PAGE_SIZE // SLC_BLOCK
    rq = lax.broadcasted_iota(jnp.int32, (rows, 1), 0) % t_real
    qpos = past + rq

    @pl.when(c == 0)
    def _():
        ncmp = cmp_ref.shape[2]
        nbp = cov_ref.shape[0]
        jrow = lax.broadcasted_iota(jnp.int32, (1, ncmp), 1)
        cmp_mask = (jrow >= 1) & (jrow * CMP_STRIDE + (CMP_BLOCK - CMP_STRIDE - 1) <= qpos)
        p = _masked_softmax_rows(_dot_nt(qbd, cmp_ref[0, 0]), cmp_mask)
        oc_ref[...] = _dot(p, cmp_ref[0, 1])
        ri = lax.broadcasted_iota(jnp.int32, (rows, rows), 0)
        ci = lax.broadcasted_iota(jnp.int32, (rows, rows), 1)
        group_rows = A_GROUP * t_real
        same = (ri // group_rows == ci // group_rows) & (ri % t_real == ci % t_real)
        p_group = _dot(jnp.where(same, 1.0, 0.0), p)
        imp_t = _dot_nt(cov_ref[...], p_group)
        blk = lax.broadcasted_iota(jnp.int32, (nbp, 1), 0)
        qpos_r = past + lax.broadcasted_iota(jnp.int32, (1, rows), 1) % t_real
        cur = qpos_r // SLC_BLOCK
        forced = (blk == 0) | (blk == cur) | (blk == cur - 1)
        future = blk * SLC_BLOCK > qpos_r
        score = jnp.where(future, -BIG, imp_t + jnp.where(forced, BIG, 0.0))
        score = jnp.where(blk < n_slc, score, NEG_INF)
        sel_t = _topk_mask(score, min(SLC_TOPK, n_slc), 0)
        sel = _dot_nt(jnp.where(ri == ci, 1.0, 0.0), sel_t)
        for cc in range(n_chunks + 1):
            chunk_sel = sel[:, cc * bpc:(cc + 1) * bpc]
            if bpc < 128:
                chunk_sel = jnp.concatenate([chunk_sel, jnp.zeros((rows, 128 - bpc), F32)], axis=1)
            sel_ref[cc] = chunk_sel.astype(BF16)
        m_ref[...] = jnp.full(m_ref.shape, NEG_INF, F32)
        l_ref[...] = jnp.zeros_like(l_ref)
        acc_ref[...] = jnp.zeros_like(acc_ref)

    scores = jnp.concatenate([_dot(qbd, buf_ref[slot, k, 0:KV_WIDTH, :]) for k in range(pages)], axis=1)
    picked = jnp.dot(sel_ref[c], exp_ref[...], preferred_element_type=F32) > 0.5
    s = jnp.where(picked, scores, NEG_INF)

    def values(p):
        return sum(_dot_nt(p[:, k * PAGE_SIZE:(k + 1) * PAGE_SIZE], buf_ref[slot, k, KV_WIDTH:2 * KV_WIDTH, :])
                   for k in range(pages))

    carry = _flash_tile((m_ref[...], l_ref[...], acc_ref[...]), s, values)
    m_ref[...], l_ref[...], acc_ref[...] = carry

    @pl.when(c == n_chunks - 1)
    def _():
        zpad = jnp.zeros((NEW_KEYS - SAMPLE_ROWS, 2 * KV_WIDTH), F32)
        kcol = lax.broadcasted_iota(jnp.int32, (1, NEW_KEYS), 1)
        new_mask = (kcol <= rq) & (kcol < t_real)
        knew = jnp.concatenate([knew_ref[0], zpad], axis=0)
        last_picked = sel_ref[n_chunks][:, 0:1].astype(F32) > 0.5
        s_new = jnp.where(new_mask & last_picked, _dot_nt(qbd, knew[:, :KV_WIDTH]), NEG_INF)
        o_s = _flash_out(_flash_tile((m_ref[...], l_ref[...], acc_ref[...]), s_new, knew[:, KV_WIDTH:]))

        wc = wcache_ref[0]
        wb = wc.shape[0]
        wcol = lax.broadcasted_iota(jnp.int32, (1, wb), 1)
        s_w = jnp.where(wcol > rq + (wb - WINDOW), _dot_nt(qbd, wc[:, :KV_WIDTH]), NEG_INF)
        cw = _flash_tile(_flash_init(rows, KV_WIDTH), s_w, wc[:, KV_WIDTH:])
        wnew = jnp.concatenate([wnew_ref[0], zpad], axis=0)
        s_wn = jnp.where(new_mask, _dot_nt(qbd, wnew[:, :KV_WIDTH]), NEG_INF)
        o_w = _flash_out(_flash_tile(cw, s_wn, wnew[:, KV_WIDTH:]))

        gate = jax.nn.sigmoid(gl_ref[0])
        o = gate[:, 0:1] * oc_ref[...] + gate[:, 1:2] * o_s + gate[:, 2:3] * o_w
        lane_g = lax.broadcasted_iota(jnp.int32, (1, KV_WIDTH), 1) // A_HEAD_DIM
        row_g = lax.broadcasted_iota(jnp.int32, (rows, 1), 0) // (A_GROUP * t_real)
        o = jnp.where(lane_g == row_g, o, 0.0)
        o_ref[0] = sum(o[:, g * A_HEAD_DIM:(g + 1) * A_HEAD_DIM] for g in range(A_KV_HEADS))


def nsa_sample(q_rot, z, cmp, pool, page_table, k_new, win_cache, w_new, past, t_real):
    dbs = q_rot.shape[0]
    n_pages = page_table.shape[1]
    pages = math.gcd(n_pages, SLC_PAGES)
    n_chunks = n_pages // pages
    bpc = pages * PAGE_SIZE // SLC_BLOCK
    ncmp = cmp.shape[2]
    n_slc = -(-(past + t_real) // SLC_BLOCK)
    assert n_slc == n_chunks * bpc + 1 and bpc <= 128
    nbp = -(-((n_chunks + 1) * bpc) // 128) * 128
    rows = A_HEADS * t_real
    q5 = q_rot[:, :t_real].reshape(dbs, t_real, A_KV_HEADS, A_GROUP, A_HEAD_DIM) * (A_HEAD_DIM ** -0.5)
    qbd = jnp.einsum("bqgrd,gh->bgrqhd", q5, jnp.eye(A_KV_HEADS, dtype=F32)).reshape(dbs, rows, KV_WIDTH).astype(BF16)
    gl = z[:, :t_real, Z_SMALL + SMALL_AG:Z_SMALL + SMALL_AG + 3 * A_HEADS].reshape(dbs, t_real, A_HEADS, 3)
    gl = jnp.swapaxes(gl, 1, 2).reshape(dbs, rows, 3)
    cov = np.zeros((nbp, ncmp), np.float32)
    cov[:n_slc] = _coverage(ncmp, n_slc).T
    expand = jnp.asarray(_block_expand(128, pages * PAGE_SIZE), BF16)
    per_b = lambda shape: pl.BlockSpec((1,) + shape, lambda b_, c, pt: (b_,) + (0,) * len(shape))
    const = lambda shape: pl.BlockSpec(shape, lambda b_, c, pt: (0,) * len(shape))
    wb = win_cache.shape[1]
    out = pl.pallas_call(
        functools.partial(_nsa_sample_kernel, n_chunks=n_chunks, pages=pages, past=past, t_real=t_real, n_slc=n_slc),
        out_shape=jax.ShapeDtypeStruct((dbs, rows, A_HEAD_DIM), F32),
        grid_spec=pltpu.PrefetchScalarGridSpec(
            num_scalar_prefetch=1,
            grid=(dbs, n_chunks),
            in_specs=[per_b((rows, KV_WIDTH)), per_b((rows, 3)), per_b((2, ncmp, KV_WIDTH)),
                      pl.BlockSpec(memory_space=pl.ANY),
                      per_b((SAMPLE_ROWS, 2 * KV_WIDTH)), per_b((wb, 2 * KV_WIDTH)), per_b((SAMPLE_ROWS, 2 * KV_WIDTH)),
                      const((nbp, ncmp)), const((128, pages * PAGE_SIZE))],
            out_specs=per_b((rows, A_HEAD_DIM)),
            scratch_shapes=[pltpu.VMEM((2, pages, 2 * KV_WIDTH, PAGE_SIZE), F32),
                            pltpu.SemaphoreType.DMA((2,)),
                            pltpu.VMEM((n_chunks + 1, rows, 128), BF16),
                            pltpu.VMEM((rows, 1), F32), pltpu.VMEM((rows, 1), F32),
                            pltpu.VMEM((rows, KV_WIDTH), F32), pltpu.VMEM((rows, KV_WIDTH), F32)]),
        compiler_params=_cparams(("arbitrary", "arbitrary")),
        name="nsa_sample",
    )(page_table, qbd, gl, cmp, pool, k_new, win_cache, w_new, jnp.asarray(cov, BF16), expand)
    out = jnp.swapaxes(out.reshape(dbs, A_HEADS, t_real, A_HEAD_DIM), 1, 2).reshape(dbs, t_real, D_MODEL)
    return jnp.pad(out, ((0, 0), (0, SAMPLE_ROWS - t_real), (0, 0)))


def _kv_rows(a, bsz, t):
    return a.reshape(bsz, t, 2, A_KV_HEADS, A_HEAD_DIM)


def kernel(x_prompt, x_sample, cache_cmp_kv, cache_slc_kv, cache_win_kv, state_C, state_n, state_m, state_conv,
           page_table, c_prompt, c_sample, w_ada, b_ada, g_pre_mix, g_post_mix, g_pre_ffn, g_post_ffn, w_in, b_in,
           m_norm_w, cmp_w1, cmp_pe, cmp_w2, w_branch_m, w_branch_a, w_out, w_up, conv_w, conv_b, w_down):
    depth = w_ada.shape[0]
    bsz, t, d = x_prompt.shape
    dbs, ts, _ = x_sample.shape
    n_pages = page_table.shape[1]
    past = n_pages * PAGE_SIZE
    assert ts <= SAMPLE_ROWS and (past + ts) // CMP_STRIDE == past // CMP_STRIDE and past >= WINDOW
    srows = dbs * SAMPLE_ROWS
    xp = x_prompt.astype(F32)
    xs = jnp.pad(x_sample.astype(F32), ((0, 0), (0, SAMPLE_ROWS - ts), (0, 0))).reshape(1, srows, d)
    c_all = jnp.concatenate([c_prompt, c_sample], axis=0).astype(F32)
    c_all = jnp.pad(c_all, ((0, (-c_all.shape[0]) % 8), (0, 0)))
    tab_p = rope_tables(jnp.arange(t, dtype=jnp.int32))
    tab_s = rope_tables(jnp.tile(past + jnp.arange(SAMPLE_ROWS, dtype=jnp.int32), dbs))
    lchunk = math.gcd(t, 256)
    tm_p = math.gcd(t, 512)
    p_states, s_states = [], []
    for l in range(depth):
        mod = ada_modulation(c_all, w_ada[l], b_ada[l])
        mod_p = [m[:, None, :] for m in jnp.split(mod[:bsz], 6, axis=-1)]
        mod_s = [jnp.repeat(m, SAMPLE_ROWS, axis=0)[None] for m in jnp.split(mod[bsz:bsz + dbs], 6, axis=-1)]
        w_r, b_r = regroup_in_weights(w_in[l], b_in[l])
        cw = compress_weights(cmp_w1[l], cmp_pe[l], cmp_w2[l])
        wm, wa, wo = w_branch_m[l].astype(BF16), w_branch_a[l].astype(BF16), w_out[l].astype(BF16)
        wu, wd = w_up[l].astype(BF16), w_down[l].astype(BF16)

        sh_m, sc_m, gt_m, sh_f, sc_f, gt_f = mod_p
        z = in_projection(xp, g_pre_mix[l], sc_m, sh_m, w_r, b_r, tm=math.gcd(t, 1024))
        q_rot, kv_cmp, kv_slc, kv_win = rope_split(z, tab_p, tm=tm_p)
        hm, p_c, p_n, p_m = mlstm(z, m_norm_w[l], jnp.zeros((bsz, M_HEADS, M_HEAD_DIM, M_HEAD_DIM), F32),
                                  jnp.zeros((bsz, M_HEADS, M_HEAD_DIM), F32), jnp.zeros((bsz, M_HEADS), F32),
                                  lb=lchunk, lp=lchunk, t_real=lchunk)
        ha = nsa_prompt(q_rot, z, compress_prompt(kv_cmp, cw), kv_slc, kv_win, tq=256)
        xp = mix_out(xp, hm, ha, z, gt_m, g_post_mix[l], wm, wa, wo, tm=tm_p)
        xp, p_conv = conv_ffn_rows(xp, g_pre_ffn[l], sc_f, sh_f, gt_f, g_post_ffn[l], wu, conv_w[l], conv_b[l], wd,
                                   tm=tm_p)
        wkeep = min(WINDOW, t)
        p_states.append((_kv_rows(kv_cmp, bsz, t), _kv_rows(kv_slc, bsz, t), _kv_rows(kv_win[:, t - wkeep:], bsz, wkeep),
                         p_c, p_n, p_m, p_conv))

        sh_m, sc_m, gt_m, sh_f, sc_f, gt_f = mod_s
        z = in_projection(xs, g_pre_mix[l], sc_m, sh_m, w_r, b_r, tm=srows)
        q_rot, kv_cmp, kv_slc, kv_win = rope_split(z, tab_s, tm=srows)
        z3 = z.reshape(dbs, SAMPLE_ROWS, Z_WIDTH)
        hm, s_c, s_n, s_m = mlstm(z3, m_norm_w[l], state_C[l].astype(F32), state_n[l].astype(F32),
                                  state_m[l].astype(F32), lb=SAMPLE_ROWS, lp=128, t_real=ts)
        cmp_s = compress_paged(feature_major_pool(cache_cmp_kv[l].astype(F32)), page_table, cw)
        new3 = lambda a: a.reshape(dbs, SAMPLE_ROWS, 2 * KV_WIDTH)
        win_cache = cache_win_kv[l].astype(F32).reshape(dbs, -1, 2 * KV_WIDTH)
        ha = nsa_sample(q_rot.reshape(dbs, SAMPLE_ROWS, d), z3, cmp_s, feature_major_pool(cache_slc_kv[l].astype(F32)),
                        page_table, new3(kv_slc), win_cache, new3(kv_win), past, ts)
        xs = mix_out(xs, hm.reshape(1, srows, d), ha.reshape(1, srows, d), z, gt_m, g_post_mix[l], wm, wa, wo, tm=srows)
        st = state_conv[l].astype(F32)
        s2 = jnp.pad(st, ((0, 0), (0, SAMPLE_ROWS - (CONV_W - 1)), (0, 0))).reshape(1, srows, 2 * D_FF)
        s1 = jnp.pad(st[:, 1:], ((0, 0), (0, SAMPLE_ROWS - 1), (0, 0))).reshape(1, srows, 2 * D_FF)
        xs, u = conv_ffn(xs, g_pre_ffn[l], sc_f, sh_f, gt_f, g_post_ffn[l], wu, conv_w[l], conv_b[l], wd, tm=srows,
                         state_rows=(s1, s2))
        wb = win_cache.shape[1]
        s_win = jnp.concatenate([win_cache, new3(kv_win)[:, :ts]], axis=1)[:, ts:]
        s_conv = jnp.concatenate([st, u.reshape(dbs, SAMPLE_ROWS, 2 * D_FF)[:, :ts]], axis=1)[:, ts:]
        s_states.append((_kv_rows(new3(kv_cmp)[:, :ts], dbs, ts), _kv_rows(new3(kv_slc)[:, :ts], dbs, ts),
                         _kv_rows(s_win, dbs, wb), s_c, s_n, s_m, s_conv))

    stack = lambda states: [jnp.stack([s[i] for s in states]) for i in range(7)]
    y_sample = xs.reshape(dbs, SAMPLE_ROWS, d)[:, :ts]
    return (xp, y_sample, *stack(p_states), *stack(s_states))
```

```python
import functools
import math

import numpy as np
import jax
import jax.numpy as jnp
from jax import lax
from jax.experimental import pallas as pl
from jax.experimental.pallas import tpu as pltpu

F32 = jnp.float32
BF16 = jnp.bfloat16

D_MODEL = 1024
M_HEADS = 4
M_HEAD_DIM = 256
A_HEADS = 16
A_HEAD_DIM = 64
A_KV_HEADS = 4
A_GROUP = 4
KV_WIDTH = A_KV_HEADS * A_HEAD_DIM
CMP_STRIDE = 16
CMP_BLOCK = 32
SLC_BLOCK = 64
SLC_TOPK = 16
WINDOW = 512
ROPE_THETA = 500000.0
ROPE_DIM = 16
BIG = 1e6
D_FF = 2816
CONV_W = 3
NORM_EPS = 1e-6
PAGE_SIZE = 128
NEG_INF = float("-inf")
TINY = float(np.finfo(np.float32).tiny)

Z_MQ, Z_MK, Z_MV, Z_MO, Z_AQ, Z_GA, Z_GB, Z_AKV, Z_SMALL = 0, 1024, 2048, 3072, 4096, 5120, 6144, 7168, 8704
Z_WIDTH = 9216
SMALL_MI, SMALL_MF, SMALL_AG = 0, 4, 8

VMEM_LIMIT = 48 * 1024 * 1024


def _cparams(sem):
    return pltpu.CompilerParams(dimension_semantics=sem, vmem_limit_bytes=VMEM_LIMIT)


def _dot(a, b):
    return jnp.dot(a.astype(BF16), b.astype(BF16), preferred_element_type=F32)


def _dot_nt(a, b):
    return lax.dot_general(a.astype(BF16), b.astype(BF16), (((1,), (1,)), ((), ())), preferred_element_type=F32)


def _dot_tn(a, b):
    return lax.dot_general(a.astype(BF16), b.astype(BF16), (((0,), (0,)), ((), ())), preferred_element_type=F32)


def _split3(x):
    x1 = x.astype(BF16)
    r1 = x - x1.astype(F32)
    x2 = r1.astype(BF16)
    x3 = (r1 - x2.astype(F32)).astype(BF16)
    return x1, x2, x3


def _rms(x, g):
    return x * lax.rsqrt(jnp.mean(x * x, axis=-1, keepdims=True) + NORM_EPS) * g


def _ada_kernel(c_ref, w_ref, b_ref, o_ref):
    c = c_ref[...]
    o_ref[...] = _dot(c * jax.nn.sigmoid(c), w_ref[...]) + b_ref[...]


def ada_modulation(c, w_ada, b_ada):
    rows, d = c.shape
    n = w_ada.shape[1]
    tn = 512
    return pl.pallas_call(
        _ada_kernel,
        out_shape=jax.ShapeDtypeStruct((rows, n), F32),
        grid=(n // tn,),
        in_specs=[pl.BlockSpec((rows, d), lambda j: (0, 0)),
                  pl.BlockSpec((d, tn), lambda j: (0, j)),
                  pl.BlockSpec((1, tn), lambda j: (0, j))],
        out_specs=pl.BlockSpec((rows, tn), lambda j: (0, j)),
        compiler_params=_cparams(("arbitrary",)),
        name="ada",
    )(c, w_ada, b_ada.reshape(1, n))


def _inproj_kernel(x_ref, g_ref, sc_ref, sh_ref, w_ref, b_ref, o_ref, h_ref):
    @pl.when(pl.program_id(2) == 0)
    def _():
        h = _rms(x_ref[0], g_ref[...]) * (1.0 + sc_ref[0]) + sh_ref[0]
        h_ref[...] = h.astype(BF16)

    o_ref[0] = jnp.dot(h_ref[...], w_ref[...], preferred_element_type=F32) + b_ref[...]


def in_projection(x, g, sc, sh, w_bf16, b, tm):
    bsz, t, d = x.shape
    r = sc.shape[1]
    rb = 1 if r == 1 else tm
    tn = 1024
    mod_spec = pl.BlockSpec((1, rb, d), (lambda b_, i, j: (b_, 0, 0)) if r == 1 else (lambda b_, i, j: (b_, i, 0)))
    return pl.pallas_call(
        _inproj_kernel,
        out_shape=jax.ShapeDtypeStruct((bsz, t, Z_WIDTH), F32),
        grid=(bsz, t // tm, Z_WIDTH // tn),
        in_specs=[pl.BlockSpec((1, tm, d), lambda b_, i, j: (b_, i, 0)),
                  pl.BlockSpec((1, d), lambda b_, i, j: (0, 0)),
                  mod_spec, mod_spec,
                  pl.BlockSpec((d, tn), lambda b_, i, j: (0, j)),
                  pl.BlockSpec((1, tn), lambda b_, i, j: (0, j))],
        out_specs=pl.BlockSpec((1, tm, tn), lambda b_, i, j: (b_, i, j)),
        scratch_shapes=[pltpu.VMEM((tm, d), BF16)],
        compiler_params=_cparams(("arbitrary", "arbitrary", "arbitrary")),
        name="inproj",
    )(x, g.reshape(1, d), sc, sh, w_bf16, b.reshape(1, Z_WIDTH))


def regroup_in_weights(w_in, b_in):
    mw = M_HEADS * M_HEAD_DIM
    o_mi = 4 * mw
    o_aq = o_mi + 2 * M_HEADS
    o_akv = o_aq + A_HEADS * A_HEAD_DIM
    o_ag = o_akv + 6 * KV_WIDTH
    o_ga = o_ag + 3 * A_HEADS
    o_gb = o_ga + D_MODEL

    def regroup(a):
        lead = a.shape[:-1]
        parts = [a[..., :o_mi], a[..., o_aq:o_akv], a[..., o_ga:o_gb], a[..., o_gb:o_gb + D_MODEL],
                 a[..., o_akv:o_ag], a[..., o_mi:o_aq], a[..., o_ag:o_ga],
                 jnp.zeros(lead + (128 - 2 * M_HEADS - 3 * A_HEADS,), a.dtype),
                 jnp.zeros(lead + (Z_WIDTH - Z_SMALL - 128,), a.dtype)]
        return jnp.concatenate(parts, axis=-1)

    return regroup(w_in).astype(BF16), regroup(b_in)


def rope_tables(pos):
    half = ROPE_DIM // 2
    inv_freq = ROPE_THETA ** (-jnp.arange(half, dtype=F32) / half)
    ang = pos.astype(F32)[:, None] * inv_freq
    cos, sin = jnp.cos(ang), jnp.sin(ang)
    rows = pos.shape[0]
    zeros = jnp.zeros((rows, half), F32)
    rest1 = jnp.ones((rows, A_HEAD_DIM - ROPE_DIM), F32)
    rest0 = jnp.zeros((rows, A_HEAD_DIM - ROPE_DIM), F32)
    c = jnp.concatenate([cos, cos, rest1], axis=1)
    sa = jnp.concatenate([zeros, sin, rest0], axis=1)
    sb = jnp.concatenate([-sin, zeros, rest0], axis=1)
    return tuple(jnp.concatenate([a, a], axis=1) for a in (c, sa, sb))


def _rope_apply(x, c, sa, sb):
    w = x.shape[1]
    n = w // 128
    ct, sat, sbt = (jnp.concatenate([a] * n, axis=1) for a in (c, sa, sb))
    return x * ct + pltpu.roll(x, ROPE_DIM // 2, 1) * sat + pltpu.roll(x, w - ROPE_DIM // 2, 1) * sbt


ATTN_V_ROWS = A_KV_HEADS * (A_HEAD_DIM + 16)


def _rope_kernel(q_ref, c_ref, s_ref, w_ref, cos_ref, sa_ref, sb_ref, qo_ref, co_ref, so_ref, wo_ref, *attn_refs):
    c, sa, sb = cos_ref[...], sa_ref[...], sb_ref[...]
    qo_ref[0] = _rope_apply(q_ref[0], c, sa, sb)
    rotated = []
    for src, dst in ((c_ref, co_ref), (s_ref, so_ref), (w_ref, wo_ref)):
        kv = src[0]
        rotated.append(jnp.concatenate([_rope_apply(kv[:, :KV_WIDTH], c, sa, sb), kv[:, KV_WIDTH:]], axis=1))
        dst[0] = rotated[-1]
    if attn_refs:
        tm = q_ref.shape[1]
        ones = jnp.ones((16, tm), BF16)
        for kv, k_ref, vt_ref in ((rotated[1], attn_refs[0], attn_refs[1]), (rotated[2], attn_refs[2], attn_refs[3])):
            k_ref[0] = kv[:, :KV_WIDTH].astype(BF16)
            v_t = kv[:, KV_WIDTH:].T.astype(BF16)
            vt_ref[0] = jnp.concatenate([p for g in range(A_KV_HEADS)
                                         for p in (v_t[g * A_HEAD_DIM:(g + 1) * A_HEAD_DIM], ones)], axis=0)


def rope_split(z, tables, tm, attn_layouts=False):
    bsz, t, _ = z.shape
    nt = t // tm
    kvw = 2 * KV_WIDTH
    tab_spec = pl.BlockSpec((tm, 128), lambda b_, i: (i, 0))
    out_shape = (jax.ShapeDtypeStruct((bsz, t, D_MODEL), F32),) + (jax.ShapeDtypeStruct((bsz, t, kvw), F32),) * 3
    out_specs = (pl.BlockSpec((1, tm, D_MODEL), lambda b_, i: (b_, i, 0)),) + \
                (pl.BlockSpec((1, tm, kvw), lambda b_, i: (b_, i, 0)),) * 3
    if attn_layouts:
        out_shape += (jax.ShapeDtypeStruct((bsz, t, KV_WIDTH), BF16), jax.ShapeDtypeStruct((bsz, ATTN_V_ROWS, t), BF16)) * 2
        out_specs += (pl.BlockSpec((1, tm, KV_WIDTH), lambda b_, i: (b_, i, 0)),
                      pl.BlockSpec((1, ATTN_V_ROWS, tm), lambda b_, i: (b_, 0, i))) * 2
    return pl.pallas_call(
        _rope_kernel,
        out_shape=out_shape,
        grid=(bsz, nt),
        in_specs=[pl.BlockSpec((1, tm, D_MODEL), lambda b_, i: (b_, i, Z_AQ // D_MODEL)),
                  pl.BlockSpec((1, tm, kvw), lambda b_, i: (b_, i, Z_AKV // kvw)),
                  pl.BlockSpec((1, tm, kvw), lambda b_, i: (b_, i, Z_AKV // kvw + 1)),
                  pl.BlockSpec((1, tm, kvw), lambda b_, i: (b_, i, Z_AKV // kvw + 2)),
                  tab_spec, tab_spec, tab_spec],
        out_specs=out_specs,
        compiler_params=_cparams(("arbitrary", "arbitrary")),
        name="rope",
    )(z, z, z, z, *tables)


def _mlstm_kernel(q_ref, k_ref, v_ref, o_ref, s_ref, nw_ref, c0_ref, n0_ref, m0_ref,
                  h_ref, c_ref, n_ref, m_ref, *, lb, lp, t_real):
    @pl.when(pl.program_id(1) == 0)
    def _():
        c_ref[...] = c0_ref[...]
        n_ref[...] = n0_ref[...]
        m_ref[...] = m0_ref[...]

    def pad(a):
        if lb == lp:
            return a
        return jnp.concatenate([a, jnp.zeros((lp - lb, a.shape[1]), a.dtype)], axis=0)

    small = pad(s_ref[0])
    small_t = small.T
    row_c = lax.broadcasted_iota(jnp.int32, (lp, 1), 0)
    row_r = lax.broadcasted_iota(jnp.int32, (1, lp), 1)
    li_col_all = jnp.where(row_c < t_real, small, NEG_INF)
    lf_col_all = jnp.where(row_c < t_real, jax.nn.log_sigmoid(small), 0.0)
    li_row_all = jnp.where(row_r < t_real, small_t[0:8], NEG_INF)
    lf_row_all = jnp.where(row_r < t_real, jax.nn.log_sigmoid(small_t[0:8]), 0.0)
    rr = lax.broadcasted_iota(jnp.int32, (lp, lp), 0)
    cc = lax.broadcasted_iota(jnp.int32, (lp, lp), 1)
    causal = cc <= rr
    tril = jnp.where(causal, 1.0, 0.0).astype(BF16)
    triu = jnp.where(rr <= cc, 1.0, 0.0).astype(BF16)
    b_col_all = sum(jnp.dot(tril, p, preferred_element_type=F32) for p in _split3(lf_col_all))
    b_row_all = sum(jnp.dot(p, triu, preferred_element_type=F32) for p in _split3(lf_row_all))

    q_all, k_all, v_all, o_all = pad(q_ref[0]), pad(k_ref[0]), pad(v_ref[0]), pad(o_ref[0])
    nw = nw_ref[...]
    for h in range(M_HEADS):
        hs = slice(h * M_HEAD_DIM, (h + 1) * M_HEAD_DIM)
        qf = q_all[:, hs]
        kf = k_all[:, hs] * (M_HEAD_DIM ** -0.5)
        vf = v_all[:, hs]
        li_row = li_row_all[SMALL_MI + h:SMALL_MI + h + 1, :]
        b_row = b_row_all[SMALL_MF + h:SMALL_MF + h + 1, :]
        li_col = li_col_all[:, SMALL_MI + h:SMALL_MI + h + 1]
        b_col = b_col_all[:, SMALL_MF + h:SMALL_MF + h + 1]
        m_prev = m_ref[0, h]
        c_prev = c_ref[0, h]
        n_prev = n_ref[0, h]

        dlog = jnp.where(causal, b_col - b_row + li_row, NEG_INF)
        inter = m_prev + b_col
        mt = jnp.maximum(inter, jnp.max(dlog, axis=1, keepdims=True))
        a = jnp.exp(inter - mt)
        s = _dot_nt(qf, kf) * jnp.exp(dlog - mt)
        num = a * _dot_nt(qf, c_prev) + _dot(s, vf)
        den = a * jnp.sum(qf * n_prev, axis=1, keepdims=True) + jnp.sum(s, axis=1, keepdims=True)
        hh = num / jnp.maximum(jnp.abs(den), jnp.exp(-mt))
        mu = jnp.mean(hh, axis=1, keepdims=True)
        var = jnp.mean(jnp.square(hh - mu), axis=1, keepdims=True)
        out = (hh - mu) * lax.rsqrt(var + NORM_EPS) * nw[:, hs] * jax.nn.sigmoid(o_all[:, hs])
        h_ref[0, :, hs] = out[:lb]

        bl = b_row[:, lp - 1:lp]
        wlog = bl - b_col + li_col
        m_new = jnp.maximum(m_prev + bl, jnp.max(wlog, axis=0, keepdims=True))
        w = jnp.exp(wlog - m_new)
        decay = jnp.exp(m_prev + bl - m_new)
        c_ref[0, h] = decay * c_prev + _dot_tn(vf * w, kf)
        n_ref[0, h] = decay * n_prev + jnp.sum(w * kf, axis=0, keepdims=True)
        m_ref[0, h] = m_new


def mlstm(z, norm_w, c0, n0, m0, lb, lp, t_real):
    bsz, t, _ = z.shape
    nc = t // lb
    mw = M_HEADS * M_HEAD_DIM
    zspec = lambda col: pl.BlockSpec((1, lb, mw), lambda b_, c: (b_, c, col // mw))
    cst = lambda shape: pl.BlockSpec((1,) + shape, lambda b_, c: (b_,) + (0,) * len(shape))
    h, c, n, m = pl.pallas_call(
        functools.partial(_mlstm_kernel, lb=lb, lp=lp, t_real=t_real),
        out_shape=(jax.ShapeDtypeStruct((bsz, t, mw), F32),
                   jax.ShapeDtypeStruct((bsz, M_HEADS, M_HEAD_DIM, M_HEAD_DIM), F32),
                   jax.ShapeDtypeStruct((bsz, M_HEADS, 1, M_HEAD_DIM), F32),
                   jax.ShapeDtypeStruct((bsz, M_HEADS, 1, 1), F32)),
        grid=(bsz, nc),
        in_specs=[zspec(Z_MQ), zspec(Z_MK), zspec(Z_MV), zspec(Z_MO),
                  pl.BlockSpec((1, lb, 128), lambda b_, c: (b_, c, Z_SMALL // 128)),
                  pl.BlockSpec((1, mw), lambda b_, c: (0, 0)),
                  cst((M_HEADS, M_HEAD_DIM, M_HEAD_DIM)), cst((M_HEADS, 1, M_HEAD_DIM)), cst((M_HEADS, 1, 1))],
        out_specs=(pl.BlockSpec((1, lb, mw), lambda b_, c: (b_, c, 0)),
                   cst((M_HEADS, M_HEAD_DIM, M_HEAD_DIM)), cst((M_HEADS, 1, M_HEAD_DIM)), cst((M_HEADS, 1, 1))),
        compiler_params=_cparams(("arbitrary", "arbitrary")),
        name="mlstm",
    )(z, z, z, z, z, norm_w.reshape(1, mw), c0, n0.reshape(bsz, M_HEADS, 1, M_HEAD_DIM),
      m0.reshape(bsz, M_HEADS, 1, 1))
    return h, c, n.reshape(bsz, M_HEADS, M_HEAD_DIM), m.reshape(bsz, M_HEADS)


def _mix_kernel(x_ref, hm_ref, ha_ref, ga_ref, gb_ref, gt_ref, g_ref, wm_ref, wa_ref, wo_ref, o_ref):
    mixed = (jax.nn.sigmoid(ga_ref[0]) * _dot(hm_ref[0], wm_ref[...]) +
             jax.nn.sigmoid(gb_ref[0]) * _dot(ha_ref[0], wa_ref[...]))
    o_ref[0] = x_ref[0] + gt_ref[0] * _rms(_dot(mixed, wo_ref[...]), g_ref[...])


def mix_out(x, hm, ha, z, gt, g_post, wm, wa, wo, tm):
    bsz, t, d = x.shape
    r = gt.shape[1]
    rb = 1 if r == 1 else tm
    row = lambda col=0: pl.BlockSpec((1, tm, d), lambda b_, i: (b_, i, col // d))
    mod = pl.BlockSpec((1, rb, d), (lambda b_, i: (b_, 0, 0)) if r == 1 else (lambda b_, i: (b_, i, 0)))
    wsp = pl.BlockSpec((d, d), lambda b_, i: (0, 0))
    return pl.pallas_call(
        _mix_kernel,
        out_shape=jax.ShapeDtypeStruct((bsz, t, d), F32),
        grid=(bsz, t // tm),
        in_specs=[row(), row(), row(), row(Z_GA), row(Z_GB), mod,
                  pl.BlockSpec((1, d), lambda b_, i: (0, 0)), wsp, wsp, wsp],
        out_specs=row(),
        compiler_params=_cparams(("arbitrary", "arbitrary")),
        name="mix",
    )(x, hm, ha, z, z, gt, g_post.reshape(1, d), wm, wa, wo)


FF_CHUNK = 256


def _ffn_kernel(x_ref, g_ref, sc_ref, sh_ref, gt_ref, gp_ref, wa_ref, wg_ref, cwa_ref, cwg_ref, cba_ref, cbg_ref,
                wd_ref, s1a_ref, s1g_ref, s2a_ref, s2g_ref, y_ref, ua_ref, ug_ref, h_ref, acc_ref):
    f = pl.program_id(2)
    tm = x_ref.shape[1]

    @pl.when(f == 0)
    def _():
        h = _rms(x_ref[0], g_ref[...]) * (1.0 + sc_ref[0]) + sh_ref[0]
        h_ref[...] = h.astype(BF16)
        acc_ref[...] = jnp.zeros_like(acc_ref)

    t = lax.broadcasted_iota(jnp.int32, (tm, 1), 0) % SAMPLE_ROWS

    def branch(w_ref, cw_ref, cb_ref, s1_ref, s2_ref):
        u = jnp.dot(h_ref[...], w_ref[...], preferred_element_type=F32)
        u1 = jnp.where(t < 1, s1_ref[0], pltpu.roll(u, 1, 0))
        u2 = jnp.where(t < 2, s2_ref[0], pltpu.roll(u, 2, 0))
        cw = cw_ref[...]
        return u, cb_ref[...] + cw[0:1] * u2 + cw[1:2] * u1 + cw[2:3] * u

    ua_ref[0], conv_a = branch(wa_ref, cwa_ref, cba_ref, s1a_ref, s2a_ref)
    ug_ref[0], conv_g = branch(wg_ref, cwg_ref, cbg_ref, s1g_ref, s2g_ref)
    acc_ref[...] += _dot(jax.nn.gelu(conv_g) * conv_a, wd_ref[...])

    @pl.when(f == pl.num_programs(2) - 1)
    def _():
        y_ref[0] = x_ref[0] + gt_ref[0] * _rms(acc_ref[...], gp_ref[...])


def conv_ffn(x, g_pre, sc, sh, gt, g_post, w_up, conv_w, conv_b, w_down, tm, state_rows):
    bsz, t, d = x.shape
    ck = FF_CHUNK
    nf = D_FF // ck
    xrow = pl.BlockSpec((1, tm, d), lambda b_, i, f: (b_, i, 0))
    vec = pl.BlockSpec((1, d), lambda b_, i, f: (0, 0))
    col_a = lambda rows: pl.BlockSpec((rows, ck), lambda b_, i, f: (0, f))
    col_g = lambda rows: pl.BlockSpec((rows, ck), lambda b_, i, f: (0, nf + f))
    st_a = pl.BlockSpec((1, tm, ck), lambda b_, i, f: (b_, i, f))
    st_g = pl.BlockSpec((1, tm, ck), lambda b_, i, f: (b_, i, nf + f))
    cb = conv_b.reshape(1, 2 * D_FF)
    y, ua, ug = pl.pallas_call(
        _ffn_kernel,
        out_shape=(jax.ShapeDtypeStruct((bsz, t, d), F32),) + (jax.ShapeDtypeStruct((bsz, t, D_FF), F32),) * 2,
        grid=(bsz, t // tm, nf),
        in_specs=[xrow, vec, xrow, xrow, xrow, vec, col_a(d), col_g(d), col_a(CONV_W), col_g(CONV_W), col_a(1), col_g(1),
                  pl.BlockSpec((ck, d), lambda b_, i, f: (f, 0)), st_a, st_g, st_a, st_g],
        out_specs=(xrow, st_a, st_a),
        scratch_shapes=[pltpu.VMEM((tm, d), BF16), pltpu.VMEM((tm, d), F32)],
        compiler_params=_cparams(("arbitrary", "arbitrary", "arbitrary")),
        name="ffn",
    )(x, g_pre.reshape(1, d), sc, sh, gt, g_post.reshape(1, d), w_up, w_up, conv_w, conv_w, cb, cb, w_down,
      state_rows[0], state_rows[0], state_rows[1], state_rows[1])
    return y, jnp.concatenate([ua, ug], axis=-1)


def _ffn_rows_kernel(x_ref, g_ref, sc_ref, sh_ref, gt_ref, gp_ref, wu_ref, cw_ref, cb_ref, wd_ref,
                     y_ref, tail_ref, carry_ref):
    tm = x_ref.shape[1]
    ck = FF_CHUNK

    @pl.when(pl.program_id(1) == 0)
    def _():
        carry_ref[...] = jnp.zeros_like(carry_ref)

    h = (_rms(x_ref[0], g_ref[...]) * (1.0 + sc_ref[0]) + sh_ref[0]).astype(BF16)
    top = lax.broadcasted_iota(jnp.int32, (8, 1), 0)
    acc = jnp.zeros((tm, x_ref.shape[2]), F32)

    def conv(cols):
        u = jnp.dot(h, wu_ref[:, cols], preferred_element_type=F32)
        r1 = pltpu.roll(u, 1, 0)
        r2 = pltpu.roll(u, 2, 0)
        prev = carry_ref[:, cols]
        u1 = jnp.concatenate([jnp.where(top < 1, prev[1:2], r1[0:8]), r1[8:]], axis=0)
        u2 = jnp.concatenate([jnp.where(top < 1, prev[0:1], jnp.where(top < 2, prev[1:2], r2[0:8])), r2[8:]], axis=0)
        carry_ref[0:2, cols] = u[tm - 2:tm]
        tail_ref[0, 0, :, cols] = u[tm - 2:tm]
        cw = cw_ref[:, cols]
        return cb_ref[:, cols] + cw[0:1] * u2 + cw[1:2] * u1 + cw[2:3] * u

    for c in range(D_FF // ck):
        conv_a = conv(slice(c * ck, (c + 1) * ck))
        conv_g = conv(slice(D_FF + c * ck, D_FF + (c + 1) * ck))
        acc = acc + _dot(jax.nn.gelu(conv_g) * conv_a, wd_ref[c * ck:(c + 1) * ck, :])
    y_ref[0] = x_ref[0] + gt_ref[0] * _rms(acc, gp_ref[...])


def conv_ffn_rows(x, g_pre, sc, sh, gt, g_post, w_up, conv_w, conv_b, w_down, tm):
    bsz, t, d = x.shape
    mod = pl.BlockSpec((1, 1, d), lambda b_, i: (b_, 0, 0))
    xrow = pl.BlockSpec((1, tm, d), lambda b_, i: (b_, i, 0))
    whole = lambda a: pl.BlockSpec(a.shape, lambda b_, i: (0,) * a.ndim, pipeline_mode=pl.Buffered(1))
    cb = conv_b.reshape(1, 2 * D_FF)
    g1, g2 = g_pre.reshape(1, d), g_post.reshape(1, d)
    y, tail = pl.pallas_call(
        _ffn_rows_kernel,
        out_shape=(jax.ShapeDtypeStruct((bsz, t, d), F32),
                   jax.ShapeDtypeStruct((bsz, t // tm, CONV_W - 1, 2 * D_FF), F32)),
        grid=(bsz, t // tm),
        in_specs=[xrow, whole(g1), mod, mod, mod, whole(g2), whole(w_up), whole(conv_w), whole(cb), whole(w_down)],
        out_specs=(xrow, pl.BlockSpec((1, 1, CONV_W - 1, 2 * D_FF), lambda b_, i: (b_, i, 0, 0))),
        scratch_shapes=[pltpu.VMEM((8, 2 * D_FF), F32)],
        compiler_params=_cparams(("arbitrary", "arbitrary")),
        name="ffn_rows",
    )(x, g1, sc, sh, gt, g2, w_up, conv_w, cb, w_down)
    return y, tail[:, -1]


LANE = 128
QUARTERS = 2 * KV_WIDTH // LANE
HEADS_PER_LANE_ROW = LANE // A_HEAD_DIM


def compress_weights(cmp_w1, cmp_pe, cmp_w2):
    eye = jnp.eye(HEADS_PER_LANE_ROW, dtype=F32)
    bd = lambda w: jnp.einsum("gh,...de->...gdhe", eye, w).reshape(w.shape[:-2] + (LANE, LANE))
    w1ab = jnp.concatenate([bd(cmp_w1[:, :CMP_STRIDE]), bd(cmp_w1[:, CMP_STRIDE:])], axis=-1).astype(BF16)
    w1ab = w1ab.reshape(2, CMP_STRIDE // 2, 2 * LANE, 2 * LANE)
    w2 = bd(cmp_w2).astype(BF16)
    w1r = cmp_w1.reshape(2, CMP_BLOCK * A_HEAD_DIM, A_HEAD_DIM)
    pe = cmp_pe.reshape(2, CMP_BLOCK * A_HEAD_DIM, 1)
    return w1ab, w2, w1r, pe


def _compress_chunk(get_x, nrows, w1_ref, w2_ref, w1r_ref, pe_ref, carry_ref, out_ref):
    row = lax.broadcasted_iota(jnp.int32, (nrows, 1), 0)
    for kind in range(2):
        peb = jnp.sum(pe_ref[kind] * w1r_ref[kind], axis=0, keepdims=True)
        peb = jnp.concatenate([peb] * HEADS_PER_LANE_ROW, axis=1)
        acc2 = jnp.zeros((2 * nrows, 2 * LANE), F32)
        for j in range(CMP_STRIDE // 2):
            x = jnp.concatenate([get_x(2 * kind, j), get_x(2 * kind + 1, j)], axis=0).astype(BF16)
            acc2 = acc2 + jnp.dot(x, w1_ref[kind, j], preferred_element_type=F32)
        for half in range(2):
            acc = acc2[half * nrows:(half + 1) * nrows]
            acc_a, acc_b = acc[:, :LANE], acc[:, LANE:]
            a_shift = jnp.where(row == 0, carry_ref[kind, half], pltpu.roll(acc_a, 1, 0))
            carry_ref[kind, half] = acc_a[nrows - 1:nrows]
            hid = jax.nn.gelu(a_shift + acc_b + peb)
            out_ref[0, kind, :, half * LANE:(half + 1) * LANE] = _dot(hid, w2_ref[kind])


def _cmp_prompt_kernel(kv0_ref, kv1_ref, kv2_ref, kv3_ref, w1_ref, w2_ref, w1r_ref, pe_ref, out_ref, carry_ref):
    kv_refs = (kv0_ref, kv1_ref, kv2_ref, kv3_ref)
    nseg = kv0_ref.shape[1] // CMP_STRIDE
    carry_ref[...] = jnp.zeros_like(carry_ref)
    offset = lambda quarter, s: kv_refs[quarter][0, pl.ds(s, nseg, stride=CMP_STRIDE), :]
    get_x = lambda quarter, j: jnp.concatenate([offset(quarter, 2 * j), offset(quarter, 2 * j + 1)], axis=1)
    _compress_chunk(get_x, nseg, w1_ref, w2_ref, w1r_ref, pe_ref, carry_ref, out_ref)


def _cmp_weight_specs():
    zero = lambda n: (lambda *_: (0,) * n)
    return [pl.BlockSpec((2, CMP_STRIDE // 2, 2 * LANE, 2 * LANE), zero(4)),
            pl.BlockSpec((2, LANE, LANE), zero(3)),
            pl.BlockSpec((2, CMP_BLOCK * A_HEAD_DIM, A_HEAD_DIM), zero(3)),
            pl.BlockSpec((2, CMP_BLOCK * A_HEAD_DIM, 1), zero(3))]


def compress_prompt(kv_cmp, cw):
    bsz, t, w = kv_cmp.shape
    nseg = t // CMP_STRIDE
    return pl.pallas_call(
        _cmp_prompt_kernel,
        out_shape=jax.ShapeDtypeStruct((bsz, 2, nseg, KV_WIDTH), F32),
        grid=(bsz,),
        in_specs=[pl.BlockSpec((1, t, LANE), functools.partial(lambda q, b_: (b_, 0, q), q)) for q in range(QUARTERS)]
                 + _cmp_weight_specs(),
        out_specs=pl.BlockSpec((1, 2, nseg, KV_WIDTH), lambda b_: (b_, 0, 0, 0)),
        scratch_shapes=[pltpu.VMEM((2, 2, 1, LANE), F32)],
        compiler_params=_cparams(("arbitrary",)),
        name="cmp_prompt",
    )(*([kv_cmp] * QUARTERS), *cw)


def feature_major_pool(cache):
    n_pool = cache.shape[0]
    return jnp.transpose(cache, (0, 2, 3, 4, 1)).reshape(n_pool, 2 * KV_WIDTH, PAGE_SIZE)


def _page_copy(pool_ref, dst, sem_ref, pt_ref, step, slot, k, n_chunks, pages):
    b_ = step // n_chunks
    c = step % n_chunks
    pid = pt_ref[b_, c * pages + k]
    return pltpu.make_async_copy(pool_ref.at[pid], dst(slot, k), sem_ref.at[slot])


def _page_pipeline(pool_ref, dst, sem_ref, pt_ref, n_chunks, pages):
    step = pl.program_id(0) * n_chunks + pl.program_id(1)
    total = pl.num_programs(0) * n_chunks
    slot = step % 2

    def start(st, sl):
        for k in range(pages):
            _page_copy(pool_ref, dst, sem_ref, pt_ref, st, sl, k, n_chunks, pages).start()

    @pl.when(step == 0)
    def _():
        start(step, slot)

    @pl.when(step + 1 < total)
    def _():
        start(step + 1, 1 - slot)

    for k in range(pages):
        _page_copy(pool_ref, dst, sem_ref, pt_ref, step, slot, k, n_chunks, pages).wait()
    return slot


def _cmp_sample_kernel(pt_ref, pool_ref, w1_ref, w2_ref, w1r_ref, pe_ref, out_ref,
                       buf_ref, sem_ref, x_ref, carry_ref, *, n_chunks, pages):
    slot = _page_pipeline(pool_ref, lambda sl, k: buf_ref.at[sl, k], sem_ref, pt_ref, n_chunks, pages)

    @pl.when(pl.program_id(1) == 0)
    def _():
        carry_ref[...] = jnp.zeros_like(carry_ref)

    segs = PAGE_SIZE // CMP_STRIDE
    nrows = pages * segs
    half_rows = PAGE_SIZE // 2
    dst = lax.broadcasted_iota(jnp.int32, (PAGE_SIZE, PAGE_SIZE), 0)
    src = lax.broadcasted_iota(jnp.int32, (PAGE_SIZE, PAGE_SIZE), 1)
    wanted = (dst % segs) * CMP_STRIDE + 2 * ((dst % half_rows) // segs) + dst // half_rows
    pick = jnp.where(src == wanted, 1.0, 0.0).astype(BF16)

    def relayout(p, carry):
        for quarter in range(QUARTERS):
            t = buf_ref[slot, p, quarter * LANE:(quarter + 1) * LANE, :]
            y = _dot_nt(pick, t)
            x_ref[quarter, p] = jnp.concatenate([y[:half_rows], y[half_rows:]], axis=1)
        return carry

    lax.fori_loop(0, pages, relayout, 0, unroll=16)

    def get_x(quarter, j):
        return x_ref[quarter, :, j * segs:(j + 1) * segs, :].reshape(nrows, 2 * LANE)

    _compress_chunk(get_x, nrows, w1_ref, w2_ref, w1r_ref, pe_ref, carry_ref, out_ref)


CMP_PAGES = 32
SLC_PAGES = 64


def compress_paged(pool, page_table, cw):
    dbs, n_pages = page_table.shape
    pages = math.gcd(n_pages, CMP_PAGES)
    n_chunks = n_pages // pages
    rows = pages * PAGE_SIZE // CMP_STRIDE
    return pl.pallas_call(
        functools.partial(_cmp_sample_kernel, n_chunks=n_chunks, pages=pages),
        out_shape=jax.ShapeDtypeStruct((dbs, 2, n_chunks * rows, KV_WIDTH), F32),
        grid_spec=pltpu.PrefetchScalarGridSpec(
            num_scalar_prefetch=1,
            grid=(dbs, n_chunks),
            in_specs=[pl.BlockSpec(memory_space=pl.ANY)] + _cmp_weight_specs(),
            out_specs=pl.BlockSpec((1, 2, rows, KV_WIDTH), lambda b_, c, pt: (b_, 0, c, 0)),
            scratch_shapes=[pltpu.VMEM((2, pages, 2 * KV_WIDTH, PAGE_SIZE), F32),
                            pltpu.SemaphoreType.DMA((2,)),
                            pltpu.VMEM((QUARTERS, pages, PAGE_SIZE // 2, 2 * LANE), F32),
                            pltpu.VMEM((2, 2, 1, LANE), F32)]),
        compiler_params=_cparams(("arbitrary", "arbitrary")),
        name="cmp_paged",
    )(page_table, pool, *cw)


def _masked_softmax_rows(s, mask):
    s = jnp.where(mask, s, NEG_INF)
    m = jnp.max(s, axis=-1, keepdims=True)
    m = jnp.where(m == NEG_INF, 0.0, m)
    e = jnp.exp(s - m)
    return e / jnp.maximum(jnp.sum(e, axis=-1, keepdims=True), TINY)


def _topk_mask(score, k, axis):
    n = score.shape[axis]
    idx = lax.broadcasted_iota(jnp.int32, score.shape, axis)
    sel = jnp.zeros(score.shape, F32)
    for _ in range(k):
        mx = jnp.max(score, axis=axis, keepdims=True)
        first = jnp.min(jnp.where(score == mx, idx, n), axis=axis, keepdims=True)
        pick = idx == first
        sel = jnp.where(pick, 1.0, sel)
        score = jnp.where(pick, NEG_INF, score)
    return sel


def _flash_tile(carry, s, v):
    m_old, l_old, acc = carry
    m_new = jnp.maximum(m_old, jnp.max(s, axis=-1, keepdims=True))
    m_safe = jnp.where(m_new == NEG_INF, 0.0, m_new)
    p = jnp.exp(s - m_safe)
    alpha = jnp.exp(m_old - m_safe)
    pv = v(p.astype(BF16)) if callable(v) else _dot(p, v)
    return m_new, alpha * l_old + jnp.sum(p, axis=-1, keepdims=True), alpha * acc + pv


MASKED = -(2.0 ** 100)
LOG2E = math.log2(math.e)


ONES_ROWS = 16
V_ROWS = A_HEAD_DIM + ONES_ROWS


def _flash_cols(carry, s, v_t):
    m_old, acc = carry
    m_new = jnp.maximum(m_old, jnp.max(s, axis=0, keepdims=True))
    return m_new, jnp.exp2(m_old - m_new) * acc + _dot(v_t, jnp.exp2(s - m_new))


def _flash_cols_init(cols):
    return jnp.full((1, cols), MASKED, F32), jnp.zeros((V_ROWS, cols), F32)


def _flash_cols_out(carry):
    acc = carry[1]
    return acc[:A_HEAD_DIM] / jnp.maximum(acc[A_HEAD_DIM:A_HEAD_DIM + 1], TINY)


def _flash_init(rows, dv):
    return jnp.full((rows, 1), NEG_INF, F32), jnp.zeros((rows, 1), F32), jnp.zeros((rows, dv), F32)


def _flash_out(carry):
    _, l, acc = carry
    return acc / jnp.maximum(l, TINY)


KEY_TILE = 512


def _nsa_prompt_kernel(q_ref, small_ref, cmp_ref, ks_ref, vs_ref, kw_ref, vw_ref, cov_ref, exp_ref, o_ref,
                       *, tq, n_slc):
    i = pl.program_id(1)
    q0 = i * tq
    tk = math.gcd(ks_ref.shape[1], KEY_TILE)
    cols = A_GROUP * tq
    q_t = (q_ref[0] * (A_HEAD_DIM ** -0.5 * LOG2E)).T.astype(BF16)
    gate_t = jax.nn.sigmoid(small_ref[0]).T
    ncmp = cmp_ref.shape[2]
    qpos = q0 + lax.broadcasted_iota(jnp.int32, (1, tq), 1)
    jcol = lax.broadcasted_iota(jnp.int32, (ncmp, 1), 0)
    cmp_mask = (jcol >= 1) & (jcol * CMP_STRIDE + (CMP_BLOCK - CMP_STRIDE - 1) <= qpos)
    blk = lax.broadcasted_iota(jnp.int32, (n_slc, 1), 0)
    cur = qpos // SLC_BLOCK
    forced = (blk == 0) | (blk == cur) | (blk == cur - 1)
    future = blk * SLC_BLOCK > qpos
    krow = lax.broadcasted_iota(jnp.int32, (tk, 1), 0)
    lanes = lambda a, r: a[:, r * tq:(r + 1) * tq]
    wk = min(WINDOW + tq, kw_ref.shape[1])
    w_off = pl.multiple_of(jnp.maximum(q0 - WINDOW, 0), tq)
    wpos = w_off + lax.broadcasted_iota(jnp.int32, (wk, 1), 0)
    band = (wpos <= qpos) & (wpos > qpos - WINDOW)
    pieces = []

    for g in range(A_KV_HEADS):
        gs = slice(g * A_HEAD_DIM, (g + 1) * A_HEAD_DIM)
        gv = slice(g * V_ROWS, (g + 1) * V_ROWS)
        head = lambda r: slice((g * A_GROUP + r) * A_HEAD_DIM, (g * A_GROUP + r + 1) * A_HEAD_DIM)
        qg = jnp.concatenate([q_t[head(r)] for r in range(A_GROUP)], axis=1)

        s = jnp.dot(kw_ref[0, pl.ds(w_off, wk), gs], qg, preferred_element_type=F32)
        s = jnp.concatenate([jnp.where(band, lanes(s, r), MASKED) for r in range(A_GROUP)], axis=1)
        o_w = _flash_cols_out(_flash_cols(_flash_cols_init(cols), s, vw_ref[0, gv, pl.ds(w_off, wk)]))

        s_c = _dot(cmp_ref[0, 0, :, gs], qg)
        p_r = []
        for r in range(A_GROUP):
            s_r = jnp.where(cmp_mask, lanes(s_c, r), NEG_INF)
            m = jnp.max(s_r, axis=0, keepdims=True)
            e = jnp.exp2(s_r - jnp.where(m == NEG_INF, 0.0, m))
            p_r.append(e * (1.0 / jnp.maximum(jnp.sum(e, axis=0, keepdims=True), TINY)))
        o_c = _dot_tn(cmp_ref[0, 1, :, gs], jnp.concatenate(p_r, axis=1))
        imp_t = _dot(cov_ref[...], p_r[0] + p_r[1] + p_r[2] + p_r[3])
        score = jnp.where(future, -BIG, imp_t + jnp.where(forced, BIG, 0.0))
        sel_t = _topk_mask(score, min(SLC_TOPK, n_slc), 0)
        sel_bias = jnp.concatenate([jnp.where(sel_t > 0.5, 0.0, MASKED),
                                    jnp.zeros((exp_ref.shape[1] - n_slc, tq), F32)], axis=0)
        sel_bias = jnp.concatenate([sel_bias] * A_GROUP, axis=1).astype(BF16)

        def slc_scores(off):
            return (jnp.dot(ks_ref[0, pl.ds(off, tk), gs], qg, preferred_element_type=F32) +
                    jnp.dot(exp_ref[pl.ds(off, tk), :], sel_bias, preferred_element_type=F32))

        def slc_step(j, carry):
            off = pl.multiple_of(j * tk, tk)
            return _flash_cols(carry, slc_scores(off), vs_ref[0, gv, pl.ds(off, tk)])

        j_last = (q0 + tq - 1) // tk
        carry = lax.fori_loop(0, j_last, slc_step, _flash_cols_init(cols))
        off = pl.multiple_of(j_last * tk, tk)
        causal = off + krow <= qpos
        s = slc_scores(off)
        s = jnp.concatenate([jnp.where(causal, lanes(s, r), MASKED) for r in range(A_GROUP)], axis=1)
        o_s = _flash_cols_out(_flash_cols(carry, s, vs_ref[0, gv, pl.ds(off, tk)]))

        for r in range(A_GROUP):
            row = SMALL_AG + 3 * (g * A_GROUP + r)
            pieces.append(gate_t[row:row + 1] * lanes(o_c, r) + gate_t[row + 1:row + 2] * lanes(o_s, r) +
                          gate_t[row + 2:row + 3] * lanes(o_w, r))
    o_ref[0] = jnp.concatenate(pieces, axis=0).T


def _coverage(n_cmp_rows, n_slc):
    cs = (np.arange(n_cmp_rows) - 1) * CMP_STRIDE
    ss = np.arange(n_slc) * SLC_BLOCK
    lo = np.maximum(cs[:, None], ss[None, :])
    hi = np.minimum(cs[:, None] + CMP_BLOCK, ss[None, :] + SLC_BLOCK)
    cov = np.clip(hi - lo, 0, None) / CMP_BLOCK
    cov[0] = 0.0
    return cov.astype(np.float32)


def _block_expand(n_rows, n_keys):
    return (np.arange(n_rows)[:, None] == (np.arange(n_keys) // SLC_BLOCK)[None, :]).astype(np.float32)


def nsa_prompt(q_rot, z, cmp, k_slc, vt_slc, k_win, vt_win, tq):
    bsz, t, _ = q_rot.shape
    n_slc = t // SLC_BLOCK
    ncmp = cmp.shape[2]
    assert vt_slc.shape[1] == A_KV_HEADS * V_ROWS

    cov_t = jnp.asarray(_coverage(ncmp, n_slc).T, BF16)
    n_exp = -(-n_slc // 128) * 128
    expand = jnp.asarray(_block_expand(n_exp, t).T, BF16)
    per_b = lambda shape: pl.BlockSpec((1,) + shape, lambda b_, i: (b_,) + (0,) * len(shape))
    return pl.pallas_call(
        functools.partial(_nsa_prompt_kernel, tq=tq, n_slc=n_slc),
        out_shape=jax.ShapeDtypeStruct((bsz, t, D_MODEL), F32),
        grid=(bsz, t // tq),
        in_specs=[pl.BlockSpec((1, tq, D_MODEL), lambda b_, i: (b_, i, 0)),
                  pl.BlockSpec((1, tq, 128), lambda b_, i: (b_, i, Z_SMALL // 128)),
                  per_b((2, ncmp, KV_WIDTH)),
                  per_b((t, KV_WIDTH)), per_b((A_KV_HEADS * V_ROWS, t)),
                  per_b((t, KV_WIDTH)), per_b((A_KV_HEADS * V_ROWS, t)),
                  pl.BlockSpec((n_slc, ncmp), lambda b_, i: (0, 0)),
                  pl.BlockSpec((t, n_exp), lambda b_, i: (0, 0))],
        out_specs=pl.BlockSpec((1, tq, D_MODEL), lambda b_, i: (b_, i, 0)),
        compiler_params=_cparams(("arbitrary", "arbitrary")),
        name="nsa_prompt",
    )(q_rot, z, cmp, k_slc, vt_slc, k_win, vt_win, cov_t, expand)


SAMPLE_ROWS = 8
NEW_KEYS = 128


def _nsa_sample_kernel(pt_ref, qbd_ref, gl_ref, cmp_ref, pool_ref, knew_ref, wcache_ref, wnew_ref, cov_ref, exp_ref,
                       o_ref, buf_ref, sem_ref, sel_ref, m_ref, l_ref, acc_ref, oc_ref,
                       *, n_chunks, pages, past, t_real, n_slc):
    c = pl.program_id(1)
    slot = _page_pipeline(pool_ref, lambda sl, k: buf_ref.at[sl, k], sem_ref, pt_ref, n_chunks, pages)
    qbd = qbd_ref[0]
    rows = qbd.shape[0]
    bpc = pages * PAGE_SIZE // SLC_BLOCK
    rq = lax.broadcasted_iota(jnp.int32, (rows, 1), 0) % t_real
    qpos = past + rq

    @pl.when(c == 0)
    def _():
        ncmp = cmp_ref.shape[2]
        nbp = cov_ref.shape[0]
        jrow = lax.broadcasted_iota(jnp.int32, (1, ncmp), 1)
        cmp_mask = (jrow >= 1) & (jrow * CMP_STRIDE + (CMP_BLOCK - CMP_STRIDE - 1) <= qpos)
        p = _masked_softmax_rows(_dot_nt(qbd, cmp_ref[0, 0]), cmp_mask)
        oc_ref[...] = _dot(p, cmp_ref[0, 1])
        ri = lax.broadcasted_iota(jnp.int32, (rows, rows), 0)
        ci = lax.broadcasted_iota(jnp.int32, (rows, rows), 1)
        group_rows = A_GROUP * t_real
        same = (ri // group_rows == ci // group_rows) & (ri % t_real == ci % t_real)
        p_group = _dot(jnp.where(same, 1.0, 0.0), p)
        imp_t = _dot_nt(cov_ref[...], p_group)
        blk = lax.broadcasted_iota(jnp.int32, (nbp, 1), 0)
        qpos_r = past + lax.broadcasted_iota(jnp.int32, (1, rows), 1) % t_real
        cur = qpos_r // SLC_BLOCK
        forced = (blk == 0) | (blk == cur) | (blk == cur - 1)
        future = blk * SLC_BLOCK > qpos_r
        score = jnp.where(future, -BIG, imp_t + jnp.where(forced, BIG, 0.0))
        score = jnp.where(blk < n_slc, score, NEG_INF)
        sel_t = _topk_mask(score, min(SLC_TOPK, n_slc), 0)
        sel = _dot_nt(jnp.where(ri == ci, 1.0, 0.0), sel_t)
        for cc in range(n_chunks + 1):
            chunk_sel = sel[:, cc * bpc:(cc + 1) * bpc]
            if bpc < 128:
                chunk_sel = jnp.concatenate([chunk_sel, jnp.zeros((rows, 128 - bpc), F32)], axis=1)
            sel_ref[cc] = chunk_sel.astype(BF16)
        m_ref[...] = jnp.full(m_ref.shape, NEG_INF, F32)
        l_ref[...] = jnp.zeros_like(l_ref)
        acc_ref[...] = jnp.zeros_like(acc_ref)

    scores = jnp.concatenate([_dot(qbd, buf_ref[slot, k, 0:KV_WIDTH, :]) for k in range(pages)], axis=1)
    picked = jnp.dot(sel_ref[c], exp_ref[...], preferred_element_type=F32) > 0.5
    s = jnp.where(picked, scores, NEG_INF)

    def values(p):
        return sum(_dot_nt(p[:, k * PAGE_SIZE:(k + 1) * PAGE_SIZE], buf_ref[slot, k, KV_WIDTH:2 * KV_WIDTH, :])
                   for k in range(pages))

    carry = _flash_tile((m_ref[...], l_ref[...], acc_ref[...]), s, values)
    m_ref[...], l_ref[...], acc_ref[...] = carry

    @pl.when(c == n_chunks - 1)
    def _():
        zpad = jnp.zeros((NEW_KEYS - SAMPLE_ROWS, 2 * KV_WIDTH), F32)
        kcol = lax.broadcasted_iota(jnp.int32, (1, NEW_KEYS), 1)
        new_mask = (kcol <= rq) & (kcol < t_real)
        knew = jnp.concatenate([knew_ref[0], zpad], axis=0)
        last_picked = sel_ref[n_chunks][:, 0:1].astype(F32) > 0.5
        s_new = jnp.where(new_mask & last_picked, _dot_nt(qbd, knew[:, :KV_WIDTH]), NEG_INF)
        o_s = _flash_out(_flash_tile((m_ref[...], l_ref[...], acc_ref[...]), s_new, knew[:, KV_WIDTH:]))

        wc = wcache_ref[0]
        wb = wc.shape[0]
        wcol = lax.broadcasted_iota(jnp.int32, (1, wb), 1)
        s_w = jnp.where(wcol > rq + (wb - WINDOW), _dot_nt(qbd, wc[:, :KV_WIDTH]), NEG_INF)
        cw = _flash_tile(_flash_init(rows, KV_WIDTH), s_w, wc[:, KV_WIDTH:])
        wnew = jnp.concatenate([wnew_ref[0], zpad], axis=0)
        s_wn = jnp.where(new_mask, _dot_nt(qbd, wnew[:, :KV_WIDTH]), NEG_INF)
        o_w = _flash_out(_flash_tile(cw, s_wn, wnew[:, KV_WIDTH:]))

        gate = jax.nn.sigmoid(gl_ref[0])
        o = gate[:, 0:1] * oc_ref[...] + gate[:, 1:2] * o_s + gate[:, 2:3] * o_w
        lane_g = lax.broadcasted_iota(jnp.int32, (1, KV_WIDTH), 1) // A_HEAD_DIM
        row_g = lax.broadcasted_iota(jnp.int32, (rows, 1), 0) // (A_GROUP * t_real)
        o = jnp.where(lane_g == row_g, o, 0.0)
        o_ref[0] = sum(o[:, g * A_HEAD_DIM:(g + 1) * A_HEAD_DIM] for g in range(A_KV_HEADS))


def nsa_sample(q_rot, z, cmp, pool, page_table, k_new, win_cache, w_new, past, t_real):
    dbs = q_rot.shape[0]
    n_pages = page_table.shape[1]
    pages = math.gcd(n_pages, SLC_PAGES)
    n_chunks = n_pages // pages
    bpc = pages * PAGE_SIZE // SLC_BLOCK
    ncmp = cmp.shape[2]
    n_slc = -(-(past + t_real) // SLC_BLOCK)
    assert n_slc == n_chunks * bpc + 1 and bpc <= 128
    nbp = -(-((n_chunks + 1) * bpc) // 128) * 128
    rows = A_HEADS * t_real
    q5 = q_rot[:, :t_real].reshape(dbs, t_real, A_KV_HEADS, A_GROUP, A_HEAD_DIM) * (A_HEAD_DIM ** -0.5)
    qbd = jnp.einsum("bqgrd,gh->bgrqhd", q5, jnp.eye(A_KV_HEADS, dtype=F32)).reshape(dbs, rows, KV_WIDTH).astype(BF16)
    gl = z[:, :t_real, Z_SMALL + SMALL_AG:Z_SMALL + SMALL_AG + 3 * A_HEADS].reshape(dbs, t_real, A_HEADS, 3)
    gl = jnp.swapaxes(gl, 1, 2).reshape(dbs, rows, 3)
    cov = np.zeros((nbp, ncmp), np.float32)
    cov[:n_slc] = _coverage(ncmp, n_slc).T
    expand = jnp.asarray(_block_expand(128, pages * PAGE_SIZE), BF16)
    per_b = lambda shape: pl.BlockSpec((1,) + shape, lambda b_, c, pt: (b_,) + (0,) * len(shape))
    const = lambda shape: pl.BlockSpec(shape, lambda b_, c, pt: (0,) * len(shape))
    wb = win_cache.shape[1]
    out = pl.pallas_call(
        functools.partial(_nsa_sample_kernel, n_chunks=n_chunks, pages=pages, past=past, t_real=t_real, n_slc=n_slc),
        out_shape=jax.ShapeDtypeStruct((dbs, rows, A_HEAD_DIM), F32),
        grid_spec=pltpu.PrefetchScalarGridSpec(
            num_scalar_prefetch=1,
            grid=(dbs, n_chunks),
            in_specs=[per_b((rows, KV_WIDTH)), per_b((rows, 3)), per_b((2, ncmp, KV_WIDTH)),
                      pl.BlockSpec(memory_space=pl.ANY),
                      per_b((SAMPLE_ROWS, 2 * KV_WIDTH)), per_b((wb, 2 * KV_WIDTH)), per_b((SAMPLE_ROWS, 2 * KV_WIDTH)),
                      const((nbp, ncmp)), const((128, pages * PAGE_SIZE))],
            out_specs=per_b((rows, A_HEAD_DIM)),
            scratch_shapes=[pltpu.VMEM((2, pages, 2 * KV_WIDTH, PAGE_SIZE), F32),
                            pltpu.SemaphoreType.DMA((2,)),
                            pltpu.VMEM((n_chunks + 1, rows, 128), BF16),
                            pltpu.VMEM((rows, 1), F32), pltpu.VMEM((rows, 1), F32),
                            pltpu.VMEM((rows, KV_WIDTH), F32), pltpu.VMEM((rows, KV_WIDTH), F32)]),
        compiler_params=_cparams(("arbitrary", "arbitrary")),
        name="nsa_sample",
    )(page_table, qbd, gl, cmp, pool, k_new, win_cache, w_new, jnp.asarray(cov, BF16), expand)
    out = jnp.swapaxes(out.reshape(dbs, A_HEADS, t_real, A_HEAD_DIM), 1, 2).reshape(dbs, t_real, D_MODEL)
    return jnp.pad(out, ((0, 0), (0, SAMPLE_ROWS - t_real), (0, 0)))


def _kv_rows(a, bsz, t):
    return a.reshape(bsz, t, 2, A_KV_HEADS, A_HEAD_DIM)


def kernel(x_prompt, x_sample, cache_cmp_kv, cache_slc_kv, cache_win_kv, state_C, state_n, state_m, state_conv,
           page_table, c_prompt, c_sample, w_ada, b_ada, g_pre_mix, g_post_mix, g_pre_ffn, g_post_ffn, w_in, b_in,
           m_norm_w, cmp_w1, cmp_pe, cmp_w2, w_branch_m, w_branch_a, w_out, w_up, conv_w, conv_b, w_down):
    depth = w_ada.shape[0]
    bsz, t, d = x_prompt.shape
    dbs, ts, _ = x_sample.shape
    n_pages = page_table.shape[1]
    past = n_pages * PAGE_SIZE
    assert ts <= SAMPLE_ROWS and (past + ts) // CMP_STRIDE == past // CMP_STRIDE and past >= WINDOW
    srows = dbs * SAMPLE_ROWS
    xp = x_prompt.astype(F32)
    xs = jnp.pad(x_sample.astype(F32), ((0, 0), (0, SAMPLE_ROWS - ts), (0, 0))).reshape(1, srows, d)
    c_all = jnp.concatenate([c_prompt, c_sample], axis=0).astype(F32)
    c_all = jnp.pad(c_all, ((0, (-c_all.shape[0]) % 8), (0, 0)))
    tab_p = rope_tables(jnp.arange(t, dtype=jnp.int32))
    tab_s = rope_tables(jnp.tile(past + jnp.arange(SAMPLE_ROWS, dtype=jnp.int32), dbs))
    lchunk = math.gcd(t, 256)
    tm_p = math.gcd(t, 512)
    p_states, s_states = [], []
    for l in range(depth):
        mod = ada_modulation(c_all, w_ada[l], b_ada[l])
        mod_p = [m[:, None, :] for m in jnp.split(mod[:bsz], 6, axis=-1)]
        mod_s = [jnp.repeat(m, SAMPLE_ROWS, axis=0)[None] for m in jnp.split(mod[bsz:bsz + dbs], 6, axis=-1)]
        w_r, b_r = regroup_in_weights(w_in[l], b_in[l])
        cw = compress_weights(cmp_w1[l], cmp_pe[l], cmp_w2[l])
        wm, wa, wo = w_branch_m[l].astype(BF16), w_branch_a[l].astype(BF16), w_out[l].astype(BF16)
        wu, wd = w_up[l].astype(BF16), w_down[l].astype(BF16)

        sh_m, sc_m, gt_m, sh_f, sc_f, gt_f = mod_p
        z = in_projection(xp, g_pre_mix[l], sc_m, sh_m, w_r, b_r, tm=math.gcd(t, 1024))
        q_rot, kv_cmp, kv_slc, kv_win, k_slc, vt_slc, k_win, vt_win = rope_split(z, tab_p, tm=tm_p, attn_layouts=True)
        hm, p_c, p_n, p_m = mlstm(z, m_norm_w[l], jnp.zeros((bsz, M_HEADS, M_HEAD_DIM, M_HEAD_DIM), F32),
                                  jnp.zeros((bsz, M_HEADS, M_HEAD_DIM), F32), jnp.zeros((bsz, M_HEADS), F32),
                                  lb=lchunk, lp=lchunk, t_real=lchunk)
        ha = nsa_prompt(q_rot, z, compress_prompt(kv_cmp, cw), k_slc, vt_slc, k_win, vt_win, tq=256)
        xp = mix_out(xp, hm, ha, z, gt_m, g_post_mix[l], wm, wa, wo, tm=tm_p)
        xp, p_conv = conv_ffn_rows(xp, g_pre_ffn[l], sc_f, sh_f, gt_f, g_post_ffn[l], wu, conv_w[l], conv_b[l], wd,
                                   tm=tm_p)
        wkeep = min(WINDOW, t)
        p_states.append((_kv_rows(kv_cmp, bsz, t), _kv_rows(kv_slc, bsz, t), _kv_rows(kv_win[:, t - wkeep:], bsz, wkeep),
                         p_c, p_n, p_m, p_conv))

        sh_m, sc_m, gt_m, sh_f, sc_f, gt_f = mod_s
        z = in_projection(xs, g_pre_mix[l], sc_m, sh_m, w_r, b_r, tm=srows)
        q_rot, kv_cmp, kv_slc, kv_win = rope_split(z, tab_s, tm=srows)
        z3 = z.reshape(dbs, SAMPLE_ROWS, Z_WIDTH)
        hm, s_c, s_n, s_m = mlstm(z3, m_norm_w[l], state_C[l].astype(F32), state_n[l].astype(F32),
                                  state_m[l].astype(F32), lb=SAMPLE_ROWS, lp=128, t_real=ts)
        cmp_s = compress_paged(feature_major_pool(cache_cmp_kv[l].astype(F32)), page_table, cw)
        new3 = lambda a: a.reshape(dbs, SAMPLE_ROWS, 2 * KV_WIDTH)
        win_cache = cache_win_kv[l].astype(F32).reshape(dbs, -1, 2 * KV_WIDTH)
        ha = nsa_sample(q_rot.reshape(dbs, SAMPLE_ROWS, d), z3, cmp_s, feature_major_pool(cache_slc_kv[l].astype(F32)),
                        page_table, new3(kv_slc), win_cache, new3(kv_win), past, ts)
        xs = mix_out(xs, hm.reshape(1, srows, d), ha.reshape(1, srows, d), z, gt_m, g_post_mix[l], wm, wa, wo, tm=srows)
        st = state_conv[l].astype(F32)
        s2 = jnp.pad(st, ((0, 0), (0, SAMPLE_ROWS - (CONV_W - 1)), (0, 0))).reshape(1, srows, 2 * D_FF)
        s1 = jnp.pad(st[:, 1:], ((0, 0), (0, SAMPLE_ROWS - 1), (0, 0))).reshape(1, srows, 2 * D_FF)
        xs, u = conv_ffn(xs, g_pre_ffn[l], sc_f, sh_f, gt_f, g_post_ffn[l], wu, conv_w[l], conv_b[l], wd, tm=srows,
                         state_rows=(s1, s2))
        wb = win_cache.shape[1]
        s_win = jnp.concatenate([win_cache, new3(kv_win)[:, :ts]], axis=1)[:, ts:]
        s_conv = jnp.concatenate([st, u.reshape(dbs, SAMPLE_ROWS, 2 * D_FF)[:, :ts]], axis=1)[:, ts:]
        s_states.append((_kv_rows(new3(kv_cmp)[:, :ts], dbs, ts), _kv_rows(new3(kv_slc)[:, :ts], dbs, ts),
                         _kv_rows(s_win, dbs, wb), s_c, s_n, s_m, s_conv))

    stack = lambda states: [jnp.stack([s[i] for s in states]) for i in range(7)]
    y_sample = xs.reshape(dbs, SAMPLE_ROWS, d)[:, :ts]
    return (xp, y_sample, *stack(p_states), *stack(s_states))
```
